```python
import jax, jax.numpy as jnp
from jax import lax
import numpy as np

D_MODEL = 2048
BATCH = 8
SEQ = 2048
DEPTH = 1

D_MIX = D_MODEL
D_ATT = D_MIX // 2
D_RNN = D_MIX - D_ATT
N_Q_HEADS = 16
N_KV_GROUPS = 4
HEADS_PER_GROUP = N_Q_HEADS // N_KV_GROUPS
HEAD_DIM = D_ATT // N_Q_HEADS
D_KV = N_KV_GROUPS * HEAD_DIM
CMP_BLOCK = 32
CMP_STRIDE = 16
CMP_HIDDEN = 4 * HEAD_DIM
SEL_BLOCK = 64
SEL_TOP_N = 8
WINDOW = 512
Q_CHUNK = 128
N_BRANCH = 3
RNN_BLOCKS = 8
RNN_BLOCK_DIM = D_RNN // RNN_BLOCKS
CONV_WIDTH = 4
LRU_C = 8.0
D_FF = 4 * D_MODEL
EPS = 1e-6
NEG = -1e30
FORCE_SCORE = 1e9

SPLIT_SIZES = [D_ATT] + [D_KV] * 6 + [N_BRANCH * N_Q_HEADS, D_RNN, D_RNN]
D_IN = sum(SPLIT_SIZES)
SPLIT_POINTS = [int(v) for v in np.cumsum(SPLIT_SIZES)[:-1]]

kernel_name = "hymba_nsa_rglru_sandwich_adaln_block"


def rms_norm(x, g):
    xf = x.astype(jnp.float32)
    y = xf * lax.rsqrt(jnp.mean(xf * xf, axis=-1, keepdims=True) + EPS)
    return (y * g.astype(jnp.float32)).astype(x.dtype)


def masked_softmax(s, mask):
    p = jax.nn.softmax(jnp.where(mask, s.astype(jnp.float32), NEG), axis=-1)
    return jnp.where(mask, p, 0.0)


def compress_kv(k, w1, w2, pe):
    B, S = k.shape[0], k.shape[1]
    n_cmp = (S - CMP_BLOCK) // CMP_STRIDE + 1
    idx = jnp.arange(n_cmp)[:, None] * CMP_STRIDE + jnp.arange(CMP_BLOCK)[None, :]
    blk = k[:, idx] + pe[None, None, :, None, :]
    blk = jnp.moveaxis(blk, 2, 3).reshape(B, n_cmp, N_KV_GROUPS, CMP_BLOCK * HEAD_DIM)
    return jax.nn.gelu(blk @ w1) @ w2


def cmp_to_sel_weights(n_cmp, n_sel):
    c0 = np.arange(n_cmp)[:, None] * CMP_STRIDE
    s0 = np.arange(n_sel)[None, :] * SEL_BLOCK
    ov = np.minimum(c0 + CMP_BLOCK, s0 + SEL_BLOCK) - np.maximum(c0, s0)
    return jnp.asarray(np.clip(ov, 0, None).astype(np.float32) / CMP_BLOCK)


def nsa_attention(q, kc, vc, ks, vs, kw, vw, gate_logits, w1k, w2k, pek, w1v, w2v, pev):
    B, S = q.shape[0], q.shape[1]
    G, Hg, Dh = N_KV_GROUPS, HEADS_PER_GROUP, HEAD_DIM
    q = q.reshape(B, S, G, Hg, Dh) * (Dh ** -0.5)
    kc, vc, ks, vs, kw, vw = [t.reshape(B, S, G, Dh) for t in (kc, vc, ks, vs, kw, vw)]
    pos = jnp.arange(S)

    k_cmp = compress_kv(kc, w1k, w2k, pek)
    v_cmp = compress_kv(vc, w1v, w2v, pev)
    n_cmp = k_cmp.shape[1]
    cmp_end = jnp.arange(n_cmp) * CMP_STRIDE + CMP_BLOCK - 1
    mask_c = (cmp_end[None, :] <= pos[:, None])[None, :, None, None, :]
    s_c = jnp.einsum('bsghd,bngd->bsghn', q, k_cmp)
    p_c = masked_softmax(s_c, mask_c)
    o_cmp = jnp.einsum('bsghn,bngd->bsghd', p_c.astype(v_cmp.dtype), v_cmp)

    n_sel = S // SEL_BLOCK
    imp = jnp.einsum('bsghn,nj->bsgj', p_c, cmp_to_sel_weights(n_cmp, n_sel))
    blk = jnp.arange(n_sel)[None, :]
    cur = (pos // SEL_BLOCK)[:, None]
    forced = (blk == 0) | (blk == cur) | (blk == cur - 1)
    valid = blk * SEL_BLOCK <= pos[:, None]
    imp = jnp.where(forced[None, :, None, :], FORCE_SCORE,
                    jnp.where(valid[None, :, None, :], imp, -FORCE_SCORE))
    top_n = min(SEL_TOP_N, n_sel)
    _, sel_idx = lax.top_k(imp, top_n)

    ks_blk = ks.reshape(B, n_sel, SEL_BLOCK, G, Dh).transpose(0, 3, 1, 2, 4)
    vs_blk = vs.reshape(B, n_sel, SEL_BLOCK, G, Dh).transpose(0, 3, 1, 2, 4)
    kw_pad = jnp.pad(kw, ((0, 0), (WINDOW, 0), (0, 0), (0, 0)))
    vw_pad = jnp.pad(vw, ((0, 0), (WINDOW, 0), (0, 0), (0, 0)))
    b_ix = jnp.arange(B)[:, None, None, None]
    g_ix = jnp.arange(G)[None, None, :, None]
    n_keys_sel = top_n * SEL_BLOCK

    def query_block(ci):
        start = ci * Q_CHUNK
        tq = start + jnp.arange(Q_CHUNK)
        qc = lax.dynamic_slice_in_dim(q, start, Q_CHUNK, axis=1)
        idx = lax.dynamic_slice_in_dim(sel_idx, start, Q_CHUNK, axis=1)
        k_sel = ks_blk[b_ix, g_ix, idx].reshape(B, Q_CHUNK, G, n_keys_sel, Dh)
        v_sel = vs_blk[b_ix, g_ix, idx].reshape(B, Q_CHUNK, G, n_keys_sel, Dh)
        kpos = (idx[..., None] * SEL_BLOCK + jnp.arange(SEL_BLOCK)).reshape(B, Q_CHUNK, G, n_keys_sel)
        mask_s = (kpos <= tq[None, :, None, None])[:, :, :, None, :]
        p_s = masked_softmax(jnp.einsum('bcghd,bcgnd->bcghn', qc, k_sel), mask_s)
        o_s = jnp.einsum('bcghn,bcgnd->bcghd', p_s.astype(v_sel.dtype), v_sel)
        k_w = lax.dynamic_slice_in_dim(kw_pad, start, Q_CHUNK + WINDOW, axis=1)
        v_w = lax.dynamic_slice_in_dim(vw_pad, start, Q_CHUNK + WINDOW, axis=1)
        wpos = start - WINDOW + jnp.arange(Q_CHUNK + WINDOW)
        mask_w = ((wpos[None, :] <= tq[:, None]) & (wpos[None, :] > tq[:, None] - WINDOW)
                  & (wpos[None, :] >= 0))[None, :, None, None, :]
        p_w = masked_softmax(jnp.einsum('bcghd,bkgd->bcghk', qc, k_w), mask_w)
        o_w = jnp.einsum('bcghk,bkgd->bcghd', p_w.astype(v_w.dtype), v_w)
        return o_s, o_w

    o_sel, o_win = lax.map(query_block, jnp.arange(S // Q_CHUNK))
    o_sel = jnp.moveaxis(o_sel, 0, 1).reshape(B, S, G, Hg, Dh)
    o_win = jnp.moveaxis(o_win, 0, 1).reshape(B, S, G, Hg, Dh)

    g = jax.nn.sigmoid(gate_logits).reshape(B, S, G, Hg, N_BRANCH)
    o = g[..., 0:1] * o_cmp + g[..., 1:2] * o_sel + g[..., 2:3] * o_win
    return o.reshape(B, S, D_ATT)


def causal_depthwise_conv(x, w, b):
    S = x.shape[1]
    xp = jnp.pad(x, ((0, 0), (CONV_WIDTH - 1, 0), (0, 0)))
    return b + sum(xp[:, k:k + S] * w[k] for k in range(CONV_WIDTH))


def block_diag_linear(x, w, b):
    B, S = x.shape[0], x.shape[1]
    xb = x.reshape(B, S, RNN_BLOCKS, RNN_BLOCK_DIM)
    return jnp.einsum('bsni,nij->bsnj', xb, w).reshape(B, S, D_RNN) + b


def rg_lru(x, w_a, b_a, w_x, b_x, lam):
    xf = x.astype(jnp.float32)
    r = jax.nn.sigmoid(block_diag_linear(x, w_a, b_a).astype(jnp.float32))
    i = jax.nn.sigmoid(block_diag_linear(x, w_x, b_x).astype(jnp.float32))
    log_a = -LRU_C * r * jax.nn.softplus(-lam.astype(jnp.float32))
    a = jnp.exp(log_a)
    bterm = jnp.sqrt(-jnp.expm1(2.0 * log_a)) * (i * xf)

    def combine(lhs, rhs):
        a1, b1 = lhs
        a2, b2 = rhs
        return a1 * a2, a2 * b1 + b2

    _, h = lax.associative_scan(combine, (a, bterm), axis=1)
    return h.astype(x.dtype)


def setup_inputs(seed: int = 0) -> dict:
    key = jax.random.key(seed)
    ks = jax.random.split(key, 32)
    nrm = lambda k, shape, s: jax.random.normal(k, shape, jnp.float32) * s
    L = DEPTH
    u = jax.random.uniform(ks[21], (L, D_RNN), jnp.float32, 0.9, 0.999)
    a0 = u ** (1.0 / LRU_C)
    return {
        "x": nrm(ks[0], (BATCH, SEQ, D_MODEL), 1.0),
        "c": nrm(ks[1], (BATCH, D_MODEL), 1.0),
        "w_ada": nrm(ks[2], (L, D_MODEL, 6 * D_MODEL), D_MODEL ** -0.5),
        "b_ada": nrm(ks[3], (L, 6 * D_MODEL), 0.01),
        "g_pre_mix": 1.0 + nrm(ks[4], (L, D_MODEL), 0.01),
        "g_post_mix": 1.0 + nrm(ks[5], (L, D_MODEL), 0.01),
        "g_pre_mlp": 1.0 + nrm(ks[6], (L, D_MODEL), 0.01),
        "g_post_mlp": 1.0 + nrm(ks[7], (L, D_MODEL), 0.01),
        "w_in": nrm(ks[8], (L, D_MODEL, D_IN), D_MODEL ** -0.5),
        "cmp_w1_k": nrm(ks[9], (L, CMP_BLOCK * HEAD_DIM, CMP_HIDDEN), (CMP_BLOCK * HEAD_DIM) ** -0.5),
        "cmp_w2_k": nrm(ks[10], (L, CMP_HIDDEN, HEAD_DIM), CMP_HIDDEN ** -0.5),
        "cmp_pe_k": nrm(ks[11], (L, CMP_BLOCK, HEAD_DIM), 0.1),
        "cmp_w1_v": nrm(ks[12], (L, CMP_BLOCK * HEAD_DIM, CMP_HIDDEN), (CMP_BLOCK * HEAD_DIM) ** -0.5),
        "cmp_w2_v": nrm(ks[13], (L, CMP_HIDDEN, HEAD_DIM), CMP_HIDDEN ** -0.5),
        "cmp_pe_v": nrm(ks[14], (L, CMP_BLOCK, HEAD_DIM), 0.1),
        "conv_w": nrm(ks[15], (L, CONV_WIDTH, D_RNN), CONV_WIDTH ** -0.5),
        "conv_b": nrm(ks[16], (L, D_RNN), 0.01),
        "w_rg_a": nrm(ks[17], (L, RNN_BLOCKS, RNN_BLOCK_DIM, RNN_BLOCK_DIM), RNN_BLOCK_DIM ** -0.5),
        "b_rg_a": nrm(ks[18], (L, D_RNN), 0.01),
        "w_rg_x": nrm(ks[19], (L, RNN_BLOCKS, RNN_BLOCK_DIM, RNN_BLOCK_DIM), RNN_BLOCK_DIM ** -0.5),
        "b_rg_x": nrm(ks[20], (L, D_RNN), 0.01),
        "lru_lambda": jnp.log(a0) - jnp.log1p(-a0),
        "g_grp_att": 1.0 + nrm(ks[22], (L, D_ATT), 0.01),
        "g_grp_rnn": 1.0 + nrm(ks[23], (L, D_RNN), 0.01),
        "w_out": nrm(ks[24], (L, D_MIX, D_MODEL), D_MIX ** -0.5),
        "w_ff1": nrm(ks[25], (L, D_MODEL, D_FF), D_MODEL ** -0.5),
        "w_ff2": nrm(ks[26], (L, D_FF, D_MODEL), D_FF ** -0.5),
    }


def reference(x, c, w_ada, b_ada, g_pre_mix, g_post_mix, g_pre_mlp, g_post_mlp, w_in,
              cmp_w1_k, cmp_w2_k, cmp_pe_k, cmp_w1_v, cmp_w2_v, cmp_pe_v,
              conv_w, conv_b, w_rg_a, b_rg_a, w_rg_x, b_rg_x, lru_lambda,
              g_grp_att, g_grp_rnn, w_out, w_ff1, w_ff2):
    c_act = jax.nn.silu(c)
    for l in range(DEPTH):
        mod = c_act @ w_ada[l] + b_ada[l]
        sh1, sc1, gt1, sh2, sc2, gt2 = [m[:, None, :] for m in jnp.split(mod, 6, axis=-1)]

        h = rms_norm(x, g_pre_mix[l]) * (1.0 + sc1) + sh1
        proj = h @ w_in[l]
        q, kc, vc, ksl, vsl, kw, vw, gl, xr, yr = jnp.split(proj, SPLIT_POINTS, axis=-1)
        o_att = nsa_attention(q, kc, vc, ksl, vsl, kw, vw, gl,
                              cmp_w1_k[l], cmp_w2_k[l], cmp_pe_k[l],
                              cmp_w1_v[l], cmp_w2_v[l], cmp_pe_v[l])
        xr = causal_depthwise_conv(xr, conv_w[l], conv_b[l])
        hr = rg_lru(xr, w_rg_a[l], b_rg_a[l], w_rg_x[l], b_rg_x[l], lru_lambda[l])
        o_rnn = jax.nn.gelu(yr) * hr
        mix = jnp.concatenate([rms_norm(o_att, g_grp_att[l]),
                               rms_norm(o_rnn, g_grp_rnn[l])], axis=-1) @ w_out[l]
        x = x + gt1 * rms_norm(mix, g_post_mix[l])

        h = rms_norm(x, g_pre_mlp[l]) * (1.0 + sc2) + sh2
        f = jnp.square(jax.nn.relu(h @ w_ff1[l])) @ w_ff2[l]
        x = x + gt2 * rms_norm(f, g_post_mlp[l])
    return x
```

```python
import functools

import jax
import jax.numpy as jnp
from jax import lax
from jax.experimental import pallas as pl
from jax.experimental.pallas import tpu as pltpu

F32 = jnp.float32
BF16 = jnp.bfloat16

D_MODEL = 2048
D_ATT = 1024
D_RNN = 1024
N_Q_HEADS = 16
N_KV_GROUPS = 4
HEADS_PER_GROUP = 4
HEAD_DIM = 64
D_KV = 256
CMP_BLOCK = 32
CMP_STRIDE = 16
SEL_BLOCK = 64
SEL_SHIFT = 6
SEL_TOP_N = 8
WINDOW = 512
N_BRANCH = 3
RNN_BLOCKS = 8
RNN_BLOCK_DIM = 128
CONV_WIDTH = 4
LRU_C = 8.0
D_FF = 4 * D_MODEL
EPS = 1e-6
NEG = -1e30
FORCE_SCORE = 1e9

LANES = 128
SUBLANES = 8
VMEM_LIMIT = 56 * 1024 * 1024

TQ = 256
N_CMP_PAD = 128
N_SEL = 32
GL_PAD = 128
TT = 256


def _params(sem):
    return pltpu.CompilerParams(dimension_semantics=sem, vmem_limit_bytes=VMEM_LIMIT)


def _dot(a, b):
    return jnp.dot(a, b, preferred_element_type=F32)


def _rms(x, g):
    return x * lax.rsqrt(jnp.mean(x * x, axis=-1, keepdims=True) + EPS) * g


def _ada_kernel(c_ref, w_ref, b_ref, o_ref):
    c = c_ref[...]
    ca = (c * jax.nn.sigmoid(c)).astype(BF16)
    o_ref[...] = _dot(ca, w_ref[...].astype(BF16)) + b_ref[...]


def _ada(c, w, b):
    bsz, d = c.shape
    n = w.shape[1]
    tn = 1024
    return pl.pallas_call(
        _ada_kernel,
        grid=(n // tn,),
        in_specs=[
            pl.BlockSpec((bsz, d), lambda j: (0, 0)),
            pl.BlockSpec((d, tn), lambda j: (0, j)),
            pl.BlockSpec((1, tn), lambda j: (0, j)),
        ],
        out_specs=pl.BlockSpec((bsz, tn), lambda j: (0, j)),
        out_shape=jax.ShapeDtypeStruct((bsz, n), F32),
        compiler_params=_params(("arbitrary",)),
        name="ada_mod",
    )(c, w, b.reshape(1, n))


C_Q = (0, D_ATT)
C_CMP = (C_Q[1], C_Q[1] + 2 * D_KV)
C_SW = (C_CMP[1], C_CMP[1] + 4 * D_KV)
C_R = (C_SW[1], C_SW[1] + 2 * D_RNN)
C_GL = (C_R[1], C_R[1] + GL_PAD)
D_IN_PAD = C_GL[1]


def _inproj_kernel(x_ref, g_ref, sc_ref, sh_ref, w_ref, q_ref, cmp_ref, sw_ref, r_ref, gl_ref):
    x = x_ref[0]
    h = _rms(x, g_ref[...]) * (1.0 + sc_ref[0, 0]) + sh_ref[0, 0]
    hb = h.astype(BF16)
    q_ref[0] = (_dot(hb, w_ref[:, C_Q[0]:C_Q[1]]) * (HEAD_DIM ** -0.5)).astype(BF16)
    cmp_ref[0] = _dot(hb, w_ref[:, C_CMP[0]:C_CMP[1]])
    sw_ref[0] = _dot(hb, w_ref[:, C_SW[0]:C_SW[1]]).astype(BF16)
    half = (C_R[0] + C_R[1]) // 2
    r_ref[0, :, 0:D_RNN] = _dot(hb, w_ref[:, C_R[0]:half])
    r_ref[0, :, D_RNN:2 * D_RNN] = _dot(hb, w_ref[:, half:C_R[1]])
    gl_ref[0] = _dot(hb, w_ref[:, C_GL[0]:C_GL[1]])


def _inproj(x, g, mod6, w_perm, tm=512):
    bsz, s, d = x.shape
    row = lambda width: pl.BlockSpec((1, tm, width), lambda b, i: (b, i, 0))
    shp = lambda width, dt: jax.ShapeDtypeStruct((bsz, s, width), dt)
    return pl.pallas_call(
        _inproj_kernel,
        grid=(bsz, s // tm),
        in_specs=[
            row(d),
            pl.BlockSpec((1, d), lambda b, i: (0, 0)),
            pl.BlockSpec((1, 1, 1, d), lambda b, i: (1, b, 0, 0)),
            pl.BlockSpec((1, 1, 1, d), lambda b, i: (0, b, 0, 0)),
            pl.BlockSpec((d, D_IN_PAD), lambda b, i: (0, 0), pipeline_mode=pl.Buffered(1)),
        ],
        out_specs=[row(D_ATT), row(2 * D_KV), row(4 * D_KV), row(2 * D_RNN), row(GL_PAD)],
        out_shape=[shp(D_ATT, BF16), shp(2 * D_KV, F32), shp(4 * D_KV, BF16),
                   shp(2 * D_RNN, F32), shp(GL_PAD, F32)],
        compiler_params=_params(("parallel", "arbitrary")),
        name="in_proj",
    )(x, g.reshape(1, d), mod6, mod6, w_perm)


def _compress_kernel(x_ref, pe_ref, w1_ref, w2_ref, o_ref):
    x = x_ref[0, 0, 0]
    half = x.shape[1]
    xa = (x + pe_ref[0, 0:1, :]).astype(BF16)
    xb = (x + pe_ref[0, 1:2, :]).astype(BF16)
    first = _dot(xa, w1_ref[0, 0:half, :])
    second = _dot(xb, w1_ref[0, half:2 * half, :])
    n_rows = x.shape[0]
    hid = first + pltpu.roll(second, n_rows - 1, axis=0)
    act = jax.nn.gelu(hid)
    o_ref[0, 0, 0] = _dot(act.astype(BF16), w2_ref[0])


def _compress(x16, pe2, w1, w2):
    two, bsz, g, nr, width = x16.shape
    hid = w1.shape[2]
    dh = w2.shape[2]
    return pl.pallas_call(
        _compress_kernel,
        grid=(two, bsz, g),
        in_specs=[
            pl.BlockSpec((1, 1, 1, nr, width), lambda t, b, j: (t, b, j, 0, 0)),
            pl.BlockSpec((1, 2, width), lambda t, b, j: (t, 0, 0)),
            pl.BlockSpec((1, 2 * width, hid), lambda t, b, j: (t, 0, 0)),
            pl.BlockSpec((1, hid, dh), lambda t, b, j: (t, 0, 0)),
        ],
        out_specs=pl.BlockSpec((1, 1, 1, nr, dh), lambda t, b, j: (t, b, j, 0, 0)),
        out_shape=jax.ShapeDtypeStruct((two, bsz, g, nr, dh), F32),
        compiler_params=_params(("arbitrary", "arbitrary", "arbitrary")),
        name="compress_kv",
    )(x16, pe2, w1, w2)


def _attn_kernel(q_ref, kct_ref, vc_ref, kst_ref, kwt_ref, vcat_ref, gl_ref, o_ref,
                 m_ref, l_ref, acc_ref, oc_ref):
    qi = pl.program_id(2)
    hg = HEADS_PER_GROUP
    dh = HEAD_DIM
    q = q_ref[0]
    qh = [q[:, dh * h:dh * (h + 1)] for h in range(hg)]
    row1 = lax.broadcasted_iota(jnp.int32, (TQ, 1), 0)
    pos = qi * TQ + row1
    rows = lax.broadcasted_iota(jnp.int32, (TQ, TQ), 0)
    cols = lax.broadcasted_iota(jnp.int32, (TQ, TQ), 1)
    causal = cols <= rows

    ncol = lax.broadcasted_iota(jnp.int32, (TQ, N_CMP_PAD), 1)
    mask_c = ncol * CMP_STRIDE + (CMP_BLOCK - 1) <= pos
    kct = kct_ref[0, 0]
    vc = vc_ref[0, 0]
    psum = jnp.zeros((TQ, N_CMP_PAD), F32)
    for h in range(hg):
        s = jnp.where(mask_c, _dot(qh[h], kct), NEG)
        e = jnp.exp(s - jnp.max(s, axis=-1, keepdims=True))
        p = jnp.where(mask_c, e / jnp.sum(e, axis=-1, keepdims=True), 0.0)
        psum = psum + p
        oc_ref[h] = _dot(p.astype(BF16), vc)

    nn = lax.broadcasted_iota(jnp.int32, (N_CMP_PAD, N_SEL), 0) * CMP_STRIDE
    jj = lax.broadcasted_iota(jnp.int32, (N_CMP_PAD, N_SEL), 1) * SEL_BLOCK
    ov = jnp.minimum(nn + CMP_BLOCK, jj + SEL_BLOCK) - jnp.maximum(nn, jj)
    w_sel = (jnp.maximum(ov, 0).astype(F32) * (1.0 / CMP_BLOCK)).astype(BF16)
    p_hi = psum.astype(BF16)
    p_lo = (psum - p_hi.astype(F32)).astype(BF16)
    imp = _dot(p_hi, w_sel) + _dot(p_lo, w_sel)
    blk = lax.broadcasted_iota(jnp.int32, (TQ, N_SEL), 1)
    cur = lax.shift_right_logical(pos, SEL_SHIFT)
    free = jnp.where(blk * SEL_BLOCK <= pos, imp, -FORCE_SCORE)
    score = jnp.where(blk == 0, FORCE_SCORE,
                      jnp.where(blk == cur, FORCE_SCORE, jnp.where(blk == cur - 1, FORCE_SCORE, free)))
    rank = jnp.zeros((TQ, N_SEL), F32)
    for k in range(N_SEL):
        sk = score[:, k:k + 1]
        tie = jnp.where(blk > k, 1.0, 0.0)
        rank = rank + jnp.where(sk > score, 1.0, jnp.where(sk == score, tie, 0.0))
    selm = jnp.where(rank < SEL_TOP_N, 1.0, 0.0).astype(BF16)

    def expand_mask(kt):
        bj = lax.broadcasted_iota(jnp.int32, (N_SEL, TQ), 0)
        kc = lax.broadcasted_iota(jnp.int32, (N_SEL, TQ), 1) + kt * TQ
        expand = jnp.where(lax.shift_right_logical(kc, SEL_SHIFT) == bj, 1.0, 0.0).astype(BF16)
        return _dot(selm, expand)

    def first_tile(branch, kt_ref, kt, mask):
        k_t = kt_ref[0, 0, kt]
        v_t = vcat_ref[0, 0, pl.ds(pl.multiple_of(kt * TQ, TQ), TQ), :]
        for h in range(hg):
            s = jnp.where(mask, _dot(qh[h], k_t), NEG)
            m = jnp.max(s, axis=-1, keepdims=True)
            p = jnp.exp(s - m)
            m_ref[branch, h] = m
            l_ref[branch, h] = jnp.sum(p, axis=-1, keepdims=True)
            acc_ref[branch, h] = _dot(p.astype(BF16), v_t)

    def next_tile(branch, kt_ref, kt, mask):
        k_t = kt_ref[0, 0, kt]
        v_t = vcat_ref[0, 0, pl.ds(pl.multiple_of(kt * TQ, TQ), TQ), :]
        for h in range(hg):
            s = _dot(qh[h], k_t)
            if mask is not None:
                s = jnp.where(mask, s, NEG)
            m_old = m_ref[branch, h]
            m_new = jnp.maximum(m_old, jnp.max(s, axis=-1, keepdims=True))
            alpha = jnp.exp(m_old - m_new)
            p = jnp.exp(s - m_new)
            m_ref[branch, h] = m_new
            l_ref[branch, h] = alpha * l_ref[branch, h] + jnp.sum(p, axis=-1, keepdims=True)
            acc_ref[branch, h] = alpha * acc_ref[branch, h] + _dot(p.astype(BF16), v_t)

    first_tile(0, kst_ref, qi, jnp.where(causal, expand_mask(qi), 0.0) > 0.5)

    def sel_body(kt, carry):
        next_tile(0, kst_ref, kt, expand_mask(kt) > 0.5)
        return carry

    lax.fori_loop(0, qi, sel_body, 0)

    first_tile(1, kwt_ref, qi, causal)

    @pl.when(qi >= 1)
    def _():
        next_tile(1, kwt_ref, qi - 1, None)

    @pl.when(qi >= 2)
    def _():
        next_tile(1, kwt_ref, qi - 2, cols > rows)

    gate = jax.nn.sigmoid(gl_ref[0, 0])
    outs = []
    for h in range(hg):
        o_s = acc_ref[0, h][:, 0:dh] / l_ref[0, h]
        o_w = acc_ref[1, h][:, dh:2 * dh] / l_ref[1, h]
        g0 = gate[:, 3 * h:3 * h + 1]
        g1 = gate[:, 3 * h + 1:3 * h + 2]
        g2 = gate[:, 3 * h + 2:3 * h + 3]
        outs.append(g0 * oc_ref[h] + g1 * o_s + g2 * o_w)
    o_ref[0] = jnp.concatenate(outs, axis=-1)


def _attention(q, kct, vc, kst, kwt, vcat, glg):
    bsz, s, _ = q.shape
    g = N_KV_GROUPS
    nq = s // TQ
    width = HEADS_PER_GROUP * HEAD_DIM
    per_bg = lambda *shape: pl.BlockSpec((1, 1) + shape, lambda b, j, i: (b, j) + (0,) * len(shape))
    return pl.pallas_call(
        _attn_kernel,
        grid=(bsz, g, nq),
        in_specs=[
            pl.BlockSpec((1, TQ, width), lambda b, j, i: (b, i, j)),
            per_bg(HEAD_DIM, N_CMP_PAD),
            per_bg(N_CMP_PAD, HEAD_DIM),
            per_bg(nq, HEAD_DIM, TQ),
            per_bg(nq, HEAD_DIM, TQ),
            per_bg(s, 2 * HEAD_DIM),
            pl.BlockSpec((1, 1, TQ, HEADS_PER_GROUP * N_BRANCH), lambda b, j, i: (b, j, i, 0)),
        ],
        out_specs=pl.BlockSpec((1, TQ, width), lambda b, j, i: (b, i, j)),
        out_shape=jax.ShapeDtypeStruct((bsz, s, D_ATT), F32),
        scratch_shapes=[
            pltpu.VMEM((2, HEADS_PER_GROUP, TQ, 1), F32),
            pltpu.VMEM((2, HEADS_PER_GROUP, TQ, 1), F32),
            pltpu.VMEM((2, HEADS_PER_GROUP, TQ, 2 * HEAD_DIM), F32),
            pltpu.VMEM((HEADS_PER_GROUP, TQ, HEAD_DIM), F32),
        ],
        compiler_params=_params(("parallel", "parallel", "arbitrary")),
        name="nsa_attention",
    )(q, kct, vc, kst, kwt, vcat, glg)


def _rglru_kernel(x_ref, y_ref, cw_ref, cb_ref, w_ref, b_ref, lam_ref, o_ref,
                  xpad_ref, a_ref, bt_ref):
    s = x_ref.shape[1]
    c = RNN_BLOCK_DIM
    pad = SUBLANES
    xpad_ref[0:pad, :] = jnp.zeros((pad, c), F32)
    xpad_ref[pad:pad + s, :] = x_ref[0]
    cw = cw_ref[...]
    w = w_ref[0]
    bias = b_ref[0]
    lam = lam_ref[...]
    neg_softplus = -(jnp.maximum(-lam, 0.0) + jnp.log1p(jnp.exp(-jnp.abs(lam))))
    sub = lax.broadcasted_iota(jnp.int32, (TT, c), 0) & (SUBLANES - 1)

    for ci in range(s // TT):
        t0 = ci * TT
        xc = cb_ref[...] + sum(
            xpad_ref[t0 + pad - (CONV_WIDTH - 1) + k:t0 + pad - (CONV_WIDTH - 1) + k + TT, :] * cw[k:k + 1, :]
            for k in range(CONV_WIDTH))
        gates = jax.nn.sigmoid(_dot(xc.astype(BF16), w) + bias)
        r = gates[:, 0:c]
        i = gates[:, c:2 * c]
        log_a = LRU_C * r * neg_softplus
        a = jnp.exp(log_a)
        bt = jnp.sqrt(-jnp.tanh(log_a) * (a * a + 1.0)) * (i * xc)
        for d in (1, 2, 4):
            keep = sub >= d
            a_prev = jnp.where(keep, pltpu.roll(a, d, axis=0), 1.0)
            b_prev = jnp.where(keep, pltpu.roll(bt, d, axis=0), 0.0)
            bt = bt + a * b_prev
            a = a * a_prev
        a_ref[t0:t0 + TT, :] = a
        bt_ref[t0:t0 + TT, :] = bt

    def group(gi, h):
        r0 = pl.multiple_of(gi * SUBLANES, SUBLANES)
        hg = bt_ref[pl.ds(r0, SUBLANES), :] + a_ref[pl.ds(r0, SUBLANES), :] * h
        bt_ref[pl.ds(r0, SUBLANES), :] = hg
        return hg[SUBLANES - 1:SUBLANES, :]

    lax.fori_loop(0, s // SUBLANES, group, jnp.zeros((1, c), F32), unroll=8)

    for ci in range(s // TT):
        t0 = ci * TT
        o_ref[0, t0:t0 + TT, :] = jax.nn.gelu(y_ref[0, t0:t0 + TT, :]) * bt_ref[t0:t0 + TT, :]


def _rglru(r_in, conv_w, conv_b, w_cat, b_cat, lam):
    bsz, s, _ = r_in.shape
    c = RNN_BLOCK_DIM
    nb = RNN_BLOCKS
    return pl.pallas_call(
        _rglru_kernel,
        grid=(bsz, nb),
        in_specs=[
            pl.BlockSpec((1, s, c), lambda b, j: (b, 0, j)),
            pl.BlockSpec((1, s, c), lambda b, j: (b, 0, nb + j)),
            pl.BlockSpec((CONV_WIDTH, c), lambda b, j: (0, j)),
            pl.BlockSpec((1, c), lambda b, j: (0, j)),
            pl.BlockSpec((1, c, 2 * c), lambda b, j: (j, 0, 0)),
            pl.BlockSpec((1, 1, 2 * c), lambda b, j: (j, 0, 0)),
            pl.BlockSpec((1, c), lambda b, j: (0, j)),
        ],
        out_specs=pl.BlockSpec((1, s, c), lambda b, j: (b, 0, j)),
        out_shape=jax.ShapeDtypeStruct((bsz, s, D_RNN), F32),
        scratch_shapes=[
            pltpu.VMEM((s + SUBLANES, c), F32),
            pltpu.VMEM((s, c), F32),
            pltpu.VMEM((s, c), F32),
        ],
        compiler_params=_params(("parallel", "arbitrary")),
        name="rg_lru",
    )(r_in, r_in, conv_w, conv_b, w_cat, b_cat, lam)


def _outproj_kernel(oa_ref, or_ref, ga_ref, gr_ref, w_ref, x_ref, gt_ref, gp_ref, o_ref):
    a = _rms(oa_ref[0], ga_ref[...]).astype(BF16)
    r = _rms(or_ref[0], gr_ref[...]).astype(BF16)
    mix = _dot(a, w_ref[0:D_ATT, :]) + _dot(r, w_ref[D_ATT:D_ATT + D_RNN, :])
    o_ref[0] = x_ref[0] + gt_ref[0, 0] * _rms(mix, gp_ref[...])


def _outproj(o_att, o_rnn, g_att, g_rnn, w_out, x, mod6, g_post, tm=256):
    bsz, s, d = x.shape
    row = lambda width: pl.BlockSpec((1, tm, width), lambda b, i: (b, i, 0))
    vec = lambda width: pl.BlockSpec((1, width), lambda b, i: (0, 0))
    return pl.pallas_call(
        _outproj_kernel,
        grid=(bsz, s // tm),
        in_specs=[
            row(D_ATT), row(D_RNN), vec(D_ATT), vec(D_RNN),
            pl.BlockSpec((D_ATT + D_RNN, d), lambda b, i: (0, 0), pipeline_mode=pl.Buffered(1)),
            row(d),
            pl.BlockSpec((1, 1, 1, d), lambda b, i: (2, b, 0, 0)),
            vec(d),
        ],
        out_specs=row(d),
        out_shape=jax.ShapeDtypeStruct((bsz, s, d), F32),
        compiler_params=_params(("parallel", "arbitrary")),
        name="out_proj",
    )(o_att, o_rnn, g_att.reshape(1, -1), g_rnn.reshape(1, -1), w_out, x, mod6, g_post.reshape(1, d))


def _mlp_kernel(x_ref, g_ref, sc_ref, sh_ref, w1_ref, w2_ref, gt_ref, gp_ref, o_ref, h_ref, acc_ref):
    j = pl.program_id(2)

    @pl.when(j == 0)
    def _():
        h = _rms(x_ref[0], g_ref[...]) * (1.0 + sc_ref[0, 0]) + sh_ref[0, 0]
        h_ref[...] = h.astype(BF16)

    u = jnp.maximum(_dot(h_ref[...], w1_ref[...]), 0.0)
    part = _dot((u * u).astype(BF16), w2_ref[...])

    @pl.when(j == 0)
    def _():
        acc_ref[...] = part

    @pl.when(j > 0)
    def _():
        acc_ref[...] += part

    @pl.when(j == pl.num_programs(2) - 1)
    def _():
        o_ref[0] = x_ref[0] + gt_ref[0, 0] * _rms(acc_ref[...], gp_ref[...])


def _mlp(x, g_pre, mod6, w1, w2, g_post, tm=512, tf=1024):
    bsz, s, d = x.shape
    dff = w1.shape[1]
    row = pl.BlockSpec((1, tm, d), lambda b, i, j: (b, i, 0))
    vec = pl.BlockSpec((1, d), lambda b, i, j: (0, 0))
    modk = lambda k: pl.BlockSpec((1, 1, 1, d), lambda b, i, j: (k, b, 0, 0))
    return pl.pallas_call(
        _mlp_kernel,
        grid=(bsz, s // tm, dff // tf),
        in_specs=[
            row, vec, modk(4), modk(3),
            pl.BlockSpec((d, tf), lambda b, i, j: (0, j)),
            pl.BlockSpec((tf, d), lambda b, i, j: (j, 0)),
            modk(5), vec,
        ],
        out_specs=row,
        out_shape=jax.ShapeDtypeStruct((bsz, s, d), F32),
        scratch_shapes=[pltpu.VMEM((tm, d), BF16), pltpu.VMEM((tm, d), F32)],
        compiler_params=_params(("parallel", "parallel", "arbitrary")),
        name="mlp",
    )(x, g_pre.reshape(1, d), mod6, mod6, w1, w2, mod6, g_post.reshape(1, d))


def _layer(x, c, w_ada, b_ada, g_pre_mix, g_post_mix, g_pre_mlp, g_post_mlp, w_in,
           cmp_w1_k, cmp_w2_k, cmp_pe_k, cmp_w1_v, cmp_w2_v, cmp_pe_v,
           conv_w, conv_b, w_rg_a, b_rg_a, w_rg_x, b_rg_x, lru_lambda,
           g_grp_att, g_grp_rnn, w_out, w_ff1, w_ff2):
    bsz, s, d = x.shape
    g = N_KV_GROUPS
    dh = HEAD_DIM
    nq = s // TQ

    mod = _ada(c, w_ada, b_ada)
    mod6 = mod.reshape(bsz, 6, 1, d).transpose(1, 0, 2, 3)

    n_gl = N_BRANCH * N_Q_HEADS
    c_gl0 = D_ATT + 6 * D_KV
    w_perm = jnp.concatenate(
        [w_in[:, :c_gl0], w_in[:, c_gl0 + n_gl:], w_in[:, c_gl0:c_gl0 + n_gl],
         jnp.zeros((d, GL_PAD - n_gl), w_in.dtype)], axis=1).astype(BF16)
    q, kvc, sw, r_in, gl = _inproj(x, g_pre_mix, mod6, w_perm)

    x16 = kvc.reshape(bsz, s, 2, g, dh).transpose(2, 0, 3, 1, 4).reshape(
        2, bsz, g, s // CMP_STRIDE, CMP_STRIDE * dh)
    pe2 = jnp.stack([cmp_pe_k, cmp_pe_v]).reshape(2, 2, CMP_STRIDE * dh)
    w1c = jnp.stack([cmp_w1_k, cmp_w1_v]).astype(BF16)
    w2c = jnp.stack([cmp_w2_k, cmp_w2_v]).astype(BF16)
    kv_cmp = _compress(x16, pe2, w1c, w2c)
    kct = kv_cmp[0].transpose(0, 1, 3, 2).astype(BF16)
    vcm = kv_cmp[1].astype(BF16)

    sw5 = sw.reshape(bsz, s, 4, g, dh)
    kst = sw5[:, :, 0].reshape(bsz, nq, TQ, g, dh).transpose(0, 3, 1, 4, 2)
    kwt = sw5[:, :, 2].reshape(bsz, nq, TQ, g, dh).transpose(0, 3, 1, 4, 2)
    vcat = jnp.concatenate([sw5[:, :, 1], sw5[:, :, 3]], axis=-1).transpose(0, 2, 1, 3)
    glg = gl[:, :, :n_gl].reshape(bsz, s, g, HEADS_PER_GROUP * N_BRANCH).transpose(0, 2, 1, 3)
    o_att = _attention(q, kct, vcm, kst, kwt, vcat, glg)

    w_cat = jnp.concatenate([w_rg_a, w_rg_x], axis=-1).astype(BF16)
    b_cat = jnp.concatenate([b_rg_a.reshape(RNN_BLOCKS, 1, RNN_BLOCK_DIM),
                             b_rg_x.reshape(RNN_BLOCKS, 1, RNN_BLOCK_DIM)], axis=-1)
    o_rnn = _rglru(r_in, conv_w, conv_b.reshape(1, -1), w_cat, b_cat, lru_lambda.reshape(1, -1))

    x1 = _outproj(o_att, o_rnn, g_grp_att, g_grp_rnn, w_out.astype(BF16), x, mod6, g_post_mix)
    return _mlp(x1, g_pre_mlp, mod6, w_ff1.astype(BF16), w_ff2.astype(BF16), g_post_mlp)


def kernel(x, c, w_ada, b_ada, g_pre_mix, g_post_mix, g_pre_mlp, g_post_mlp, w_in, cmp_w1_k, cmp_w2_k, cmp_pe_k, cmp_w1_v, cmp_w2_v, cmp_pe_v, conv_w, conv_b, w_rg_a, b_rg_a, w_rg_x, b_rg_x, lru_lambda, g_grp_att, g_grp_rnn, w_out, w_ff1, w_ff2):
    depth = w_ada.shape[0]
    for l in range(depth):
        x = _layer(x, c, w_ada[l], b_ada[l], g_pre_mix[l], g_post_mix[l], g_pre_mlp[l], g_post_mlp[l],
                   w_in[l], cmp_w1_k[l], cmp_w2_k[l], cmp_pe_k[l], cmp_w1_v[l], cmp_w2_v[l], cmp_pe_v[l],
                   conv_w[l], conv_b[l], w_rg_a[l], b_rg_a[l], w_rg_x[l], b_rg_x[l], lru_lambda[l],
                   g_grp_att[l], g_grp_rnn[l], w_out[l], w_ff1[l], w_ff2[l])
    return x
```

```python
import functools

import jax
import jax.numpy as jnp
from jax import lax
from jax.experimental import pallas as pl
from jax.experimental.pallas import tpu as pltpu

F32 = jnp.float32
BF16 = jnp.bfloat16

D_MODEL = 2048
D_ATT = 1024
D_RNN = 1024
N_Q_HEADS = 16
N_KV_GROUPS = 4
HEADS_PER_GROUP = 4
HEAD_DIM = 64
D_KV = 256
CMP_BLOCK = 32
CMP_STRIDE = 16
SEL_BLOCK = 64
SEL_SHIFT = 6
SEL_TOP_N = 8
WINDOW = 512
N_BRANCH = 3
RNN_BLOCKS = 8
RNN_BLOCK_DIM = 128
CONV_WIDTH = 4
LRU_C = 8.0
D_FF = 4 * D_MODEL
EPS = 1e-6
NEG = -1e30
FORCE_SCORE = 1e9

LANES = 128
SUBLANES = 8
VMEM_LIMIT = 56 * 1024 * 1024

TQ = 256
N_CMP_PAD = 128
N_SEL = 32
GL_PAD = 128
TT = 256


def _params(sem):
    return pltpu.CompilerParams(dimension_semantics=sem, vmem_limit_bytes=VMEM_LIMIT)


def _dot(a, b):
    return jnp.dot(a, b, preferred_element_type=F32)


def _rms(x, g):
    return x * lax.rsqrt(jnp.mean(x * x, axis=-1, keepdims=True) + EPS) * g


def _ada_kernel(c_ref, w_ref, b_ref, o_ref):
    c = c_ref[...]
    ca = (c * jax.nn.sigmoid(c)).astype(BF16)
    o_ref[...] = _dot(ca, w_ref[...].astype(BF16)) + b_ref[...]


def _ada(c, w, b):
    bsz, d = c.shape
    n = w.shape[1]
    tn = 1024
    return pl.pallas_call(
        _ada_kernel,
        grid=(n // tn,),
        in_specs=[
            pl.BlockSpec((bsz, d), lambda j: (0, 0)),
            pl.BlockSpec((d, tn), lambda j: (0, j)),
            pl.BlockSpec((1, tn), lambda j: (0, j)),
        ],
        out_specs=pl.BlockSpec((bsz, tn), lambda j: (0, j)),
        out_shape=jax.ShapeDtypeStruct((bsz, n), F32),
        compiler_params=_params(("arbitrary",)),
        name="ada_mod",
    )(c, w, b.reshape(1, n))


C_Q = (0, D_ATT)
C_CMP = (C_Q[1], C_Q[1] + 2 * D_KV)
C_SW = (C_CMP[1], C_CMP[1] + 4 * D_KV)
C_R = (C_SW[1], C_SW[1] + 2 * D_RNN)
C_GL = (C_R[1], C_R[1] + GL_PAD)
D_IN_PAD = C_GL[1]


def _inproj_kernel(x_ref, g_ref, sc_ref, sh_ref, w_ref, q_ref, cmp_ref, sw_ref, r_ref, gl_ref):
    x = x_ref[0]
    h = _rms(x, g_ref[...]) * (1.0 + sc_ref[0, 0]) + sh_ref[0, 0]
    hb = h.astype(BF16)
    q_ref[0] = (_dot(hb, w_ref[:, C_Q[0]:C_Q[1]]) * (HEAD_DIM ** -0.5)).astype(BF16)
    cmp_ref[0] = _dot(hb, w_ref[:, C_CMP[0]:C_CMP[1]])
    sw_ref[0] = _dot(hb, w_ref[:, C_SW[0]:C_SW[1]]).astype(BF16)
    half = (C_R[0] + C_R[1]) // 2
    r_ref[0, :, 0:D_RNN] = _dot(hb, w_ref[:, C_R[0]:half])
    r_ref[0, :, D_RNN:2 * D_RNN] = _dot(hb, w_ref[:, half:C_R[1]])
    gl_ref[0] = _dot(hb, w_ref[:, C_GL[0]:C_GL[1]])


def _inproj(x, g, mod6, w_perm, tm=512):
    bsz, s, d = x.shape
    row = lambda width: pl.BlockSpec((1, tm, width), lambda b, i: (b, i, 0))
    shp = lambda width, dt: jax.ShapeDtypeStruct((bsz, s, width), dt)
    return pl.pallas_call(
        _inproj_kernel,
        grid=(bsz, s // tm),
        in_specs=[
            row(d),
            pl.BlockSpec((1, d), lambda b, i: (0, 0)),
            pl.BlockSpec((1, 1, 1, d), lambda b, i: (1, b, 0, 0)),
            pl.BlockSpec((1, 1, 1, d), lambda b, i: (0, b, 0, 0)),
            pl.BlockSpec((d, D_IN_PAD), lambda b, i: (0, 0), pipeline_mode=pl.Buffered(1)),
        ],
        out_specs=[row(D_ATT), row(2 * D_KV), row(4 * D_KV), row(2 * D_RNN), row(GL_PAD)],
        out_shape=[shp(D_ATT, BF16), shp(2 * D_KV, F32), shp(4 * D_KV, BF16),
                   shp(2 * D_RNN, F32), shp(GL_PAD, F32)],
        compiler_params=_params(("parallel", "arbitrary")),
        name="in_proj",
    )(x, g.reshape(1, d), mod6, mod6, w_perm)


def _compress_kernel(x_ref, pe_ref, w1_ref, w2_ref, o_ref):
    x = x_ref[0, 0, 0]
    half = x.shape[1]
    xa = (x + pe_ref[0, 0:1, :]).astype(BF16)
    xb = (x + pe_ref[0, 1:2, :]).astype(BF16)
    first = _dot(xa, w1_ref[0, 0:half, :])
    second = _dot(xb, w1_ref[0, half:2 * half, :])
    n_rows = x.shape[0]
    hid = first + pltpu.roll(second, n_rows - 1, axis=0)
    act = jax.nn.gelu(hid)
    o_ref[0, 0, 0] = _dot(act.astype(BF16), w2_ref[0])


def _compress(x16, pe2, w1, w2):
    two, bsz, g, nr, width = x16.shape
    hid = w1.shape[2]
    dh = w2.shape[2]
    return pl.pallas_call(
        _compress_kernel,
        grid=(two, bsz, g),
        in_specs=[
            pl.BlockSpec((1, 1, 1, nr, width), lambda t, b, j: (t, b, j, 0, 0)),
            pl.BlockSpec((1, 2, width), lambda t, b, j: (t, 0, 0)),
            pl.BlockSpec((1, 2 * width, hid), lambda t, b, j: (t, 0, 0)),
            pl.BlockSpec((1, hid, dh), lambda t, b, j: (t, 0, 0)),
        ],
        out_specs=pl.BlockSpec((1, 1, 1, nr, dh), lambda t, b, j: (t, b, j, 0, 0)),
        out_shape=jax.ShapeDtypeStruct((two, bsz, g, nr, dh), F32),
        compiler_params=_params(("arbitrary", "arbitrary", "arbitrary")),
        name="compress_kv",
    )(x16, pe2, w1, w2)


V_ROWS = HEAD_DIM + 16


def _attn_kernel(qt_ref, kc_ref, vct_ref, ks_ref, kw_ref, vst_ref, vwt_ref, gl_ref, o_ref,
                 m_ref, acc_ref):
    qi = pl.program_id(2)
    hg = HEADS_PER_GROUP
    dh = HEAD_DIM
    qt = qt_ref[0, 0, 0]

    def per_head_where(mask, a, fill):
        return jnp.concatenate(
            [jnp.where(mask, a[:, h * TQ:(h + 1) * TQ], fill) for h in range(hg)], axis=1)

    pos = qi * TQ + lax.broadcasted_iota(jnp.int32, (1, TQ), 1)
    key_r = lax.broadcasted_iota(jnp.int32, (TQ, TQ), 0)
    qry_c = lax.broadcasted_iota(jnp.int32, (TQ, TQ), 1)

    nrow = lax.broadcasted_iota(jnp.int32, (N_CMP_PAD, TQ), 0)
    mask_c = nrow * CMP_STRIDE + (CMP_BLOCK - 1) <= pos
    s = per_head_where(mask_c, _dot(kc_ref[0, 0], qt), NEG)
    e = jnp.exp(s - jnp.max(s, axis=0, keepdims=True))
    p = per_head_where(mask_c, e * (1.0 / jnp.sum(e, axis=0, keepdims=True)), 0.0)
    o_cmp = _dot(vct_ref[0, 0], p.astype(BF16))
    psum = p[:, 0:TQ]
    for h in range(1, hg):
        psum = psum + p[:, h * TQ:(h + 1) * TQ]

    jj = lax.broadcasted_iota(jnp.int32, (N_SEL, N_CMP_PAD), 0) * SEL_BLOCK
    nn = lax.broadcasted_iota(jnp.int32, (N_SEL, N_CMP_PAD), 1) * CMP_STRIDE
    ov = jnp.minimum(nn + CMP_BLOCK, jj + SEL_BLOCK) - jnp.maximum(nn, jj)
    w_sel = (jnp.maximum(ov, 0).astype(F32) * (1.0 / CMP_BLOCK)).astype(BF16)
    p_hi = psum.astype(BF16)
    p_lo = (psum - p_hi.astype(F32)).astype(BF16)
    imp = _dot(w_sel, p_hi) + _dot(w_sel, p_lo)
    blk = lax.broadcasted_iota(jnp.int32, (N_SEL, TQ), 0)
    cur = lax.shift_right_logical(pos, SEL_SHIFT)
    free = jnp.where(blk * SEL_BLOCK <= pos, imp, -FORCE_SCORE)
    score = jnp.where(blk == 0, FORCE_SCORE,
                      jnp.where(blk == cur, FORCE_SCORE, jnp.where(blk == cur - 1, FORCE_SCORE, free)))
    rank = jnp.zeros((N_SEL, TQ), F32)
    for k in range(N_SEL):
        sk = score[k:k + 1, :]
        tie = jnp.where(blk > k, 1.0, 0.0)
        rank = rank + jnp.where(sk > score, 1.0, jnp.where(sk == score, tie, 0.0))
    selm = jnp.where(rank < SEL_TOP_N, 1.0, 0.0).astype(BF16)

    def sel_mask(kt):
        kr = lax.broadcasted_iota(jnp.int32, (TQ, N_SEL), 0) + kt * TQ
        bj = lax.broadcasted_iota(jnp.int32, (TQ, N_SEL), 1)
        expand = jnp.where(lax.shift_right_logical(kr, SEL_SHIFT) == bj, 1.0, 0.0).astype(BF16)
        return _dot(expand, selm)

    def scores(k_ref, kt, mask):
        k_t = k_ref[0, 0, pl.ds(pl.multiple_of(kt * TQ, TQ), TQ), :]
        sc = _dot(k_t, qt)
        return sc if mask is None else per_head_where(mask, sc, NEG)

    def first_tile(branch, k_ref, vt_ref, kt, mask):
        sc = scores(k_ref, kt, mask)
        m = jnp.max(sc, axis=0, keepdims=True)
        m_ref[branch] = m
        acc_ref[branch] = _dot(vt_ref[0, 0, kt], jnp.exp(sc - m).astype(BF16))

    def next_tile(branch, k_ref, vt_ref, kt, mask):
        sc = scores(k_ref, kt, mask)
        m_old = m_ref[branch]
        m_new = jnp.maximum(m_old, jnp.max(sc, axis=0, keepdims=True))
        m_ref[branch] = m_new
        acc_ref[branch] = (jnp.exp(m_old - m_new) * acc_ref[branch]
                           + _dot(vt_ref[0, 0, kt], jnp.exp(sc - m_new).astype(BF16)))

    first_tile(0, ks_ref, vst_ref, qi, jnp.where(key_r <= qry_c, sel_mask(qi), 0.0) > 0.5)

    def sel_body(kt, carry):
        next_tile(0, ks_ref, vst_ref, kt, sel_mask(kt) > 0.5)
        return carry

    lax.fori_loop(0, qi, sel_body, 0)

    first_tile(1, kw_ref, vwt_ref, qi, key_r <= qry_c)

    @pl.when(qi >= 1)
    def _():
        next_tile(1, kw_ref, vwt_ref, qi - 1, None)

    @pl.when(qi >= 2)
    def _():
        next_tile(1, kw_ref, vwt_ref, qi - 2, key_r > qry_c)

    gate = jax.nn.sigmoid(gl_ref[0, 0])
    acc_s = acc_ref[0]
    acc_w = acc_ref[1]
    o_sel = acc_s[0:dh, :] * (1.0 / acc_s[dh:dh + 1, :])
    o_win = acc_w[0:dh, :] * (1.0 / acc_w[dh:dh + 1, :])
    outs = []
    for h in range(hg):
        lanes = slice(h * TQ, (h + 1) * TQ)
        g0 = gate[3 * h:3 * h + 1, :]
        g1 = gate[3 * h + 1:3 * h + 2, :]
        g2 = gate[3 * h + 2:3 * h + 3, :]
        outs.append(g0 * o_cmp[:, lanes] + g1 * o_sel[:, lanes] + g2 * o_win[:, lanes])
    o_ref[0] = jnp.concatenate(outs, axis=0).T


def _attention(qt, kc, vct, ks, kw, vst, vwt, glt):
    bsz, g, nq = qt.shape[:3]
    s = nq * TQ
    width = HEADS_PER_GROUP * HEAD_DIM
    wide = HEADS_PER_GROUP * TQ
    per_bg = lambda *shape: pl.BlockSpec((1, 1) + shape, lambda b, j, i: (b, j) + (0,) * len(shape))
    return pl.pallas_call(
        _attn_kernel,
        grid=(bsz, g, nq),
        in_specs=[
            pl.BlockSpec((1, 1, 1, HEAD_DIM, wide), lambda b, j, i: (b, j, i, 0, 0)),
            per_bg(N_CMP_PAD, HEAD_DIM),
            per_bg(HEAD_DIM, N_CMP_PAD),
            per_bg(s, HEAD_DIM),
            per_bg(s, HEAD_DIM),
            per_bg(nq, V_ROWS, TQ),
            per_bg(nq, V_ROWS, TQ),
            pl.BlockSpec((1, 1, HEADS_PER_GROUP * N_BRANCH, TQ), lambda b, j, i: (b, j, 0, i)),
        ],
        out_specs=pl.BlockSpec((1, TQ, width), lambda b, j, i: (b, i, j)),
        out_shape=jax.ShapeDtypeStruct((bsz, s, D_ATT), F32),
        scratch_shapes=[
            pltpu.VMEM((2, 1, wide), F32),
            pltpu.VMEM((2, V_ROWS, wide), F32),
        ],
        compiler_params=_params(("parallel", "parallel", "arbitrary")),
        name="nsa_attention",
    )(qt, kc, vct, ks, kw, vst, vwt, glt)


def _rglru_kernel(x_ref, y_ref, cw_ref, cb_ref, w_ref, b_ref, lam_ref, o_ref,
                  xpad_ref, a_ref, bt_ref):
    s = x_ref.shape[1]
    c = RNN_BLOCK_DIM
    pad = SUBLANES
    xpad_ref[0:pad, :] = jnp.zeros((pad, c), F32)
    xpad_ref[pad:pad + s, :] = x_ref[0]
    cw = cw_ref[...]
    w = w_ref[0]
    bias = b_ref[0]
    lam = lam_ref[...]
    neg_softplus = -(jnp.maximum(-lam, 0.0) + jnp.log1p(jnp.exp(-jnp.abs(lam))))
    sub = lax.broadcasted_iota(jnp.int32, (TT, c), 0) & (SUBLANES - 1)

    for ci in range(s // TT):
        t0 = ci * TT
        xc = cb_ref[...] + sum(
            xpad_ref[t0 + pad - (CONV_WIDTH - 1) + k:t0 + pad - (CONV_WIDTH - 1) + k + TT, :] * cw[k:k + 1, :]
            for k in range(CONV_WIDTH))
        gates = jax.nn.sigmoid(_dot(xc.astype(BF16), w) + bias)
        r = gates[:, 0:c]
        i = gates[:, c:2 * c]
        log_a = LRU_C * r * neg_softplus
        a = jnp.exp(log_a)
        bt = jnp.sqrt(-jnp.tanh(log_a) * (a * a + 1.0)) * (i * xc)
        for d in (1, 2, 4):
            keep = sub >= d
            a_prev = jnp.where(keep, pltpu.roll(a, d, axis=0), 1.0)
            b_prev = jnp.where(keep, pltpu.roll(bt, d, axis=0), 0.0)
            bt = bt + a * b_prev
            a = a * a_prev
        a_ref[t0:t0 + TT, :] = a
        bt_ref[t0:t0 + TT, :] = bt

    def group(gi, h):
        r0 = pl.multiple_of(gi * SUBLANES, SUBLANES)
        hg = bt_ref[pl.ds(r0, SUBLANES), :] + a_ref[pl.ds(r0, SUBLANES), :] * h
        bt_ref[pl.ds(r0, SUBLANES), :] = hg
        return hg[SUBLANES - 1:SUBLANES, :]

    lax.fori_loop(0, s // SUBLANES, group, jnp.zeros((1, c), F32), unroll=8)

    for ci in range(s // TT):
        t0 = ci * TT
        o_ref[0, t0:t0 + TT, :] = jax.nn.gelu(y_ref[0, t0:t0 + TT, :]) * bt_ref[t0:t0 + TT, :]


def _rglru(r_in, conv_w, conv_b, w_cat, b_cat, lam):
    bsz, s, _ = r_in.shape
    c = RNN_BLOCK_DIM
    nb = RNN_BLOCKS
    return pl.pallas_call(
        _rglru_kernel,
        grid=(bsz, nb),
        in_specs=[
            pl.BlockSpec((1, s, c), lambda b, j: (b, 0, j)),
            pl.BlockSpec((1, s, c), lambda b, j: (b, 0, nb + j)),
            pl.BlockSpec((CONV_WIDTH, c), lambda b, j: (0, j)),
            pl.BlockSpec((1, c), lambda b, j: (0, j)),
            pl.BlockSpec((1, c, 2 * c), lambda b, j: (j, 0, 0)),
            pl.BlockSpec((1, 1, 2 * c), lambda b, j: (j, 0, 0)),
            pl.BlockSpec((1, c), lambda b, j: (0, j)),
        ],
        out_specs=pl.BlockSpec((1, s, c), lambda b, j: (b, 0, j)),
        out_shape=jax.ShapeDtypeStruct((bsz, s, D_RNN), F32),
        scratch_shapes=[
            pltpu.VMEM((s + SUBLANES, c), F32),
            pltpu.VMEM((s, c), F32),
            pltpu.VMEM((s, c), F32),
        ],
        compiler_params=_params(("parallel", "arbitrary")),
        name="rg_lru",
    )(r_in, r_in, conv_w, conv_b, w_cat, b_cat, lam)


def _outproj_kernel(oa_ref, or_ref, ga_ref, gr_ref, w_ref, x_ref, gt_ref, gp_ref, o_ref):
    a = _rms(oa_ref[0], ga_ref[...]).astype(BF16)
    r = _rms(or_ref[0], gr_ref[...]).astype(BF16)
    mix = _dot(a, w_ref[0:D_ATT, :]) + _dot(r, w_ref[D_ATT:D_ATT + D_RNN, :])
    o_ref[0] = x_ref[0] + gt_ref[0, 0] * _rms(mix, gp_ref[...])


def _outproj(o_att, o_rnn, g_att, g_rnn, w_out, x, mod6, g_post, tm=256):
    bsz, s, d = x.shape
    row = lambda width: pl.BlockSpec((1, tm, width), lambda b, i: (b, i, 0))
    vec = lambda width: pl.BlockSpec((1, width), lambda b, i: (0, 0))
    return pl.pallas_call(
        _outproj_kernel,
        grid=(bsz, s // tm),
        in_specs=[
            row(D_ATT), row(D_RNN), vec(D_ATT), vec(D_RNN),
            pl.BlockSpec((D_ATT + D_RNN, d), lambda b, i: (0, 0), pipeline_mode=pl.Buffered(1)),
            row(d),
            pl.BlockSpec((1, 1, 1, d), lambda b, i: (2, b, 0, 0)),
            vec(d),
        ],
        out_specs=row(d),
        out_shape=jax.ShapeDtypeStruct((bsz, s, d), F32),
        compiler_params=_params(("parallel", "arbitrary")),
        name="out_proj",
    )(o_att, o_rnn, g_att.reshape(1, -1), g_rnn.reshape(1, -1), w_out, x, mod6, g_post.reshape(1, d))


def _mlp_kernel(x_ref, g_ref, sc_ref, sh_ref, w1_ref, w2_ref, gt_ref, gp_ref, o_ref, h_ref, acc_ref):
    j = pl.program_id(2)

    @pl.when(j == 0)
    def _():
        h = _rms(x_ref[0], g_ref[...]) * (1.0 + sc_ref[0, 0]) + sh_ref[0, 0]
        h_ref[...] = h.astype(BF16)

    u = jnp.maximum(_dot(h_ref[...], w1_ref[...]), 0.0)
    part = _dot((u * u).astype(BF16), w2_ref[...])

    @pl.when(j == 0)
    def _():
        acc_ref[...] = part

    @pl.when(j > 0)
    def _():
        acc_ref[...] += part

    @pl.when(j == pl.num_programs(2) - 1)
    def _():
        o_ref[0] = x_ref[0] + gt_ref[0, 0] * _rms(acc_ref[...], gp_ref[...])


def _mlp(x, g_pre, mod6, w1, w2, g_post, tm=512, tf=1024):
    bsz, s, d = x.shape
    dff = w1.shape[1]
    row = pl.BlockSpec((1, tm, d), lambda b, i, j: (b, i, 0))
    vec = pl.BlockSpec((1, d), lambda b, i, j: (0, 0))
    modk = lambda k: pl.BlockSpec((1, 1, 1, d), lambda b, i, j: (k, b, 0, 0))
    return pl.pallas_call(
        _mlp_kernel,
        grid=(bsz, s // tm, dff // tf),
        in_specs=[
            row, vec, modk(4), modk(3),
            pl.BlockSpec((d, tf), lambda b, i, j: (0, j)),
            pl.BlockSpec((tf, d), lambda b, i, j: (j, 0)),
            modk(5), vec,
        ],
        out_specs=row,
        out_shape=jax.ShapeDtypeStruct((bsz, s, d), F32),
        scratch_shapes=[pltpu.VMEM((tm, d), BF16), pltpu.VMEM((tm, d), F32)],
        compiler_params=_params(("parallel", "parallel", "arbitrary")),
        name="mlp",
    )(x, g_pre.reshape(1, d), mod6, mod6, w1, w2, mod6, g_post.reshape(1, d))


def _layer(x, c, w_ada, b_ada, g_pre_mix, g_post_mix, g_pre_mlp, g_post_mlp, w_in,
           cmp_w1_k, cmp_w2_k, cmp_pe_k, cmp_w1_v, cmp_w2_v, cmp_pe_v,
           conv_w, conv_b, w_rg_a, b_rg_a, w_rg_x, b_rg_x, lru_lambda,
           g_grp_att, g_grp_rnn, w_out, w_ff1, w_ff2):
    bsz, s, d = x.shape
    g = N_KV_GROUPS
    dh = HEAD_DIM
    nq = s // TQ

    mod = _ada(c, w_ada, b_ada)
    mod6 = mod.reshape(bsz, 6, 1, d).transpose(1, 0, 2, 3)

    n_gl = N_BRANCH * N_Q_HEADS
    c_gl0 = D_ATT + 6 * D_KV
    w_perm = jnp.concatenate(
        [w_in[:, :c_gl0], w_in[:, c_gl0 + n_gl:], w_in[:, c_gl0:c_gl0 + n_gl],
         jnp.zeros((d, GL_PAD - n_gl), w_in.dtype)], axis=1).astype(BF16)
    q, kvc, sw, r_in, gl = _inproj(x, g_pre_mix, mod6, w_perm)

    x16 = kvc.reshape(bsz, s, 2, g, dh).transpose(2, 0, 3, 1, 4).reshape(
        2, bsz, g, s // CMP_STRIDE, CMP_STRIDE * dh)
    pe2 = jnp.stack([cmp_pe_k, cmp_pe_v]).reshape(2, 2, CMP_STRIDE * dh)
    w1c = jnp.stack([cmp_w1_k, cmp_w1_v]).astype(BF16)
    w2c = jnp.stack([cmp_w2_k, cmp_w2_v]).astype(BF16)
    kv_cmp = _compress(x16, pe2, w1c, w2c)
    kcm = kv_cmp[0].astype(BF16)
    vct = kv_cmp[1].transpose(0, 1, 3, 2).astype(BF16)

    hg = HEADS_PER_GROUP
    qt = q.reshape(bsz, nq, TQ, g, hg, dh).transpose(0, 3, 1, 5, 4, 2).reshape(bsz, g, nq, dh, hg * TQ)
    sw5 = sw.reshape(bsz, s, 4, g, dh)
    ks = sw5[:, :, 0].transpose(0, 2, 1, 3)
    kw = sw5[:, :, 2].transpose(0, 2, 1, 3)
    ones = jnp.ones((bsz, g, nq, V_ROWS - dh, TQ), BF16)
    v_t = lambda v: jnp.concatenate(
        [v.reshape(bsz, nq, TQ, g, dh).transpose(0, 3, 1, 4, 2), ones], axis=3)
    glt = gl[:, :, :n_gl].reshape(bsz, s, g, hg * N_BRANCH).transpose(0, 2, 3, 1)
    o_att = _attention(qt, kcm, vct, ks, kw, v_t(sw5[:, :, 1]), v_t(sw5[:, :, 3]), glt)

    w_cat = jnp.concatenate([w_rg_a, w_rg_x], axis=-1).astype(BF16)
    b_cat = jnp.concatenate([b_rg_a.reshape(RNN_BLOCKS, 1, RNN_BLOCK_DIM),
                             b_rg_x.reshape(RNN_BLOCKS, 1, RNN_BLOCK_DIM)], axis=-1)
    o_rnn = _rglru(r_in, conv_w, conv_b.reshape(1, -1), w_cat, b_cat, lru_lambda.reshape(1, -1))

    x1 = _outproj(o_att, o_rnn, g_grp_att, g_grp_rnn, w_out.astype(BF16), x, mod6, g_post_mix)
    return _mlp(x1, g_pre_mlp, mod6, w_ff1.astype(BF16), w_ff2.astype(BF16), g_post_mlp)


def kernel(x, c, w_ada, b_ada, g_pre_mix, g_post_mix, g_pre_mlp, g_post_mlp, w_in, cmp_w1_k, cmp_w2_k, cmp_pe_k, cmp_w1_v, cmp_w2_v, cmp_pe_v, conv_w, conv_b, w_rg_a, b_rg_a, w_rg_x, b_rg_x, lru_lambda, g_grp_att, g_grp_rnn, w_out, w_ff1, w_ff2):
    depth = w_ada.shape[0]
    for l in range(depth):
        x = _layer(x, c, w_ada[l], b_ada[l], g_pre_mix[l], g_post_mix[l], g_pre_mlp[l], g_post_mlp[l],
                   w_in[l], cmp_w1_k[l], cmp_w2_k[l], cmp_pe_k[l], cmp_w1_v[l], cmp_w2_v[l], cmp_pe_v[l],
                   conv_w[l], conv_b[l], w_rg_a[l], b_rg_a[l], w_rg_x[l], b_rg_x[l], lru_lambda[l],
                   g_grp_att[l], g_grp_rnn[l], w_out[l], w_ff1[l], w_ff2[l])
    return x
```

```python
import functools

import jax
import jax.numpy as jnp
from jax import lax
from jax.experimental import pallas as pl
from jax.experimental.pallas import tpu as pltpu

F32 = jnp.float32
BF16 = jnp.bfloat16

D_MODEL = 2048
D_ATT = 1024
D_RNN = 1024
N_Q_HEADS = 16
N_KV_GROUPS = 4
HEADS_PER_GROUP = 4
HEAD_DIM = 64
D_KV = 256
CMP_BLOCK = 32
CMP_STRIDE = 16
SEL_BLOCK = 64
SEL_SHIFT = 6
SEL_TOP_N = 8
WINDOW = 512
N_BRANCH = 3
RNN_BLOCKS = 8
RNN_BLOCK_DIM = 128
CONV_WIDTH = 4
LRU_C = 8.0
D_FF = 4 * D_MODEL
EPS = 1e-6
NEG = -1e30
FORCE_SCORE = 1e9
LOG2_E = 1.4426950408889634

LANES = 128
SUBLANES = 8
VMEM_LIMIT = 56 * 1024 * 1024

TQ = 256
N_CMP_PAD = 128
N_SEL = 32
GL_PAD = 128
TT = 256


def _params(sem):
    return pltpu.CompilerParams(dimension_semantics=sem, vmem_limit_bytes=VMEM_LIMIT)


def _dot(a, b):
    return jnp.dot(a, b, preferred_element_type=F32)


def _rms(x, g):
    return x * lax.rsqrt(jnp.mean(x * x, axis=-1, keepdims=True) + EPS) * g


def _ada_kernel(c_ref, w_ref, b_ref, o_ref):
    c = c_ref[...]
    ca = (c * jax.nn.sigmoid(c)).astype(BF16)
    o_ref[...] = _dot(ca, w_ref[...].astype(BF16)) + b_ref[...]


def _ada(c, w, b):
    bsz, d = c.shape
    n = w.shape[1]
    tn = 1024
    return pl.pallas_call(
        _ada_kernel,
        grid=(n // tn,),
        in_specs=[
            pl.BlockSpec((bsz, d), lambda j: (0, 0)),
            pl.BlockSpec((d, tn), lambda j: (0, j)),
            pl.BlockSpec((1, tn), lambda j: (0, j)),
        ],
        out_specs=pl.BlockSpec((bsz, tn), lambda j: (0, j)),
        out_shape=jax.ShapeDtypeStruct((bsz, n), F32),
        compiler_params=_params(("arbitrary",)),
        name="ada_mod",
    )(c, w, b.reshape(1, n))


C_Q = (0, D_ATT)
C_CMP = (C_Q[1], C_Q[1] + 2 * D_KV)
C_SW = (C_CMP[1], C_CMP[1] + 4 * D_KV)
C_R = (C_SW[1], C_SW[1] + 2 * D_RNN)
C_GL = (C_R[1], C_R[1] + GL_PAD)
D_IN_PAD = C_GL[1]


def _inproj_kernel(x_ref, g_ref, sc_ref, sh_ref, w_ref, q_ref, cmp_ref, sw_ref, r_ref, gl_ref):
    x = x_ref[0]
    h = _rms(x, g_ref[...]) * (1.0 + sc_ref[0, 0]) + sh_ref[0, 0]
    hb = h.astype(BF16)
    q_ref[0] = (_dot(hb, w_ref[:, C_Q[0]:C_Q[1]]) * (HEAD_DIM ** -0.5 * LOG2_E)).astype(BF16)
    cmp_ref[0] = _dot(hb, w_ref[:, C_CMP[0]:C_CMP[1]])
    sw_ref[0] = _dot(hb, w_ref[:, C_SW[0]:C_SW[1]]).astype(BF16)
    half = (C_R[0] + C_R[1]) // 2
    r_ref[0, :, 0:D_RNN] = _dot(hb, w_ref[:, C_R[0]:half])
    r_ref[0, :, D_RNN:2 * D_RNN] = _dot(hb, w_ref[:, half:C_R[1]])
    gl_ref[0] = _dot(hb, w_ref[:, C_GL[0]:C_GL[1]])


def _inproj(x, g, mod6, w_perm, tm=512):
    bsz, s, d = x.shape
    row = lambda width: pl.BlockSpec((1, tm, width), lambda b, i: (b, i, 0))
    shp = lambda width, dt: jax.ShapeDtypeStruct((bsz, s, width), dt)
    return pl.pallas_call(
        _inproj_kernel,
        grid=(bsz, s // tm),
        in_specs=[
            row(d),
            pl.BlockSpec((1, d), lambda b, i: (0, 0)),
            pl.BlockSpec((1, 1, 1, d), lambda b, i: (1, b, 0, 0)),
            pl.BlockSpec((1, 1, 1, d), lambda b, i: (0, b, 0, 0)),
            pl.BlockSpec((d, D_IN_PAD), lambda b, i: (0, 0), pipeline_mode=pl.Buffered(1)),
        ],
        out_specs=[row(D_ATT), row(2 * D_KV), row(4 * D_KV), row(2 * D_RNN), row(GL_PAD)],
        out_shape=[shp(D_ATT, BF16), shp(2 * D_KV, F32), shp(4 * D_KV, BF16),
                   shp(2 * D_RNN, F32), shp(GL_PAD, F32)],
        compiler_params=_params(("parallel", "arbitrary")),
        name="in_proj",
    )(x, g.reshape(1, d), mod6, mod6, w_perm)


def _compress_kernel(x_ref, pe_ref, w1_ref, w2_ref, o_ref):
    x = x_ref[0, 0, 0]
    half = x.shape[1]
    xa = (x + pe_ref[0, 0:1, :]).astype(BF16)
    xb = (x + pe_ref[0, 1:2, :]).astype(BF16)
    first = _dot(xa, w1_ref[0, 0:half, :])
    second = _dot(xb, w1_ref[0, half:2 * half, :])
    n_rows = x.shape[0]
    hid = first + pltpu.roll(second, n_rows - 1, axis=0)
    act = jax.nn.gelu(hid)
    o_ref[0, 0, 0] = _dot(act.astype(BF16), w2_ref[0])


def _compress(x16, pe2, w1, w2):
    two, bsz, g, nr, width = x16.shape
    hid = w1.shape[2]
    dh = w2.shape[2]
    return pl.pallas_call(
        _compress_kernel,
        grid=(two, bsz, g),
        in_specs=[
            pl.BlockSpec((1, 1, 1, nr, width), lambda t, b, j: (t, b, j, 0, 0)),
            pl.BlockSpec((1, 2, width), lambda t, b, j: (t, 0, 0)),
            pl.BlockSpec((1, 2 * width, hid), lambda t, b, j: (t, 0, 0)),
            pl.BlockSpec((1, hid, dh), lambda t, b, j: (t, 0, 0)),
        ],
        out_specs=pl.BlockSpec((1, 1, 1, nr, dh), lambda t, b, j: (t, b, j, 0, 0)),
        out_shape=jax.ShapeDtypeStruct((two, bsz, g, nr, dh), F32),
        compiler_params=_params(("arbitrary", "arbitrary", "arbitrary")),
        name="compress_kv",
    )(x16, pe2, w1, w2)


V_ROWS = HEAD_DIM + 16


def _cmp_select_kernel(qt_ref, kc_ref, vct_ref, oc_ref, bias_ref):
    hg = HEADS_PER_GROUP
    kc = kc_ref[0, 0]
    vct = vct_ref[0, 0]
    jj = lax.broadcasted_iota(jnp.int32, (N_SEL, N_CMP_PAD), 0) * SEL_BLOCK
    nn = lax.broadcasted_iota(jnp.int32, (N_SEL, N_CMP_PAD), 1) * CMP_STRIDE
    ov = jnp.minimum(nn + CMP_BLOCK, jj + SEL_BLOCK) - jnp.maximum(nn, jj)
    w_sel = (jnp.maximum(ov, 0).astype(F32) * (1.0 / CMP_BLOCK)).astype(BF16)
    nrow = lax.broadcasted_iota(jnp.int32, (N_CMP_PAD, TQ), 0)
    blk = lax.broadcasted_iota(jnp.int32, (N_SEL, TQ), 0)

    for qi in range(qt_ref.shape[2]):
        qt = qt_ref[0, 0, qi]
        pos = qi * TQ + lax.broadcasted_iota(jnp.int32, (1, TQ), 1)
        mask_c = nrow * CMP_STRIDE + (CMP_BLOCK - 1) <= pos

        def per_head_where(a, fill):
            return jnp.concatenate(
                [jnp.where(mask_c, a[:, h * TQ:(h + 1) * TQ], fill) for h in range(hg)], axis=1)

        s = per_head_where(_dot(kc, qt), NEG)
        e = jnp.exp2(s - jnp.max(s, axis=0, keepdims=True))
        p = per_head_where(e * (1.0 / jnp.sum(e, axis=0, keepdims=True)), 0.0)
        oc_ref[0, 0, qi] = _dot(vct, p.astype(BF16))
        psum = p[:, 0:TQ]
        for h in range(1, hg):
            psum = psum + p[:, h * TQ:(h + 1) * TQ]

        p_hi = psum.astype(BF16)
        p_lo = (psum - p_hi.astype(F32)).astype(BF16)
        imp = _dot(w_sel, p_hi) + _dot(w_sel, p_lo)
        cur = lax.shift_right_logical(pos, SEL_SHIFT)
        free = jnp.where(blk * SEL_BLOCK <= pos, imp, -FORCE_SCORE)
        score = jnp.where(blk == 0, FORCE_SCORE,
                          jnp.where(blk == cur, FORCE_SCORE, jnp.where(blk == cur - 1, FORCE_SCORE, free)))
        rank = jnp.zeros((N_SEL, TQ), F32)
        for k in range(N_SEL):
            sk = score[k:k + 1, :]
            tie = jnp.where(blk > k, 1.0, 0.0)
            rank = rank + jnp.where(sk > score, 1.0, jnp.where(sk == score, tie, 0.0))
        bias_ref[0, 0, qi] = jnp.where(rank < SEL_TOP_N, 0.0, NEG).astype(BF16)


def _cmp_select(qt, kc, vct):
    bsz, g, nq, dh, wide = qt.shape
    per_bg = lambda *shape: pl.BlockSpec((1, 1) + shape, lambda b, j: (b, j) + (0,) * len(shape))
    return pl.pallas_call(
        _cmp_select_kernel,
        grid=(bsz, g),
        in_specs=[per_bg(nq, dh, wide), per_bg(N_CMP_PAD, dh), per_bg(dh, N_CMP_PAD)],
        out_specs=[per_bg(nq, dh, wide), per_bg(nq, N_SEL, TQ)],
        out_shape=[jax.ShapeDtypeStruct((bsz, g, nq, dh, wide), F32),
                   jax.ShapeDtypeStruct((bsz, g, nq, N_SEL, TQ), BF16)],
        compiler_params=_params(("parallel", "arbitrary")),
        name="cmp_select",
    )(qt, kc, vct)


def _attn_kernel(qt_ref, oc_ref, bias_ref, ks_ref, kw_ref, vst_ref, vwt_ref, gl_ref, o_ref,
                 m_ref, acc_ref, sc_ref):
    qi = pl.program_id(2)
    last = pl.num_programs(2) - 1
    hg = HEADS_PER_GROUP
    dh = HEAD_DIM
    wide = hg * TQ
    qt = qt_ref[0, 0, 0]

    def per_head_where(mask, a, fill):
        return jnp.concatenate(
            [jnp.where(mask, a[:, h * TQ:(h + 1) * TQ], fill) for h in range(hg)], axis=1)

    def block_bias(value):
        return jnp.full((N_SEL, wide), value, F32).astype(BF16)

    def scores(k_ref, kt, bias):
        k_t = k_ref[0, 0, pl.ds(pl.multiple_of(kt * TQ, TQ), TQ), :]
        return _dot(k_t, jnp.concatenate([qt, bias], axis=0))

    rel = (lax.broadcasted_iota(jnp.int32, (TQ, TQ), 0)
           - lax.broadcasted_iota(jnp.int32, (TQ, TQ), 1))

    k1 = jnp.maximum(qi - 1, 0)
    k2 = jnp.maximum(qi - 2, 0)
    s0 = per_head_where(rel <= 0, scores(kw_ref, qi, block_bias(0.0)), NEG)
    s1 = scores(kw_ref, k1, block_bias(jnp.where(qi >= 1, 0.0, NEG)))
    s2 = per_head_where(rel > 0, scores(kw_ref, k2, block_bias(jnp.where(qi >= 2, 0.0, NEG))), NEG)
    m_w = jnp.maximum(jnp.max(s0, axis=0, keepdims=True),
                      jnp.maximum(jnp.max(s1, axis=0, keepdims=True), jnp.max(s2, axis=0, keepdims=True)))
    p_w = jnp.concatenate([jnp.exp2(s2 - m_w), jnp.exp2(s1 - m_w), jnp.exp2(s0 - m_w)], axis=0).astype(BF16)
    v_w = jnp.concatenate([vwt_ref[0, 0, k2], vwt_ref[0, 0, k1], vwt_ref[0, 0, qi]], axis=1)
    acc_ref[1] = _dot(v_w, p_w)

    bias_sel = jnp.concatenate([bias_ref[0, 0, 0]] * hg, axis=1)

    sd = per_head_where(rel <= 0, scores(ks_ref, qi, bias_sel), NEG)
    m_d = jnp.max(sd, axis=0, keepdims=True)
    m_ref[...] = m_d
    acc_ref[0] = _dot(vst_ref[0, 0, qi], jnp.exp2(sd - m_d).astype(BF16))

    def sel_scores(kt):
        pad_bias = block_bias(jnp.where(kt < qi, 0.0, NEG))
        return scores(ks_ref, jnp.minimum(kt, last), jnp.minimum(bias_sel, pad_bias))

    def sel_accumulate(sc, kt):
        m_old = m_ref[...]
        m_new = jnp.maximum(m_old, jnp.max(sc, axis=0, keepdims=True))
        m_ref[...] = m_new
        acc_ref[0] = (jnp.exp2(m_old - m_new) * acc_ref[0]
                      + _dot(vst_ref[0, 0, jnp.minimum(kt, last)], jnp.exp2(sc - m_new).astype(BF16)))

    sc_ref[0] = sel_scores(0)

    def sel_pair(pi, carry):
        kt = 2 * pi
        sc_ref[1] = sel_scores(kt + 1)
        sel_accumulate(sc_ref[0], kt)
        sc_ref[0] = sel_scores(kt + 2)
        sel_accumulate(sc_ref[1], kt + 1)
        return carry

    lax.fori_loop(0, (qi + 1) // 2, sel_pair, 0)

    gate = jax.nn.sigmoid(gl_ref[0, 0])
    o_cmp = oc_ref[0, 0, 0]
    acc_s = acc_ref[0]
    acc_w = acc_ref[1]
    o_sel = acc_s[0:dh, :] * (1.0 / acc_s[dh:dh + 1, :])
    o_win = acc_w[0:dh, :] * (1.0 / acc_w[dh:dh + 1, :])
    outs = []
    for h in range(hg):
        lanes = slice(h * TQ, (h + 1) * TQ)
        g0 = gate[3 * h:3 * h + 1, :]
        g1 = gate[3 * h + 1:3 * h + 2, :]
        g2 = gate[3 * h + 2:3 * h + 3, :]
        outs.append(g0 * o_cmp[:, lanes] + g1 * o_sel[:, lanes] + g2 * o_win[:, lanes])
    o_ref[0] = jnp.concatenate(outs, axis=0).T


def _attention(qt, o_cmp, bias, ks, kw, vst, vwt, glt):
    bsz, g, nq = qt.shape[:3]
    s = nq * TQ
    width = HEADS_PER_GROUP * HEAD_DIM
    wide = HEADS_PER_GROUP * TQ
    per_bg = lambda *shape: pl.BlockSpec((1, 1) + shape, lambda b, j, i: (b, j) + (0,) * len(shape))
    per_tile = lambda *shape: pl.BlockSpec((1, 1, 1) + shape, lambda b, j, i: (b, j, i) + (0,) * len(shape))
    return pl.pallas_call(
        _attn_kernel,
        grid=(bsz, g, nq),
        in_specs=[
            per_tile(HEAD_DIM, wide),
            per_tile(HEAD_DIM, wide),
            per_tile(N_SEL, TQ),
            per_bg(s, HEAD_DIM + N_SEL),
            per_bg(s, HEAD_DIM + N_SEL),
            per_bg(nq, V_ROWS, TQ),
            per_bg(nq, V_ROWS, TQ),
            pl.BlockSpec((1, 1, HEADS_PER_GROUP * N_BRANCH, TQ), lambda b, j, i: (b, j, 0, i)),
        ],
        out_specs=pl.BlockSpec((1, TQ, width), lambda b, j, i: (b, i, j)),
        out_shape=jax.ShapeDtypeStruct((bsz, s, D_ATT), F32),
        scratch_shapes=[
            pltpu.VMEM((1, wide), F32),
            pltpu.VMEM((2, V_ROWS, wide), F32),
            pltpu.VMEM((2, TQ, wide), F32),
        ],
        compiler_params=_params(("parallel", "parallel", "arbitrary")),
        name="nsa_attention",
    )(qt, o_cmp, bias, ks, kw, vst, vwt, glt)


def _rglru_kernel(x_ref, y_ref, cw_ref, cb_ref, w_ref, b_ref, lam_ref, o_ref,
                  xpad_ref, a_ref, bt_ref):
    s = x_ref.shape[1]
    c = RNN_BLOCK_DIM
    pad = SUBLANES
    xpad_ref[0:pad, :] = jnp.zeros((pad, c), F32)
    xpad_ref[pad:pad + s, :] = x_ref[0]
    cw = cw_ref[...]
    w = w_ref[0]
    bias = b_ref[0]
    lam = lam_ref[...]
    neg_softplus = -(jnp.maximum(-lam, 0.0) + jnp.log1p(jnp.exp(-jnp.abs(lam))))
    sub = lax.broadcasted_iota(jnp.int32, (TT, c), 0) & (SUBLANES - 1)

    for ci in range(s // TT):
        t0 = ci * TT
        xc = cb_ref[...] + sum(
            xpad_ref[t0 + pad - (CONV_WIDTH - 1) + k:t0 + pad - (CONV_WIDTH - 1) + k + TT, :] * cw[k:k + 1, :]
            for k in range(CONV_WIDTH))
        gates = jax.nn.sigmoid(_dot(xc.astype(BF16), w) + bias)
        r = gates[:, 0:c]
        i = gates[:, c:2 * c]
        log_a = LRU_C * r * neg_softplus
        a = jnp.exp(log_a)
        bt = jnp.sqrt(-jnp.tanh(log_a) * (a * a + 1.0)) * (i * xc)
        for d in (1, 2, 4):
            keep = sub >= d
            a_prev = jnp.where(keep, pltpu.roll(a, d, axis=0), 1.0)
            b_prev = jnp.where(keep, pltpu.roll(bt, d, axis=0), 0.0)
            bt = bt + a * b_prev
            a = a * a_prev
        a_ref[t0:t0 + TT, :] = a
        bt_ref[t0:t0 + TT, :] = bt

    def group(gi, h):
        r0 = pl.multiple_of(gi * SUBLANES, SUBLANES)
        hg = bt_ref[pl.ds(r0, SUBLANES), :] + a_ref[pl.ds(r0, SUBLANES), :] * h
        bt_ref[pl.ds(r0, SUBLANES), :] = hg
        return hg[SUBLANES - 1:SUBLANES, :]

    lax.fori_loop(0, s // SUBLANES, group, jnp.zeros((1, c), F32), unroll=8)

    for ci in range(s // TT):
        t0 = ci * TT
        o_ref[0, t0:t0 + TT, :] = jax.nn.gelu(y_ref[0, t0:t0 + TT, :]) * bt_ref[t0:t0 + TT, :]


def _rglru(r_in, conv_w, conv_b, w_cat, b_cat, lam):
    bsz, s, _ = r_in.shape
    c = RNN_BLOCK_DIM
    nb = RNN_BLOCKS
    return pl.pallas_call(
        _rglru_kernel,
        grid=(bsz, nb),
        in_specs=[
            pl.BlockSpec((1, s, c), lambda b, j: (b, 0, j)),
            pl.BlockSpec((1, s, c), lambda b, j: (b, 0, nb + j)),
            pl.BlockSpec((CONV_WIDTH, c), lambda b, j: (0, j)),
            pl.BlockSpec((1, c), lambda b, j: (0, j)),
            pl.BlockSpec((1, c, 2 * c), lambda b, j: (j, 0, 0)),
            pl.BlockSpec((1, 1, 2 * c), lambda b, j: (j, 0, 0)),
            pl.BlockSpec((1, c), lambda b, j: (0, j)),
        ],
        out_specs=pl.BlockSpec((1, s, c), lambda b, j: (b, 0, j)),
        out_shape=jax.ShapeDtypeStruct((bsz, s, D_RNN), F32),
        scratch_shapes=[
            pltpu.VMEM((s + SUBLANES, c), F32),
            pltpu.VMEM((s, c), F32),
            pltpu.VMEM((s, c), F32),
        ],
        compiler_params=_params(("parallel", "arbitrary")),
        name="rg_lru",
    )(r_in, r_in, conv_w, conv_b, w_cat, b_cat, lam)


def _outproj_kernel(oa_ref, or_ref, ga_ref, gr_ref, w_ref, x_ref, gt_ref, gp_ref, o_ref):
    a = _rms(oa_ref[0], ga_ref[...]).astype(BF16)
    r = _rms(or_ref[0], gr_ref[...]).astype(BF16)
    mix = _dot(a, w_ref[0:D_ATT, :]) + _dot(r, w_ref[D_ATT:D_ATT + D_RNN, :])
    o_ref[0] = x_ref[0] + gt_ref[0, 0] * _rms(mix, gp_ref[...])


def _outproj(o_att, o_rnn, g_att, g_rnn, w_out, x, mod6, g_post, tm=256):
    bsz, s, d = x.shape
    row = lambda width: pl.BlockSpec((1, tm, width), lambda b, i: (b, i, 0))
    vec = lambda width: pl.BlockSpec((1, width), lambda b, i: (0, 0))
    return pl.pallas_call(
        _outproj_kernel,
        grid=(bsz, s // tm),
        in_specs=[
            row(D_ATT), row(D_RNN), vec(D_ATT), vec(D_RNN),
            pl.BlockSpec((D_ATT + D_RNN, d), lambda b, i: (0, 0), pipeline_mode=pl.Buffered(1)),
            row(d),
            pl.BlockSpec((1, 1, 1, d), lambda b, i: (2, b, 0, 0)),
            vec(d),
        ],
        out_specs=row(d),
        out_shape=jax.ShapeDtypeStruct((bsz, s, d), F32),
        compiler_params=_params(("parallel", "arbitrary")),
        name="out_proj",
    )(o_att, o_rnn, g_att.reshape(1, -1), g_rnn.reshape(1, -1), w_out, x, mod6, g_post.reshape(1, d))


def _mlp_kernel(x_ref, g_ref, sc_ref, sh_ref, w1_ref, w2_ref, gt_ref, gp_ref, o_ref, h_ref, acc_ref):
    j = pl.program_id(2)

    @pl.when(j == 0)
    def _():
        h = _rms(x_ref[0], g_ref[...]) * (1.0 + sc_ref[0, 0]) + sh_ref[0, 0]
        h_ref[...] = h.astype(BF16)

    u = jnp.maximum(_dot(h_ref[...], w1_ref[...]), 0.0)
    part = _dot((u * u).astype(BF16), w2_ref[...])

    @pl.when(j == 0)
    def _():
        acc_ref[...] = part

    @pl.when(j > 0)
    def _():
        acc_ref[...] += part

    @pl.when(j == pl.num_programs(2) - 1)
    def _():
        o_ref[0] = x_ref[0] + gt_ref[0, 0] * _rms(acc_ref[...], gp_ref[...])


def _mlp(x, g_pre, mod6, w1, w2, g_post, tm=512, tf=1024):
    bsz, s, d = x.shape
    dff = w1.shape[1]
    row = pl.BlockSpec((1, tm, d), lambda b, i, j: (b, i, 0))
    vec = pl.BlockSpec((1, d), lambda b, i, j: (0, 0))
    modk = lambda k: pl.BlockSpec((1, 1, 1, d), lambda b, i, j: (k, b, 0, 0))
    return pl.pallas_call(
        _mlp_kernel,
        grid=(bsz, s // tm, dff // tf),
        in_specs=[
            row, vec, modk(4), modk(3),
            pl.BlockSpec((d, tf), lambda b, i, j: (0, j)),
            pl.BlockSpec((tf, d), lambda b, i, j: (j, 0)),
            modk(5), vec,
        ],
        out_specs=row,
        out_shape=jax.ShapeDtypeStruct((bsz, s, d), F32),
        scratch_shapes=[pltpu.VMEM((tm, d), BF16), pltpu.VMEM((tm, d), F32)],
        compiler_params=_params(("parallel", "parallel", "arbitrary")),
        name="mlp",
    )(x, g_pre.reshape(1, d), mod6, mod6, w1, w2, mod6, g_post.reshape(1, d))


def _layer(x, c, w_ada, b_ada, g_pre_mix, g_post_mix, g_pre_mlp, g_post_mlp, w_in,
           cmp_w1_k, cmp_w2_k, cmp_pe_k, cmp_w1_v, cmp_w2_v, cmp_pe_v,
           conv_w, conv_b, w_rg_a, b_rg_a, w_rg_x, b_rg_x, lru_lambda,
           g_grp_att, g_grp_rnn, w_out, w_ff1, w_ff2):
    bsz, s, d = x.shape
    g = N_KV_GROUPS
    dh = HEAD_DIM
    nq = s // TQ

    mod = _ada(c, w_ada, b_ada)
    mod6 = mod.reshape(bsz, 6, 1, d).transpose(1, 0, 2, 3)

    n_gl = N_BRANCH * N_Q_HEADS
    c_gl0 = D_ATT + 6 * D_KV
    w_perm = jnp.concatenate(
        [w_in[:, :c_gl0], w_in[:, c_gl0 + n_gl:], w_in[:, c_gl0:c_gl0 + n_gl],
         jnp.zeros((d, GL_PAD - n_gl), w_in.dtype)], axis=1).astype(BF16)
    q, kvc, sw, r_in, gl = _inproj(x, g_pre_mix, mod6, w_perm)

    x16 = kvc.reshape(bsz, s, 2, g, dh).transpose(2, 0, 3, 1, 4).reshape(
        2, bsz, g, s // CMP_STRIDE, CMP_STRIDE * dh)
    pe2 = jnp.stack([cmp_pe_k, cmp_pe_v]).reshape(2, 2, CMP_STRIDE * dh)
    w1c = jnp.stack([cmp_w1_k, cmp_w1_v]).astype(BF16)
    w2c = jnp.stack([cmp_w2_k, cmp_w2_v]).astype(BF16)
    kv_cmp = _compress(x16, pe2, w1c, w2c)
    kcm = kv_cmp[0].astype(BF16)
    vct = kv_cmp[1].transpose(0, 1, 3, 2).astype(BF16)

    hg = HEADS_PER_GROUP
    qt = q.reshape(bsz, nq, TQ, g, hg, dh).transpose(0, 3, 1, 5, 4, 2).reshape(bsz, g, nq, dh, hg * TQ)
    sw5 = sw.reshape(bsz, s, 4, g, dh)
    block_of_key = jnp.arange(s, dtype=jnp.int32)[:, None] // SEL_BLOCK
    onehot = jnp.broadcast_to(
        (block_of_key == jnp.arange(N_SEL, dtype=jnp.int32)[None, :]).astype(BF16), (bsz, g, s, N_SEL))
    ks = jnp.concatenate([sw5[:, :, 0].transpose(0, 2, 1, 3), onehot], axis=-1)
    kw = jnp.concatenate([sw5[:, :, 2].transpose(0, 2, 1, 3), onehot], axis=-1)
    ones = jnp.ones((bsz, g, nq, V_ROWS - dh, TQ), BF16)
    v_t = lambda v: jnp.concatenate(
        [v.reshape(bsz, nq, TQ, g, dh).transpose(0, 3, 1, 4, 2), ones], axis=3)
    glt = gl[:, :, :n_gl].reshape(bsz, s, g, hg * N_BRANCH).transpose(0, 2, 3, 1)
    o_cmp, sel_bias = _cmp_select(qt, kcm, vct)
    o_att = _attention(qt, o_cmp, sel_bias, ks, kw, v_t(sw5[:, :, 1]), v_t(sw5[:, :, 3]), glt)

    w_cat = jnp.concatenate([w_rg_a, w_rg_x], axis=-1).astype(BF16)
    b_cat = jnp.concatenate([b_rg_a.reshape(RNN_BLOCKS, 1, RNN_BLOCK_DIM),
                             b_rg_x.reshape(RNN_BLOCKS, 1, RNN_BLOCK_DIM)], axis=-1)
    o_rnn = _rglru(r_in, conv_w, conv_b.reshape(1, -1), w_cat, b_cat, lru_lambda.reshape(1, -1))

    x1 = _outproj(o_att, o_rnn, g_grp_att, g_grp_rnn, w_out.astype(BF16), x, mod6, g_post_mix)
    return _mlp(x1, g_pre_mlp, mod6, w_ff1.astype(BF16), w_ff2.astype(BF16), g_post_mlp)


def kernel(x, c, w_ada, b_ada, g_pre_mix, g_post_mix, g_pre_mlp, g_post_mlp, w_in, cmp_w1_k, cmp_w2_k, cmp_pe_k, cmp_w1_v, cmp_w2_v, cmp_pe_v, conv_w, conv_b, w_rg_a, b_rg_a, w_rg_x, b_rg_x, lru_lambda, g_grp_att, g_grp_rnn, w_out, w_ff1, w_ff2):
    depth = w_ada.shape[0]
    for l in range(depth):
        x = _layer(x, c, w_ada[l], b_ada[l], g_pre_mix[l], g_post_mix[l], g_pre_mlp[l], g_post_mlp[l],
                   w_in[l], cmp_w1_k[l], cmp_w2_k[l], cmp_pe_k[l], cmp_w1_v[l], cmp_w2_v[l], cmp_pe_v[l],
                   conv_w[l], conv_b[l], w_rg_a[l], b_rg_a[l], w_rg_x[l], b_rg_x[l], lru_lambda[l],
                   g_grp_att[l], g_grp_rnn[l], w_out[l], w_ff1[l], w_ff2[l])
    return x
```

```python
import functools

import jax
import jax.numpy as jnp
from jax import lax
from jax.experimental import pallas as pl
from jax.experimental.pallas import tpu as pltpu

F32 = jnp.float32
BF16 = jnp.bfloat16

D_MODEL = 2048
D_ATT = 1024
D_RNN = 1024
N_Q_HEADS = 16
N_KV_GROUPS = 4
HEADS_PER_GROUP = 4
HEAD_DIM = 64
D_KV = 256
CMP_BLOCK = 32
CMP_STRIDE = 16
SEL_BLOCK = 64
SEL_SHIFT = 6
SEL_TOP_N = 8
WINDOW = 512
N_BRANCH = 3
RNN_BLOCKS = 8
RNN_BLOCK_DIM = 128
CONV_WIDTH = 4
LRU_C = 8.0
D_FF = 4 * D_MODEL
EPS = 1e-6
NEG = -1e30
FORCE_SCORE = 1e9
LOG2_E = 1.4426950408889634

LANES = 128
SUBLANES = 8
VMEM_LIMIT = 56 * 1024 * 1024

TQ = 256
N_CMP_PAD = 128
N_SEL = 32
GL_PAD = 128
TT = 256


def _params(sem):
    return pltpu.CompilerParams(dimension_semantics=sem, vmem_limit_bytes=VMEM_LIMIT)


def _dot(a, b):
    return jnp.dot(a, b, preferred_element_type=F32)


def _rms(x, g):
    return x * lax.rsqrt(jnp.mean(x * x, axis=-1, keepdims=True) + EPS) * g


def _ada_kernel(c_ref, w_ref, b_ref, o_ref):
    c = c_ref[...]
    ca = (c * jax.nn.sigmoid(c)).astype(BF16)
    o_ref[...] = _dot(ca, w_ref[...].astype(BF16)) + b_ref[...]


def _ada(c, w, b):
    bsz, d = c.shape
    n = w.shape[1]
    tn = 1024
    return pl.pallas_call(
        _ada_kernel,
        grid=(n // tn,),
        in_specs=[
            pl.BlockSpec((bsz, d), lambda j: (0, 0)),
            pl.BlockSpec((d, tn), lambda j: (0, j)),
            pl.BlockSpec((1, tn), lambda j: (0, j)),
        ],
        out_specs=pl.BlockSpec((bsz, tn), lambda j: (0, j)),
        out_shape=jax.ShapeDtypeStruct((bsz, n), F32),
        compiler_params=_params(("arbitrary",)),
        name="ada_mod",
    )(c, w, b.reshape(1, n))


C_Q = (0, D_ATT)
C_CMP = (C_Q[1], C_Q[1] + 2 * D_KV)
C_K = (C_CMP[1], C_CMP[1] + 2 * D_KV)
C_V = (C_K[1], C_K[1] + 2 * D_KV)
C_R = (C_V[1], C_V[1] + 2 * D_RNN)
C_GL = (C_R[1], C_R[1] + GL_PAD)
D_IN_PAD = C_GL[1]
K_ROW = LANES
V_ROWS = HEAD_DIM + 16
N_GATE = HEADS_PER_GROUP * N_BRANCH


def _inproj_kernel(x_ref, g_ref, sc_ref, sh_ref, w_ref,
                   qt_ref, cmp_ref, ks_ref, kw_ref, vst_ref, vwt_ref, r_ref, glt_ref):
    tm = x_ref.shape[1]
    ng, hg, dh = N_KV_GROUPS, HEADS_PER_GROUP, HEAD_DIM
    x = x_ref[0]
    h = _rms(x, g_ref[...]) * (1.0 + sc_ref[0, 0]) + sh_ref[0, 0]
    hb = h.astype(BF16)
    tiles = [slice(u * TQ, (u + 1) * TQ) for u in range(tm // TQ)]

    q_t = (_dot(hb, w_ref[:, C_Q[0]:C_Q[1]]) * (dh ** -0.5 * LOG2_E)).T.astype(BF16)
    for j in range(ng):
        for u, cols in enumerate(tiles):
            qt_ref[0, j, u] = jnp.concatenate(
                [q_t[(hg * j + hh) * dh:(hg * j + hh + 1) * dh, cols] for hh in range(hg)], axis=1)

    cmp_ref[0] = _dot(hb, w_ref[:, C_CMP[0]:C_CMP[1]])

    keys = _dot(hb, w_ref[:, C_K[0]:C_K[1]])
    row_pos = pl.program_id(1) * tm + lax.broadcasted_iota(jnp.int32, (tm, K_ROW - dh), 0)
    lane = lax.broadcasted_iota(jnp.int32, (tm, K_ROW - dh), 1)
    onehot = jnp.where(lax.shift_right_logical(row_pos, SEL_SHIFT) == lane, 1.0, 0.0)
    for j in range(ng):
        ks_ref[0, j] = jnp.concatenate([keys[:, j * dh:(j + 1) * dh], onehot], axis=1).astype(BF16)
        kw_ref[0, j] = jnp.concatenate(
            [keys[:, D_KV + j * dh:D_KV + (j + 1) * dh], onehot], axis=1).astype(BF16)

    v_t = _dot(hb, w_ref[:, C_V[0]:C_V[1]]).T.astype(BF16)
    ones = jnp.ones((V_ROWS - dh, TQ), BF16)
    for j in range(ng):
        for u, cols in enumerate(tiles):
            vst_ref[0, j, u] = jnp.concatenate([v_t[j * dh:(j + 1) * dh, cols], ones], axis=0)
            vwt_ref[0, j, u] = jnp.concatenate([v_t[D_KV + j * dh:D_KV + (j + 1) * dh, cols], ones], axis=0)

    half = (C_R[0] + C_R[1]) // 2
    r_ref[0, :, 0:D_RNN] = _dot(hb, w_ref[:, C_R[0]:half])
    r_ref[0, :, D_RNN:2 * D_RNN] = _dot(hb, w_ref[:, half:C_R[1]])

    gl_t = _dot(hb, w_ref[:, C_GL[0]:C_GL[1]]).T
    for j in range(ng):
        glt_ref[0, j] = gl_t[N_GATE * j:N_GATE * (j + 1), :]


def _inproj(x, g, mod6, w_perm, tm=512):
    bsz, s, d = x.shape
    ng = N_KV_GROUPS
    nt = tm // TQ
    row = lambda width: pl.BlockSpec((1, tm, width), lambda b, i: (b, i, 0))
    grp_rows = pl.BlockSpec((1, ng, tm, K_ROW), lambda b, i: (b, 0, i, 0))
    grp_tiles = lambda r, c: pl.BlockSpec((1, ng, nt, r, c), lambda b, i: (b, 0, i, 0, 0))
    sds = jax.ShapeDtypeStruct
    return pl.pallas_call(
        _inproj_kernel,
        grid=(bsz, s // tm),
        in_specs=[
            row(d),
            pl.BlockSpec((1, d), lambda b, i: (0, 0)),
            pl.BlockSpec((1, 1, 1, d), lambda b, i: (1, b, 0, 0)),
            pl.BlockSpec((1, 1, 1, d), lambda b, i: (0, b, 0, 0)),
            pl.BlockSpec((d, D_IN_PAD), lambda b, i: (0, 0), pipeline_mode=pl.Buffered(1)),
        ],
        out_specs=[
            grp_tiles(HEAD_DIM, HEADS_PER_GROUP * TQ), row(2 * D_KV), grp_rows, grp_rows,
            grp_tiles(V_ROWS, TQ), grp_tiles(V_ROWS, TQ), row(2 * D_RNN),
            pl.BlockSpec((1, ng, N_GATE, tm), lambda b, i: (b, 0, 0, i)),
        ],
        out_shape=[
            sds((bsz, ng, s // TQ, HEAD_DIM, HEADS_PER_GROUP * TQ), BF16),
            sds((bsz, s, 2 * D_KV), F32),
            sds((bsz, ng, s, K_ROW), BF16), sds((bsz, ng, s, K_ROW), BF16),
            sds((bsz, ng, s // TQ, V_ROWS, TQ), BF16), sds((bsz, ng, s // TQ, V_ROWS, TQ), BF16),
            sds((bsz, s, 2 * D_RNN), F32),
            sds((bsz, ng, N_GATE, s), F32),
        ],
        compiler_params=_params(("parallel", "arbitrary")),
        name="in_proj",
    )(x, g.reshape(1, d), mod6, mod6, w_perm)


def _compress_kernel(x_ref, pe_ref, w1_ref, w2_ref, o_ref):
    x = x_ref[0, 0, 0]
    half = x.shape[1]
    xa = (x + pe_ref[0, 0:1, :]).astype(BF16)
    xb = (x + pe_ref[0, 1:2, :]).astype(BF16)
    first = _dot(xa, w1_ref[0, 0:half, :])
    second = _dot(xb, w1_ref[0, half:2 * half, :])
    n_rows = x.shape[0]
    hid = first + pltpu.roll(second, n_rows - 1, axis=0)
    act = jax.nn.gelu(hid)
    o_ref[0, 0, 0] = _dot(act.astype(BF16), w2_ref[0])


def _compress(x16, pe2, w1, w2):
    two, bsz, g, nr, width = x16.shape
    hid = w1.shape[2]
    dh = w2.shape[2]
    return pl.pallas_call(
        _compress_kernel,
        grid=(two, bsz, g),
        in_specs=[
            pl.BlockSpec((1, 1, 1, nr, width), lambda t, b, j: (t, b, j, 0, 0)),
            pl.BlockSpec((1, 2, width), lambda t, b, j: (t, 0, 0)),
            pl.BlockSpec((1, 2 * width, hid), lambda t, b, j: (t, 0, 0)),
            pl.BlockSpec((1, hid, dh), lambda t, b, j: (t, 0, 0)),
        ],
        out_specs=pl.BlockSpec((1, 1, 1, nr, dh), lambda t, b, j: (t, b, j, 0, 0)),
        out_shape=jax.ShapeDtypeStruct((two, bsz, g, nr, dh), F32),
        compiler_params=_params(("arbitrary", "arbitrary", "arbitrary")),
        name="compress_kv",
    )(x16, pe2, w1, w2)


def _cmp_select_kernel(qt_ref, kc_ref, vct_ref, oc_ref, bias_ref):
    hg = HEADS_PER_GROUP
    kc = kc_ref[0, 0]
    vct = vct_ref[0, 0]
    jj = lax.broadcasted_iota(jnp.int32, (N_SEL, N_CMP_PAD), 0) * SEL_BLOCK
    nn = lax.broadcasted_iota(jnp.int32, (N_SEL, N_CMP_PAD), 1) * CMP_STRIDE
    ov = jnp.minimum(nn + CMP_BLOCK, jj + SEL_BLOCK) - jnp.maximum(nn, jj)
    w_sel = (jnp.maximum(ov, 0).astype(F32) * (1.0 / CMP_BLOCK)).astype(BF16)
    nrow = lax.broadcasted_iota(jnp.int32, (N_CMP_PAD, TQ), 0)
    blk = lax.broadcasted_iota(jnp.int32, (N_SEL, TQ), 0)

    for qi in range(qt_ref.shape[2]):
        qt = qt_ref[0, 0, qi]
        pos = qi * TQ + lax.broadcasted_iota(jnp.int32, (1, TQ), 1)
        mask_c = nrow * CMP_STRIDE + (CMP_BLOCK - 1) <= pos

        def per_head_where(a, fill):
            return jnp.concatenate(
                [jnp.where(mask_c, a[:, h * TQ:(h + 1) * TQ], fill) for h in range(hg)], axis=1)

        s = per_head_where(_dot(kc, qt), NEG)
        e = jnp.exp2(s - jnp.max(s, axis=0, keepdims=True))
        p = per_head_where(e * (1.0 / jnp.sum(e, axis=0, keepdims=True)), 0.0)
        oc_ref[0, 0, qi] = _dot(vct, p.astype(BF16))
        psum = p[:, 0:TQ]
        for h in range(1, hg):
            psum = psum + p[:, h * TQ:(h + 1) * TQ]

        p_hi = psum.astype(BF16)
        p_lo = (psum - p_hi.astype(F32)).astype(BF16)
        imp = _dot(w_sel, p_hi) + _dot(w_sel, p_lo)
        cur = lax.shift_right_logical(pos, SEL_SHIFT)
        free = jnp.where(blk * SEL_BLOCK <= pos, imp, -FORCE_SCORE)
        score = jnp.where(blk == 0, FORCE_SCORE,
                          jnp.where(blk == cur, FORCE_SCORE, jnp.where(blk == cur - 1, FORCE_SCORE, free)))
        rank = jnp.zeros((N_SEL, TQ), F32)
        for k in range(N_SEL):
            sk = score[k:k + 1, :]
            tie = jnp.where(blk > k, 1.0, 0.0)
            rank = rank + jnp.where(sk > score, 1.0, jnp.where(sk == score, tie, 0.0))
        bias_ref[0, 0, qi] = jnp.where(rank < SEL_TOP_N, 0.0, NEG).astype(BF16)


def _cmp_select(qt, kc, vct):
    bsz, g, nq, dh, wide = qt.shape
    per_bg = lambda *shape: pl.BlockSpec((1, 1) + shape, lambda b, j: (b, j) + (0,) * len(shape))
    return pl.pallas_call(
        _cmp_select_kernel,
        grid=(bsz, g),
        in_specs=[per_bg(nq, dh, wide), per_bg(N_CMP_PAD, dh), per_bg(dh, N_CMP_PAD)],
        out_specs=[per_bg(nq, dh, wide), per_bg(nq, N_SEL, TQ)],
        out_shape=[jax.ShapeDtypeStruct((bsz, g, nq, dh, wide), F32),
                   jax.ShapeDtypeStruct((bsz, g, nq, N_SEL, TQ), BF16)],
        compiler_params=_params(("parallel", "arbitrary")),
        name="cmp_select",
    )(qt, kc, vct)


def _attn_kernel(qt_ref, oc_ref, bias_ref, ks_ref, kw_ref, vst_ref, vwt_ref, gl_ref, o_ref,
                 m_ref, acc_ref, sc_ref):
    qi = pl.program_id(2)
    last = pl.num_programs(2) - 1
    hg = HEADS_PER_GROUP
    dh = HEAD_DIM
    wide = hg * TQ
    qt = qt_ref[0, 0, 0]

    def per_head_where(mask, a, fill):
        return jnp.concatenate(
            [jnp.where(mask, a[:, h * TQ:(h + 1) * TQ], fill) for h in range(hg)], axis=1)

    def block_bias(value):
        return jnp.full((N_SEL, wide), value, F32).astype(BF16)

    zero_rows = jnp.zeros((K_ROW - dh - N_SEL, wide), BF16)

    def scores(k_ref, kt, bias):
        k_t = k_ref[0, 0, pl.ds(pl.multiple_of(kt * TQ, TQ), TQ), :]
        return _dot(k_t, jnp.concatenate([qt, bias, zero_rows], axis=0))

    rel = (lax.broadcasted_iota(jnp.int32, (TQ, TQ), 0)
           - lax.broadcasted_iota(jnp.int32, (TQ, TQ), 1))

    k1 = jnp.maximum(qi - 1, 0)
    k2 = jnp.maximum(qi - 2, 0)
    s0 = per_head_where(rel <= 0, scores(kw_ref, qi, block_bias(0.0)), NEG)
    s1 = scores(kw_ref, k1, block_bias(jnp.where(qi >= 1, 0.0, NEG)))
    s2 = per_head_where(rel > 0, scores(kw_ref, k2, block_bias(jnp.where(qi >= 2, 0.0, NEG))), NEG)
    m_w = jnp.maximum(jnp.max(s0, axis=0, keepdims=True),
                      jnp.maximum(jnp.max(s1, axis=0, keepdims=True), jnp.max(s2, axis=0, keepdims=True)))
    p_w = jnp.concatenate([jnp.exp2(s2 - m_w), jnp.exp2(s1 - m_w), jnp.exp2(s0 - m_w)], axis=0).astype(BF16)
    v_w = jnp.concatenate([vwt_ref[0, 0, k2], vwt_ref[0, 0, k1], vwt_ref[0, 0, qi]], axis=1)
    acc_ref[1] = _dot(v_w, p_w)

    bias_sel = jnp.concatenate([bias_ref[0, 0, 0]] * hg, axis=1)

    sd = per_head_where(rel <= 0, scores(ks_ref, qi, bias_sel), NEG)
    m_d = jnp.max(sd, axis=0, keepdims=True)
    m_ref[...] = m_d
    acc_ref[0] = _dot(vst_ref[0, 0, qi], jnp.exp2(sd - m_d).astype(BF16))

    def sel_scores(kt):
        pad_bias = block_bias(jnp.where(kt < qi, 0.0, NEG))
        return scores(ks_ref, jnp.minimum(kt, last), jnp.minimum(bias_sel, pad_bias))

    def sel_accumulate(sc, kt):
        m_old = m_ref[...]
        m_new = jnp.maximum(m_old, jnp.max(sc, axis=0, keepdims=True))
        m_ref[...] = m_new
        acc_ref[0] = (jnp.exp2(m_old - m_new) * acc_ref[0]
                      + _dot(vst_ref[0, 0, jnp.minimum(kt, last)], jnp.exp2(sc - m_new).astype(BF16)))

    sc_ref[0] = sel_scores(0)

    def sel_pair(pi, carry):
        kt = 2 * pi
        sc_ref[1] = sel_scores(kt + 1)
        sel_accumulate(sc_ref[0], kt)
        sc_ref[0] = sel_scores(kt + 2)
        sel_accumulate(sc_ref[1], kt + 1)
        return carry

    lax.fori_loop(0, (qi + 1) // 2, sel_pair, 0)

    gate = jax.nn.sigmoid(gl_ref[0, 0])
    o_cmp = oc_ref[0, 0, 0]
    acc_s = acc_ref[0]
    acc_w = acc_ref[1]
    o_sel = acc_s[0:dh, :] * (1.0 / acc_s[dh:dh + 1, :])
    o_win = acc_w[0:dh, :] * (1.0 / acc_w[dh:dh + 1, :])
    outs = []
    for h in range(hg):
        lanes = slice(h * TQ, (h + 1) * TQ)
        g0 = gate[3 * h:3 * h + 1, :]
        g1 = gate[3 * h + 1:3 * h + 2, :]
        g2 = gate[3 * h + 2:3 * h + 3, :]
        outs.append(g0 * o_cmp[:, lanes] + g1 * o_sel[:, lanes] + g2 * o_win[:, lanes])
    o_ref[0] = jnp.concatenate(outs, axis=0).T


def _attention(qt, o_cmp, bias, ks, kw, vst, vwt, glt):
    bsz, g, nq = qt.shape[:3]
    s = nq * TQ
    width = HEADS_PER_GROUP * HEAD_DIM
    wide = HEADS_PER_GROUP * TQ
    per_bg = lambda *shape: pl.BlockSpec((1, 1) + shape, lambda b, j, i: (b, j) + (0,) * len(shape))
    per_tile = lambda *shape: pl.BlockSpec((1, 1, 1) + shape, lambda b, j, i: (b, j, i) + (0,) * len(shape))
    return pl.pallas_call(
        _attn_kernel,
        grid=(bsz, g, nq),
        in_specs=[
            per_tile(HEAD_DIM, wide),
            per_tile(HEAD_DIM, wide),
            per_tile(N_SEL, TQ),
            per_bg(s, K_ROW),
            per_bg(s, K_ROW),
            per_bg(nq, V_ROWS, TQ),
            per_bg(nq, V_ROWS, TQ),
            pl.BlockSpec((1, 1, HEADS_PER_GROUP * N_BRANCH, TQ), lambda b, j, i: (b, j, 0, i)),
        ],
        out_specs=pl.BlockSpec((1, TQ, width), lambda b, j, i: (b, i, j)),
        out_shape=jax.ShapeDtypeStruct((bsz, s, D_ATT), F32),
        scratch_shapes=[
            pltpu.VMEM((1, wide), F32),
            pltpu.VMEM((2, V_ROWS, wide), F32),
            pltpu.VMEM((2, TQ, wide), F32),
        ],
        compiler_params=_params(("parallel", "parallel", "arbitrary")),
        name="nsa_attention",
    )(qt, o_cmp, bias, ks, kw, vst, vwt, glt)


def _rglru_kernel(x_ref, y_ref, cw_ref, cb_ref, w_ref, b_ref, lam_ref, o_ref,
                  xpad_ref, a_ref, bt_ref):
    s = x_ref.shape[1]
    c = RNN_BLOCK_DIM
    pad = SUBLANES
    xpad_ref[0:pad, :] = jnp.zeros((pad, c), F32)
    xpad_ref[pad:pad + s, :] = x_ref[0]
    cw = cw_ref[...]
    w = w_ref[0]
    bias = b_ref[0]
    lam = lam_ref[...]
    neg_softplus = -(jnp.maximum(-lam, 0.0) + jnp.log1p(jnp.exp(-jnp.abs(lam))))
    sub = lax.broadcasted_iota(jnp.int32, (TT, c), 0) & (SUBLANES - 1)

    for ci in range(s // TT):
        t0 = ci * TT
        xc = cb_ref[...] + sum(
            xpad_ref[t0 + pad - (CONV_WIDTH - 1) + k:t0 + pad - (CONV_WIDTH - 1) + k + TT, :] * cw[k:k + 1, :]
            for k in range(CONV_WIDTH))
        gates = jax.nn.sigmoid(_dot(xc.astype(BF16), w) + bias)
        r = gates[:, 0:c]
        i = gates[:, c:2 * c]
        log_a = LRU_C * r * neg_softplus
        a = jnp.exp(log_a)
        var = -jnp.tanh(log_a) * (a * a + 1.0)
        bt = jnp.where(var > 0.0, var * lax.rsqrt(var), 0.0) * (i * xc)
        for d in (1, 2, 4):
            keep = sub >= d
            a_prev = jnp.where(keep, pltpu.roll(a, d, axis=0), 1.0)
            b_prev = jnp.where(keep, pltpu.roll(bt, d, axis=0), 0.0)
            bt = bt + a * b_prev
            a = a * a_prev
        a_ref[t0:t0 + TT, :] = a
        bt_ref[t0:t0 + TT, :] = bt

    def group(gi, h):
        r0 = pl.multiple_of(gi * SUBLANES, SUBLANES)
        hg = bt_ref[pl.ds(r0, SUBLANES), :] + a_ref[pl.ds(r0, SUBLANES), :] * h
        bt_ref[pl.ds(r0, SUBLANES), :] = hg
        return hg[SUBLANES - 1:SUBLANES, :]

    lax.fori_loop(0, s // SUBLANES, group, jnp.zeros((1, c), F32), unroll=8)

    for ci in range(s // TT):
        t0 = ci * TT
        o_ref[0, t0:t0 + TT, :] = jax.nn.gelu(y_ref[0, t0:t0 + TT, :]) * bt_ref[t0:t0 + TT, :]


def _rglru(r_in, conv_w, conv_b, w_cat, b_cat, lam):
    bsz, s, _ = r_in.shape
    c = RNN_BLOCK_DIM
    nb = RNN_BLOCKS
    return pl.pallas_call(
        _rglru_kernel,
        grid=(bsz, nb),
        in_specs=[
            pl.BlockSpec((1, s, c), lambda b, j: (b, 0, j)),
            pl.BlockSpec((1, s, c), lambda b, j: (b, 0, nb + j)),
            pl.BlockSpec((CONV_WIDTH, c), lambda b, j: (0, j)),
            pl.BlockSpec((1, c), lambda b, j: (0, j)),
            pl.BlockSpec((1, c, 2 * c), lambda b, j: (j, 0, 0)),
            pl.BlockSpec((1, 1, 2 * c), lambda b, j: (j, 0, 0)),
            pl.BlockSpec((1, c), lambda b, j: (0, j)),
        ],
        out_specs=pl.BlockSpec((1, s, c), lambda b, j: (b, 0, j)),
        out_shape=jax.ShapeDtypeStruct((bsz, s, D_RNN), F32),
        scratch_shapes=[
            pltpu.VMEM((s + SUBLANES, c), F32),
            pltpu.VMEM((s, c), F32),
            pltpu.VMEM((s, c), F32),
        ],
        compiler_params=_params(("parallel", "arbitrary")),
        name="rg_lru",
    )(r_in, r_in, conv_w, conv_b, w_cat, b_cat, lam)


def _outproj_kernel(oa_ref, or_ref, ga_ref, gr_ref, w_ref, x_ref, gt_ref, gp_ref, o_ref):
    a = _rms(oa_ref[0], ga_ref[...]).astype(BF16)
    r = _rms(or_ref[0], gr_ref[...]).astype(BF16)
    mix = _dot(a, w_ref[0:D_ATT, :]) + _dot(r, w_ref[D_ATT:D_ATT + D_RNN, :])
    o_ref[0] = x_ref[0] + gt_ref[0, 0] * _rms(mix, gp_ref[...])


def _outproj(o_att, o_rnn, g_att, g_rnn, w_out, x, mod6, g_post, tm=256):
    bsz, s, d = x.shape
    row = lambda width: pl.BlockSpec((1, tm, width), lambda b, i: (b, i, 0))
    vec = lambda width: pl.BlockSpec((1, width), lambda b, i: (0, 0))
    return pl.pallas_call(
        _outproj_kernel,
        grid=(bsz, s // tm),
        in_specs=[
            row(D_ATT), row(D_RNN), vec(D_ATT), vec(D_RNN),
            pl.BlockSpec((D_ATT + D_RNN, d), lambda b, i: (0, 0), pipeline_mode=pl.Buffered(1)),
            row(d),
            pl.BlockSpec((1, 1, 1, d), lambda b, i: (2, b, 0, 0)),
            vec(d),
        ],
        out_specs=row(d),
        out_shape=jax.ShapeDtypeStruct((bsz, s, d), F32),
        compiler_params=_params(("parallel", "arbitrary")),
        name="out_proj",
    )(o_att, o_rnn, g_att.reshape(1, -1), g_rnn.reshape(1, -1), w_out, x, mod6, g_post.reshape(1, d))


def _mlp_kernel(x_ref, g_ref, sc_ref, sh_ref, w1_ref, w2_ref, gt_ref, gp_ref, o_ref, h_ref, acc_ref):
    j = pl.program_id(2)

    @pl.when(j == 0)
    def _():
        h = _rms(x_ref[0], g_ref[...]) * (1.0 + sc_ref[0, 0]) + sh_ref[0, 0]
        h_ref[...] = h.astype(BF16)

    u = jnp.maximum(_dot(h_ref[...], w1_ref[...]), 0.0)
    part = _dot((u * u).astype(BF16), w2_ref[...])

    @pl.when(j == 0)
    def _():
        acc_ref[...] = part

    @pl.when(j > 0)
    def _():
        acc_ref[...] += part

    @pl.when(j == pl.num_programs(2) - 1)
    def _():
        o_ref[0] = x_ref[0] + gt_ref[0, 0] * _rms(acc_ref[...], gp_ref[...])


def _mlp(x, g_pre, mod6, w1, w2, g_post, tm=512, tf=1024):
    bsz, s, d = x.shape
    dff = w1.shape[1]
    row = pl.BlockSpec((1, tm, d), lambda b, i, j: (b, i, 0))
    vec = pl.BlockSpec((1, d), lambda b, i, j: (0, 0))
    modk = lambda k: pl.BlockSpec((1, 1, 1, d), lambda b, i, j: (k, b, 0, 0))
    return pl.pallas_call(
        _mlp_kernel,
        grid=(bsz, s // tm, dff // tf),
        in_specs=[
            row, vec, modk(4), modk(3),
            pl.BlockSpec((d, tf), lambda b, i, j: (0, j)),
            pl.BlockSpec((tf, d), lambda b, i, j: (j, 0)),
            modk(5), vec,
        ],
        out_specs=row,
        out_shape=jax.ShapeDtypeStruct((bsz, s, d), F32),
        scratch_shapes=[pltpu.VMEM((tm, d), BF16), pltpu.VMEM((tm, d), F32)],
        compiler_params=_params(("parallel", "parallel", "arbitrary")),
        name="mlp",
    )(x, g_pre.reshape(1, d), mod6, mod6, w1, w2, mod6, g_post.reshape(1, d))


def _layer(x, c, w_ada, b_ada, g_pre_mix, g_post_mix, g_pre_mlp, g_post_mlp, w_in,
           cmp_w1_k, cmp_w2_k, cmp_pe_k, cmp_w1_v, cmp_w2_v, cmp_pe_v,
           conv_w, conv_b, w_rg_a, b_rg_a, w_rg_x, b_rg_x, lru_lambda,
           g_grp_att, g_grp_rnn, w_out, w_ff1, w_ff2):
    bsz, s, d = x.shape
    g = N_KV_GROUPS
    dh = HEAD_DIM
    nq = s // TQ

    mod = _ada(c, w_ada, b_ada)
    mod6 = mod.reshape(bsz, 6, 1, d).transpose(1, 0, 2, 3)

    n_gl = N_BRANCH * N_Q_HEADS
    c_gl0 = D_ATT + 6 * D_KV
    kv = lambda n: w_in[:, D_ATT + n * D_KV:D_ATT + (n + 1) * D_KV]
    w_perm = jnp.concatenate(
        [w_in[:, :D_ATT], kv(0), kv(1), kv(2), kv(4), kv(3), kv(5),
         w_in[:, c_gl0 + n_gl:], w_in[:, c_gl0:c_gl0 + n_gl],
         jnp.zeros((d, GL_PAD - n_gl), w_in.dtype)], axis=1).astype(BF16)
    qt, kvc, ks, kw, vst, vwt, r_in, glt = _inproj(x, g_pre_mix, mod6, w_perm)

    x16 = kvc.reshape(bsz, s, 2, g, dh).transpose(2, 0, 3, 1, 4).reshape(
        2, bsz, g, s // CMP_STRIDE, CMP_STRIDE * dh)
    pe2 = jnp.stack([cmp_pe_k, cmp_pe_v]).reshape(2, 2, CMP_STRIDE * dh)
    w1c = jnp.stack([cmp_w1_k, cmp_w1_v]).astype(BF16)
    w2c = jnp.stack([cmp_w2_k, cmp_w2_v]).astype(BF16)
    kv_cmp = _compress(x16, pe2, w1c, w2c)
    kcm = kv_cmp[0].astype(BF16)
    vct = kv_cmp[1].transpose(0, 1, 3, 2).astype(BF16)

    o_cmp, sel_bias = _cmp_select(qt, kcm, vct)
    o_att = _attention(qt, o_cmp, sel_bias, ks, kw, vst, vwt, glt)

    w_cat = jnp.concatenate([w_rg_a, w_rg_x], axis=-1).astype(BF16)
    b_cat = jnp.concatenate([b_rg_a.reshape(RNN_BLOCKS, 1, RNN_BLOCK_DIM),
                             b_rg_x.reshape(RNN_BLOCKS, 1, RNN_BLOCK_DIM)], axis=-1)
    o_rnn = _rglru(r_in, conv_w, conv_b.reshape(1, -1), w_cat, b_cat, lru_lambda.reshape(1, -1))

    x1 = _outproj(o_att, o_rnn, g_grp_att, g_grp_rnn, w_out.astype(BF16), x, mod6, g_post_mix)
    return _mlp(x1, g_pre_mlp, mod6, w_ff1.astype(BF16), w_ff2.astype(BF16), g_post_mlp)


def kernel(x, c, w_ada, b_ada, g_pre_mix, g_post_mix, g_pre_mlp, g_post_mlp, w_in, cmp_w1_k, cmp_w2_k, cmp_pe_k, cmp_w1_v, cmp_w2_v, cmp_pe_v, conv_w, conv_b, w_rg_a, b_rg_a, w_rg_x, b_rg_x, lru_lambda, g_grp_att, g_grp_rnn, w_out, w_ff1, w_ff2):
    depth = w_ada.shape[0]
    for l in range(depth):
        x = _layer(x, c, w_ada[l], b_ada[l], g_pre_mix[l], g_post_mix[l], g_pre_mlp[l], g_post_mlp[l],
                   w_in[l], cmp_w1_k[l], cmp_w2_k[l], cmp_pe_k[l], cmp_w1_v[l], cmp_w2_v[l], cmp_pe_v[l],
                   conv_w[l], conv_b[l], w_rg_a[l], b_rg_a[l], w_rg_x[l], b_rg_x[l], lru_lambda[l],
                   g_grp_att[l], g_grp_rnn[l], w_out[l], w_ff1[l], w_ff2[l])
    return x
```

```python
import functools

import jax
import jax.numpy as jnp
from jax import lax
from jax.experimental import pallas as pl
from jax.experimental.pallas import tpu as pltpu

F32 = jnp.float32
BF16 = jnp.bfloat16

D_MODEL = 2048
D_ATT = 1024
D_RNN = 1024
N_Q_HEADS = 16
N_KV_GROUPS = 4
HEADS_PER_GROUP = 4
HEAD_DIM = 64
D_KV = 256
CMP_BLOCK = 32
CMP_STRIDE = 16
SEL_BLOCK = 64
SEL_SHIFT = 6
SEL_TOP_N = 8
WINDOW = 512
N_BRANCH = 3
RNN_BLOCKS = 8
RNN_BLOCK_DIM = 128
CONV_WIDTH = 4
LRU_C = 8.0
D_FF = 4 * D_MODEL
EPS = 1e-6
NEG = -1e30
FORCE_SCORE = 1e9
LOG2_E = 1.4426950408889634

LANES = 128
SUBLANES = 8
VMEM_LIMIT = 56 * 1024 * 1024

TQ = 256
N_CMP_PAD = 128
N_SEL = 32
GL_PAD = 128
TT = 256


def _params(sem):
    return pltpu.CompilerParams(dimension_semantics=sem, vmem_limit_bytes=VMEM_LIMIT)


def _dot(a, b):
    return jnp.dot(a, b, preferred_element_type=F32)


def _rms(x, g):
    return x * lax.rsqrt(jnp.mean(x * x, axis=-1, keepdims=True) + EPS) * g


def _ada_kernel(c_ref, w_ref, b_ref, o_ref):
    c = c_ref[...]
    ca = (c * jax.nn.sigmoid(c)).astype(BF16)
    o_ref[...] = _dot(ca, w_ref[...].astype(BF16)) + b_ref[...]


def _ada(c, w, b):
    bsz, d = c.shape
    n = w.shape[1]
    tn = 1024
    return pl.pallas_call(
        _ada_kernel,
        grid=(n // tn,),
        in_specs=[
            pl.BlockSpec((bsz, d), lambda j: (0, 0)),
            pl.BlockSpec((d, tn), lambda j: (0, j)),
            pl.BlockSpec((1, tn), lambda j: (0, j)),
        ],
        out_specs=pl.BlockSpec((bsz, tn), lambda j: (0, j)),
        out_shape=jax.ShapeDtypeStruct((bsz, n), F32),
        compiler_params=_params(("arbitrary",)),
        name="ada_mod",
    )(c, w, b.reshape(1, n))


C_Q = (0, D_ATT)
C_CMP = (C_Q[1], C_Q[1] + 2 * D_KV)
C_K = (C_CMP[1], C_CMP[1] + 2 * D_KV)
C_V = (C_K[1], C_K[1] + 2 * D_KV)
C_R = (C_V[1], C_V[1] + 2 * D_RNN)
C_GL = (C_R[1], C_R[1] + GL_PAD)
D_IN_PAD = C_GL[1]
K_ROW = LANES
V_ROWS = HEAD_DIM + 16
N_GATE = HEADS_PER_GROUP * N_BRANCH


def _inproj_kernel(x_ref, g_ref, sc_ref, sh_ref, w_ref,
                   qt_ref, cmp_ref, ks_ref, kw_ref, vst_ref, vwt_ref, r_ref, glt_ref):
    tm = x_ref.shape[1]
    ng, hg, dh = N_KV_GROUPS, HEADS_PER_GROUP, HEAD_DIM
    x = x_ref[0]
    h = _rms(x, g_ref[...]) * (1.0 + sc_ref[0, 0]) + sh_ref[0, 0]
    hb = h.astype(BF16)
    tiles = [slice(u * TQ, (u + 1) * TQ) for u in range(tm // TQ)]

    q_t = (_dot(hb, w_ref[:, C_Q[0]:C_Q[1]]) * (dh ** -0.5 * LOG2_E)).T.astype(BF16)
    for j in range(ng):
        for u, cols in enumerate(tiles):
            qt_ref[0, j, u] = jnp.concatenate(
                [q_t[(hg * j + hh) * dh:(hg * j + hh + 1) * dh, cols] for hh in range(hg)], axis=1)

    kv_cmp = _dot(hb, w_ref[:, C_CMP[0]:C_CMP[1]])
    for c in range(2 * D_KV // LANES):
        cmp_ref[0, c] = kv_cmp[:, c * LANES:(c + 1) * LANES]

    keys = _dot(hb, w_ref[:, C_K[0]:C_K[1]])
    row_pos = pl.program_id(1) * tm + lax.broadcasted_iota(jnp.int32, (tm, K_ROW - dh), 0)
    lane = lax.broadcasted_iota(jnp.int32, (tm, K_ROW - dh), 1)
    onehot = jnp.where(lax.shift_right_logical(row_pos, SEL_SHIFT) == lane, 1.0, 0.0)
    for j in range(ng):
        ks_ref[0, j] = jnp.concatenate([keys[:, j * dh:(j + 1) * dh], onehot], axis=1).astype(BF16)
        kw_ref[0, j] = jnp.concatenate(
            [keys[:, D_KV + j * dh:D_KV + (j + 1) * dh], onehot], axis=1).astype(BF16)

    v_t = _dot(hb, w_ref[:, C_V[0]:C_V[1]]).T.astype(BF16)
    ones = jnp.ones((V_ROWS - dh, TQ), BF16)
    for j in range(ng):
        for u, cols in enumerate(tiles):
            vst_ref[0, j, u] = jnp.concatenate([v_t[j * dh:(j + 1) * dh, cols], ones], axis=0)
            vwt_ref[0, j, u] = jnp.concatenate([v_t[D_KV + j * dh:D_KV + (j + 1) * dh, cols], ones], axis=0)

    half = (C_R[0] + C_R[1]) // 2
    r_ref[0, :, 0:D_RNN] = _dot(hb, w_ref[:, C_R[0]:half])
    r_ref[0, :, D_RNN:2 * D_RNN] = _dot(hb, w_ref[:, half:C_R[1]])

    gl_t = _dot(hb, w_ref[:, C_GL[0]:C_GL[1]]).T
    for j in range(ng):
        glt_ref[0, j] = gl_t[N_GATE * j:N_GATE * (j + 1), :]


def _inproj(x, g, mod6, w_perm, tm=512):
    bsz, s, d = x.shape
    ng = N_KV_GROUPS
    nt = tm // TQ
    row = lambda width: pl.BlockSpec((1, tm, width), lambda b, i: (b, i, 0))
    grp_rows = pl.BlockSpec((1, ng, tm, K_ROW), lambda b, i: (b, 0, i, 0))
    grp_tiles = lambda r, c: pl.BlockSpec((1, ng, nt, r, c), lambda b, i: (b, 0, i, 0, 0))
    sds = jax.ShapeDtypeStruct
    return pl.pallas_call(
        _inproj_kernel,
        grid=(bsz, s // tm),
        in_specs=[
            row(d),
            pl.BlockSpec((1, d), lambda b, i: (0, 0)),
            pl.BlockSpec((1, 1, 1, d), lambda b, i: (1, b, 0, 0)),
            pl.BlockSpec((1, 1, 1, d), lambda b, i: (0, b, 0, 0)),
            pl.BlockSpec((d, D_IN_PAD), lambda b, i: (0, 0), pipeline_mode=pl.Buffered(1)),
        ],
        out_specs=[
            grp_tiles(HEAD_DIM, HEADS_PER_GROUP * TQ),
            pl.BlockSpec((1, 2 * D_KV // LANES, tm, LANES), lambda b, i: (b, 0, i, 0)),
            grp_rows, grp_rows,
            grp_tiles(V_ROWS, TQ), grp_tiles(V_ROWS, TQ), row(2 * D_RNN),
            pl.BlockSpec((1, ng, N_GATE, tm), lambda b, i: (b, 0, 0, i)),
        ],
        out_shape=[
            sds((bsz, ng, s // TQ, HEAD_DIM, HEADS_PER_GROUP * TQ), BF16),
            sds((bsz, 2 * D_KV // LANES, s, LANES), F32),
            sds((bsz, ng, s, K_ROW), BF16), sds((bsz, ng, s, K_ROW), BF16),
            sds((bsz, ng, s // TQ, V_ROWS, TQ), BF16), sds((bsz, ng, s // TQ, V_ROWS, TQ), BF16),
            sds((bsz, s, 2 * D_RNN), F32),
            sds((bsz, ng, N_GATE, s), F32),
        ],
        compiler_params=_params(("parallel", "arbitrary")),
        name="in_proj",
    )(x, g.reshape(1, d), mod6, mod6, w_perm)


def _compress_kernel(x_ref, pe_ref, w1_ref, w2_ref, o_ref):
    bsz = x_ref.shape[0]
    dh = HEAD_DIM
    n_rows = x_ref.shape[2] // CMP_STRIDE
    rows = bsz * n_rows
    first = jnp.zeros((rows, w1_ref.shape[3]), F32)
    second = jnp.zeros((rows, w1_ref.shape[3]), F32)
    for l in range(CMP_STRIDE):
        x_l = x_ref[:, 0, pl.ds(l, n_rows, stride=CMP_STRIDE), :].reshape(rows, LANES)
        first = first + _dot((x_l + pe_ref[0, l:l + 1, :]).astype(BF16), w1_ref[0, l])
        second = second + _dot(
            (x_l + pe_ref[0, CMP_STRIDE + l:CMP_STRIDE + l + 1, :]).astype(BF16), w1_ref[0, CMP_STRIDE + l])
    hid = first + pltpu.roll(second, rows - 1, axis=0)
    out = _dot(jax.nn.gelu(hid).astype(BF16), w2_ref[0]).astype(BF16)
    for b in range(bsz):
        for j in range(2):
            o_ref[0, b, j] = out[b * n_rows:(b + 1) * n_rows, j * dh:(j + 1) * dh]


def _compress(kvc4, pe, w1, w2):
    bsz, n_tiles, s, _ = kvc4.shape
    n_rows = s // CMP_STRIDE
    per_kind = lambda a: pl.BlockSpec((1,) + a.shape[1:], lambda t, p: (t,) + (0,) * (a.ndim - 1))
    return pl.pallas_call(
        _compress_kernel,
        grid=(2, n_tiles // 2),
        in_specs=[pl.BlockSpec((bsz, 1, s, LANES), lambda t, p: (0, 2 * t + p, 0, 0)),
                  per_kind(pe), per_kind(w1), per_kind(w2)],
        out_specs=pl.BlockSpec((1, bsz, 2, n_rows, HEAD_DIM), lambda t, p: (t, 0, p, 0, 0)),
        out_shape=jax.ShapeDtypeStruct((2, bsz, N_KV_GROUPS, n_rows, HEAD_DIM), BF16),
        compiler_params=_params(("arbitrary", "arbitrary")),
        name="compress_kv",
    )(kvc4, pe, w1, w2)


def _cmp_select_kernel(qt_ref, kc_ref, vc_ref, oc_ref, bias_ref):
    hg = HEADS_PER_GROUP
    kc = kc_ref[0, 0, 0]
    vct = vc_ref[0, 0, 0].astype(F32).T.astype(BF16)
    jj = lax.broadcasted_iota(jnp.int32, (N_SEL, N_CMP_PAD), 0) * SEL_BLOCK
    nn = lax.broadcasted_iota(jnp.int32, (N_SEL, N_CMP_PAD), 1) * CMP_STRIDE
    ov = jnp.minimum(nn + CMP_BLOCK, jj + SEL_BLOCK) - jnp.maximum(nn, jj)
    w_sel = (jnp.maximum(ov, 0).astype(F32) * (1.0 / CMP_BLOCK)).astype(BF16)
    nrow = lax.broadcasted_iota(jnp.int32, (N_CMP_PAD, TQ), 0)
    blk = lax.broadcasted_iota(jnp.int32, (N_SEL, TQ), 0)

    for qi in range(qt_ref.shape[2]):
        qt = qt_ref[0, 0, qi]
        pos = qi * TQ + lax.broadcasted_iota(jnp.int32, (1, TQ), 1)
        mask_c = nrow * CMP_STRIDE + (CMP_BLOCK - 1) <= pos

        def per_head_where(a, fill):
            return jnp.concatenate(
                [jnp.where(mask_c, a[:, h * TQ:(h + 1) * TQ], fill) for h in range(hg)], axis=1)

        s = per_head_where(_dot(kc, qt), NEG)
        e = jnp.exp2(s - jnp.max(s, axis=0, keepdims=True))
        p = per_head_where(e * (1.0 / jnp.sum(e, axis=0, keepdims=True)), 0.0)
        oc_ref[0, 0, qi] = _dot(vct, p.astype(BF16))
        psum = p[:, 0:TQ]
        for h in range(1, hg):
            psum = psum + p[:, h * TQ:(h + 1) * TQ]

        p_hi = psum.astype(BF16)
        p_lo = (psum - p_hi.astype(F32)).astype(BF16)
        imp = _dot(w_sel, p_hi) + _dot(w_sel, p_lo)
        cur = lax.shift_right_logical(pos, SEL_SHIFT)
        free = jnp.where(blk * SEL_BLOCK <= pos, imp, -FORCE_SCORE)
        score = jnp.where(blk == 0, FORCE_SCORE,
                          jnp.where(blk == cur, FORCE_SCORE, jnp.where(blk == cur - 1, FORCE_SCORE, free)))
        rank = jnp.zeros((N_SEL, TQ), F32)
        for k in range(N_SEL):
            sk = score[k:k + 1, :]
            tie = jnp.where(blk > k, 1.0, 0.0)
            rank = rank + jnp.where(sk > score, 1.0, jnp.where(sk == score, tie, 0.0))
        bias_ref[0, 0, qi] = jnp.where(rank < SEL_TOP_N, 0.0, NEG).astype(BF16)


def _cmp_select(qt, kv_cmp):
    bsz, g, nq, dh, wide = qt.shape
    per_bg = lambda *shape: pl.BlockSpec((1, 1) + shape, lambda b, j: (b, j) + (0,) * len(shape))
    kind = lambda t: pl.BlockSpec((1, 1, 1, N_CMP_PAD, dh), lambda b, j: (t, b, j, 0, 0))
    return pl.pallas_call(
        _cmp_select_kernel,
        grid=(bsz, g),
        in_specs=[per_bg(nq, dh, wide), kind(0), kind(1)],
        out_specs=[per_bg(nq, dh, wide), per_bg(nq, N_SEL, TQ)],
        out_shape=[jax.ShapeDtypeStruct((bsz, g, nq, dh, wide), F32),
                   jax.ShapeDtypeStruct((bsz, g, nq, N_SEL, TQ), BF16)],
        compiler_params=_params(("parallel", "arbitrary")),
        name="cmp_select",
    )(qt, kv_cmp, kv_cmp)


def _attn_kernel(qt_ref, oc_ref, bias_ref, ks_ref, kw_ref, vst_ref, vwt_ref, gl_ref, o_ref,
                 m_ref, acc_ref, sc_ref):
    qi = pl.program_id(2)
    last = pl.num_programs(2) - 1
    hg = HEADS_PER_GROUP
    dh = HEAD_DIM
    wide = hg * TQ
    qt = qt_ref[0, 0, 0]

    def per_head_where(mask, a, fill):
        return jnp.concatenate(
            [jnp.where(mask, a[:, h * TQ:(h + 1) * TQ], fill) for h in range(hg)], axis=1)

    def block_bias(value):
        return jnp.full((N_SEL, wide), value, F32).astype(BF16)

    zero_rows = jnp.zeros((K_ROW - dh - N_SEL, wide), BF16)

    def scores(k_ref, kt, bias):
        k_t = k_ref[0, 0, pl.ds(pl.multiple_of(kt * TQ, TQ), TQ), :]
        return _dot(k_t, jnp.concatenate([qt, bias, zero_rows], axis=0))

    rel = (lax.broadcasted_iota(jnp.int32, (TQ, TQ), 0)
           - lax.broadcasted_iota(jnp.int32, (TQ, TQ), 1))

    k1 = jnp.maximum(qi - 1, 0)
    k2 = jnp.maximum(qi - 2, 0)
    s0 = per_head_where(rel <= 0, scores(kw_ref, qi, block_bias(0.0)), NEG)
    s1 = scores(kw_ref, k1, block_bias(jnp.where(qi >= 1, 0.0, NEG)))
    s2 = per_head_where(rel > 0, scores(kw_ref, k2, block_bias(jnp.where(qi >= 2, 0.0, NEG))), NEG)
    m_w = jnp.maximum(jnp.max(s0, axis=0, keepdims=True),
                      jnp.maximum(jnp.max(s1, axis=0, keepdims=True), jnp.max(s2, axis=0, keepdims=True)))
    p_w = jnp.concatenate([jnp.exp2(s2 - m_w), jnp.exp2(s1 - m_w), jnp.exp2(s0 - m_w)], axis=0).astype(BF16)
    v_w = jnp.concatenate([vwt_ref[0, 0, k2], vwt_ref[0, 0, k1], vwt_ref[0, 0, qi]], axis=1)
    acc_ref[1] = _dot(v_w, p_w)

    bias_sel = jnp.concatenate([bias_ref[0, 0, 0]] * hg, axis=1)

    sd = per_head_where(rel <= 0, scores(ks_ref, qi, bias_sel), NEG)
    m_d = jnp.max(sd, axis=0, keepdims=True)
    m_ref[...] = m_d
    acc_ref[0] = _dot(vst_ref[0, 0, qi], jnp.exp2(sd - m_d).astype(BF16))

    def sel_scores(kt):
        pad_bias = block_bias(jnp.where(kt < qi, 0.0, NEG))
        return scores(ks_ref, jnp.minimum(kt, last), jnp.minimum(bias_sel, pad_bias))

    def sel_accumulate(sc, kt):
        m_old = m_ref[...]
        m_new = jnp.maximum(m_old, jnp.max(sc, axis=0, keepdims=True))
        m_ref[...] = m_new
        acc_ref[0] = (jnp.exp2(m_old - m_new) * acc_ref[0]
                      + _dot(vst_ref[0, 0, jnp.minimum(kt, last)], jnp.exp2(sc - m_new).astype(BF16)))

    sc_ref[0] = sel_scores(0)

    def sel_pair(pi, carry):
        kt = 2 * pi
        sc_ref[1] = sel_scores(kt + 1)
        sel_accumulate(sc_ref[0], kt)
        sc_ref[0] = sel_scores(kt + 2)
        sel_accumulate(sc_ref[1], kt + 1)
        return carry

    lax.fori_loop(0, (qi + 1) // 2, sel_pair, 0)

    gate = jax.nn.sigmoid(gl_ref[0, 0])
    o_cmp = oc_ref[0, 0, 0]
    acc_s = acc_ref[0]
    acc_w = acc_ref[1]
    o_sel = acc_s[0:dh, :] * (1.0 / acc_s[dh:dh + 1, :])
    o_win = acc_w[0:dh, :] * (1.0 / acc_w[dh:dh + 1, :])
    outs = []
    for h in range(hg):
        lanes = slice(h * TQ, (h + 1) * TQ)
        g0 = gate[3 * h:3 * h + 1, :]
        g1 = gate[3 * h + 1:3 * h + 2, :]
        g2 = gate[3 * h + 2:3 * h + 3, :]
        outs.append(g0 * o_cmp[:, lanes] + g1 * o_sel[:, lanes] + g2 * o_win[:, lanes])
    o_ref[0] = jnp.concatenate(outs, axis=0).T


def _attention(qt, o_cmp, bias, ks, kw, vst, vwt, glt):
    bsz, g, nq = qt.shape[:3]
    s = nq * TQ
    width = HEADS_PER_GROUP * HEAD_DIM
    wide = HEADS_PER_GROUP * TQ
    per_bg = lambda *shape: pl.BlockSpec((1, 1) + shape, lambda b, j, i: (b, j) + (0,) * len(shape))
    per_tile = lambda *shape: pl.BlockSpec((1, 1, 1) + shape, lambda b, j, i: (b, j, i) + (0,) * len(shape))
    return pl.pallas_call(
        _attn_kernel,
        grid=(bsz, g, nq),
        in_specs=[
            per_tile(HEAD_DIM, wide),
            per_tile(HEAD_DIM, wide),
            per_tile(N_SEL, TQ),
            per_bg(s, K_ROW),
            per_bg(s, K_ROW),
            per_bg(nq, V_ROWS, TQ),
            per_bg(nq, V_ROWS, TQ),
            pl.BlockSpec((1, 1, HEADS_PER_GROUP * N_BRANCH, TQ), lambda b, j, i: (b, j, 0, i)),
        ],
        out_specs=pl.BlockSpec((1, TQ, width), lambda b, j, i: (b, i, j)),
        out_shape=jax.ShapeDtypeStruct((bsz, s, D_ATT), F32),
        scratch_shapes=[
            pltpu.VMEM((1, wide), F32),
            pltpu.VMEM((2, V_ROWS, wide), F32),
            pltpu.VMEM((2, TQ, wide), F32),
        ],
        compiler_params=_params(("parallel", "parallel", "arbitrary")),
        name="nsa_attention",
    )(qt, o_cmp, bias, ks, kw, vst, vwt, glt)


def _rglru_kernel(x_ref, y_ref, cw_ref, cb_ref, w_ref, b_ref, lam_ref, o_ref,
                  xpad_ref, a_ref, bt_ref):
    s = x_ref.shape[1]
    c = RNN_BLOCK_DIM
    pad = SUBLANES
    xpad_ref[0:pad, :] = jnp.zeros((pad, c), F32)
    xpad_ref[pad:pad + s, :] = x_ref[0]
    cw = cw_ref[...]
    w = w_ref[0]
    bias = b_ref[0]
    lam = lam_ref[...]
    neg_softplus = -(jnp.maximum(-lam, 0.0) + jnp.log1p(jnp.exp(-jnp.abs(lam))))
    sub = lax.broadcasted_iota(jnp.int32, (TT, c), 0) & (SUBLANES - 1)

    for ci in range(s // TT):
        t0 = ci * TT
        xc = cb_ref[...] + sum(
            xpad_ref[t0 + pad - (CONV_WIDTH - 1) + k:t0 + pad - (CONV_WIDTH - 1) + k + TT, :] * cw[k:k + 1, :]
            for k in range(CONV_WIDTH))
        gates = jax.nn.sigmoid(_dot(xc.astype(BF16), w) + bias)
        r = gates[:, 0:c]
        i = gates[:, c:2 * c]
        log_a = LRU_C * r * neg_softplus
        a = jnp.exp(log_a)
        var = -jnp.tanh(log_a) * (a * a + 1.0)
        bt = jnp.where(var > 0.0, var * lax.rsqrt(var), 0.0) * (i * xc)
        for d in (1, 2, 4):
            keep = sub >= d
            a_prev = jnp.where(keep, pltpu.roll(a, d, axis=0), 1.0)
            b_prev = jnp.where(keep, pltpu.roll(bt, d, axis=0), 0.0)
            bt = bt + a * b_prev
            a = a * a_prev
        a_ref[t0:t0 + TT, :] = a
        bt_ref[t0:t0 + TT, :] = bt

    def group(gi, h):
        r0 = pl.multiple_of(gi * SUBLANES, SUBLANES)
        hg = bt_ref[pl.ds(r0, SUBLANES), :] + a_ref[pl.ds(r0, SUBLANES), :] * h
        bt_ref[pl.ds(r0, SUBLANES), :] = hg
        return hg[SUBLANES - 1:SUBLANES, :]

    lax.fori_loop(0, s // SUBLANES, group, jnp.zeros((1, c), F32), unroll=8)

    for ci in range(s // TT):
        t0 = ci * TT
        o_ref[0, t0:t0 + TT, :] = jax.nn.gelu(y_ref[0, t0:t0 + TT, :]) * bt_ref[t0:t0 + TT, :]


def _rglru(r_in, conv_w, conv_b, w_cat, b_cat, lam):
    bsz, s, _ = r_in.shape
    c = RNN_BLOCK_DIM
    nb = RNN_BLOCKS
    return pl.pallas_call(
        _rglru_kernel,
        grid=(bsz, nb),
        in_specs=[
            pl.BlockSpec((1, s, c), lambda b, j: (b, 0, j)),
            pl.BlockSpec((1, s, c), lambda b, j: (b, 0, nb + j)),
            pl.BlockSpec((CONV_WIDTH, c), lambda b, j: (0, j)),
            pl.BlockSpec((1, c), lambda b, j: (0, j)),
            pl.BlockSpec((1, c, 2 * c), lambda b, j: (j, 0, 0)),
            pl.BlockSpec((1, 1, 2 * c), lambda b, j: (j, 0, 0)),
            pl.BlockSpec((1, c), lambda b, j: (0, j)),
        ],
        out_specs=pl.BlockSpec((1, s, c), lambda b, j: (b, 0, j)),
        out_shape=jax.ShapeDtypeStruct((bsz, s, D_RNN), F32),
        scratch_shapes=[
            pltpu.VMEM((s + SUBLANES, c), F32),
            pltpu.VMEM((s, c), F32),
            pltpu.VMEM((s, c), F32),
        ],
        compiler_params=_params(("parallel", "arbitrary")),
        name="rg_lru",
    )(r_in, r_in, conv_w, conv_b, w_cat, b_cat, lam)


def _outproj_kernel(oa_ref, or_ref, ga_ref, gr_ref, w_ref, x_ref, gt_ref, gp_ref, o_ref):
    a = _rms(oa_ref[0], ga_ref[...]).astype(BF16)
    r = _rms(or_ref[0], gr_ref[...]).astype(BF16)
    mix = _dot(a, w_ref[0:D_ATT, :]) + _dot(r, w_ref[D_ATT:D_ATT + D_RNN, :])
    o_ref[0] = x_ref[0] + gt_ref[0, 0] * _rms(mix, gp_ref[...])


def _outproj(o_att, o_rnn, g_att, g_rnn, w_out, x, mod6, g_post, tm=256):
    bsz, s, d = x.shape
    row = lambda width: pl.BlockSpec((1, tm, width), lambda b, i: (b, i, 0))
    vec = lambda width: pl.BlockSpec((1, width), lambda b, i: (0, 0))
    return pl.pallas_call(
        _outproj_kernel,
        grid=(bsz, s // tm),
        in_specs=[
            row(D_ATT), row(D_RNN), vec(D_ATT), vec(D_RNN),
            pl.BlockSpec((D_ATT + D_RNN, d), lambda b, i: (0, 0), pipeline_mode=pl.Buffered(1)),
            row(d),
            pl.BlockSpec((1, 1, 1, d), lambda b, i: (2, b, 0, 0)),
            vec(d),
        ],
        out_specs=row(d),
        out_shape=jax.ShapeDtypeStruct((bsz, s, d), F32),
        compiler_params=_params(("parallel", "arbitrary")),
        name="out_proj",
    )(o_att, o_rnn, g_att.reshape(1, -1), g_rnn.reshape(1, -1), w_out, x, mod6, g_post.reshape(1, d))


def _mlp_kernel(x_ref, g_ref, sc_ref, sh_ref, w1_ref, w2_ref, gt_ref, gp_ref, o_ref,
                h_ref, a_ref, acc_ref):
    j = pl.program_id(2)
    n_chunks = pl.num_programs(2) - 1
    slot = j % 2

    def up():
        u = jnp.maximum(_dot(h_ref[...], w1_ref[...]), 0.0)
        a_ref[slot] = (u * u).astype(BF16)

    def down():
        acc_ref[...] += _dot(a_ref[1 - slot], w2_ref[...])

    @pl.when(j == 0)
    def _():
        h = _rms(x_ref[0], g_ref[...]) * (1.0 + sc_ref[0, 0]) + sh_ref[0, 0]
        h_ref[...] = h.astype(BF16)
        acc_ref[...] = jnp.zeros(acc_ref.shape, F32)
        up()

    @pl.when((j > 0) & (j < n_chunks))
    def _():
        down()
        up()

    @pl.when(j == n_chunks)
    def _():
        down()
        o_ref[0] = x_ref[0] + gt_ref[0, 0] * _rms(acc_ref[...], gp_ref[...])


def _mlp(x, g_pre, mod6, w1, w2, g_post, tm=512, tf=1024):
    bsz, s, d = x.shape
    n_chunks = w1.shape[1] // tf
    row = pl.BlockSpec((1, tm, d), lambda b, i, j: (b, i, 0))
    vec = pl.BlockSpec((1, d), lambda b, i, j: (0, 0))
    modk = lambda k: pl.BlockSpec((1, 1, 1, d), lambda b, i, j: (k, b, 0, 0))
    return pl.pallas_call(
        _mlp_kernel,
        grid=(bsz, s // tm, n_chunks + 1),
        in_specs=[
            row, vec, modk(4), modk(3),
            pl.BlockSpec((d, tf), lambda b, i, j: (0, jnp.minimum(j, n_chunks - 1))),
            pl.BlockSpec((tf, d), lambda b, i, j: (jnp.maximum(j - 1, 0), 0)),
            modk(5), vec,
        ],
        out_specs=row,
        out_shape=jax.ShapeDtypeStruct((bsz, s, d), F32),
        scratch_shapes=[pltpu.VMEM((tm, d), BF16), pltpu.VMEM((2, tm, tf), BF16),
                        pltpu.VMEM((tm, d), F32)],
        compiler_params=_params(("parallel", "parallel", "arbitrary")),
        name="mlp",
    )(x, g_pre.reshape(1, d), mod6, mod6, w1, w2, mod6, g_post.reshape(1, d))


def _layer(x, c, w_ada, b_ada, g_pre_mix, g_post_mix, g_pre_mlp, g_post_mlp, w_in,
           cmp_w1_k, cmp_w2_k, cmp_pe_k, cmp_w1_v, cmp_w2_v, cmp_pe_v,
           conv_w, conv_b, w_rg_a, b_rg_a, w_rg_x, b_rg_x, lru_lambda,
           g_grp_att, g_grp_rnn, w_out, w_ff1, w_ff2):
    bsz, s, d = x.shape
    g = N_KV_GROUPS
    dh = HEAD_DIM
    nq = s // TQ

    mod = _ada(c, w_ada, b_ada)
    mod6 = mod.reshape(bsz, 6, 1, d).transpose(1, 0, 2, 3)

    n_gl = N_BRANCH * N_Q_HEADS
    c_gl0 = D_ATT + 6 * D_KV
    kv = lambda n: w_in[:, D_ATT + n * D_KV:D_ATT + (n + 1) * D_KV]
    w_perm = jnp.concatenate(
        [w_in[:, :D_ATT], kv(0), kv(1), kv(2), kv(4), kv(3), kv(5),
         w_in[:, c_gl0 + n_gl:], w_in[:, c_gl0:c_gl0 + n_gl],
         jnp.zeros((d, GL_PAD - n_gl), w_in.dtype)], axis=1).astype(BF16)
    qt, kvc, ks, kw, vst, vwt, r_in, glt = _inproj(x, g_pre_mix, mod6, w_perm)

    def pair_diag(w, axis):
        z = jnp.zeros_like(w)
        return jnp.concatenate([jnp.concatenate([w, z], axis=-1), jnp.concatenate([z, w], axis=-1)], axis=axis)

    pe = jnp.tile(jnp.stack([cmp_pe_k, cmp_pe_v]), (1, 1, 2))
    w1c = pair_diag(jnp.stack([cmp_w1_k, cmp_w1_v]).astype(BF16).reshape(2, CMP_BLOCK, dh, -1), 2)
    w2c = pair_diag(jnp.stack([cmp_w2_k, cmp_w2_v]).astype(BF16), 1)
    kv_cmp = _compress(kvc, pe, w1c, w2c)

    o_cmp, sel_bias = _cmp_select(qt, kv_cmp)
    o_att = _attention(qt, o_cmp, sel_bias, ks, kw, vst, vwt, glt)

    w_cat = jnp.concatenate([w_rg_a, w_rg_x], axis=-1).astype(BF16)
    b_cat = jnp.concatenate([b_rg_a.reshape(RNN_BLOCKS, 1, RNN_BLOCK_DIM),
                             b_rg_x.reshape(RNN_BLOCKS, 1, RNN_BLOCK_DIM)], axis=-1)
    o_rnn = _rglru(r_in, conv_w, conv_b.reshape(1, -1), w_cat, b_cat, lru_lambda.reshape(1, -1))

    x1 = _outproj(o_att, o_rnn, g_grp_att, g_grp_rnn, w_out.astype(BF16), x, mod6, g_post_mix)
    return _mlp(x1, g_pre_mlp, mod6, w_ff1.astype(BF16), w_ff2.astype(BF16), g_post_mlp)


def kernel(x, c, w_ada, b_ada, g_pre_mix, g_post_mix, g_pre_mlp, g_post_mlp, w_in, cmp_w1_k, cmp_w2_k, cmp_pe_k, cmp_w1_v, cmp_w2_v, cmp_pe_v, conv_w, conv_b, w_rg_a, b_rg_a, w_rg_x, b_rg_x, lru_lambda, g_grp_att, g_grp_rnn, w_out, w_ff1, w_ff2):
    depth = w_ada.shape[0]
    for l in range(depth):
        x = _layer(x, c, w_ada[l], b_ada[l], g_pre_mix[l], g_post_mix[l], g_pre_mlp[l], g_post_mlp[l],
                   w_in[l], cmp_w1_k[l], cmp_w2_k[l], cmp_pe_k[l], cmp_w1_v[l], cmp_w2_v[l], cmp_pe_v[l],
                   conv_w[l], conv_b[l], w_rg_a[l], b_rg_a[l], w_rg_x[l], b_rg_x[l], lru_lambda[l],
                   g_grp_att[l], g_grp_rnn[l], w_out[l], w_ff1[l], w_ff2[l])
    return x
```

```python
import functools

import jax
import jax.numpy as jnp
from jax import lax
from jax.experimental import pallas as pl
from jax.experimental.pallas import tpu as pltpu

F32 = jnp.float32
BF16 = jnp.bfloat16

D_MODEL = 2048
D_ATT = 1024
D_RNN = 1024
N_Q_HEADS = 16
N_KV_GROUPS = 4
HEADS_PER_GROUP = 4
HEAD_DIM = 64
D_KV = 256
CMP_BLOCK = 32
CMP_STRIDE = 16
SEL_BLOCK = 64
SEL_SHIFT = 6
SEL_TOP_N = 8
WINDOW = 512
N_BRANCH = 3
RNN_BLOCKS = 8
RNN_BLOCK_DIM = 128
CONV_WIDTH = 4
LRU_C = 8.0
D_FF = 4 * D_MODEL
EPS = 1e-6
NEG = -1e30
FORCE_SCORE = 1e9
LOG2_E = 1.4426950408889634

LANES = 128
SUBLANES = 8
VMEM_LIMIT = 56 * 1024 * 1024

TQ = 256
N_CMP_PAD = 128
N_SEL = 32
GL_PAD = 128
TT = 256


def _params(sem):
    return pltpu.CompilerParams(dimension_semantics=sem, vmem_limit_bytes=VMEM_LIMIT)


def _dot(a, b):
    return jnp.dot(a, b, preferred_element_type=F32)


def _rms(x, g):
    return x * lax.rsqrt(jnp.mean(x * x, axis=-1, keepdims=True) + EPS) * g


def _ada_kernel(c_ref, w_ref, b_ref, o_ref):
    c = c_ref[...]
    ca = (c * jax.nn.sigmoid(c)).astype(BF16)
    o_ref[...] = _dot(ca, w_ref[...].astype(BF16)) + b_ref[...]


def _ada(c, w, b):
    bsz, d = c.shape
    n = w.shape[1]
    tn = 1024
    return pl.pallas_call(
        _ada_kernel,
        grid=(n // tn,),
        in_specs=[
            pl.BlockSpec((bsz, d), lambda j: (0, 0)),
            pl.BlockSpec((d, tn), lambda j: (0, j)),
            pl.BlockSpec((1, tn), lambda j: (0, j)),
        ],
        out_specs=pl.BlockSpec((bsz, tn), lambda j: (0, j)),
        out_shape=jax.ShapeDtypeStruct((bsz, n), F32),
        compiler_params=_params(("arbitrary",)),
        name="ada_mod",
    )(c, w, b.reshape(1, n))


C_Q = (0, D_ATT)
C_CMP = (C_Q[1], C_Q[1] + 2 * D_KV)
C_K = (C_CMP[1], C_CMP[1] + 2 * D_KV)
C_V = (C_K[1], C_K[1] + 2 * D_KV)
C_R = (C_V[1], C_V[1] + 2 * D_RNN)
C_GL = (C_R[1], C_R[1] + GL_PAD)
D_IN_PAD = C_GL[1]
K_ROW = LANES
V_ROWS = HEAD_DIM + 16
N_GATE = HEADS_PER_GROUP * N_BRANCH


def _inproj_kernel(x_ref, g_ref, sc_ref, sh_ref, w_ref,
                   qt_ref, cmp_ref, ks_ref, kw_ref, vst_ref, vwt_ref, r_ref, glt_ref):
    tm = x_ref.shape[1]
    ng, hg, dh = N_KV_GROUPS, HEADS_PER_GROUP, HEAD_DIM
    x = x_ref[0]
    h = _rms(x, g_ref[...]) * (1.0 + sc_ref[0, 0]) + sh_ref[0, 0]
    hb = h.astype(BF16)
    tiles = [slice(u * TQ, (u + 1) * TQ) for u in range(tm // TQ)]

    q_t = (_dot(hb, w_ref[:, C_Q[0]:C_Q[1]]) * (dh ** -0.5 * LOG2_E)).T.astype(BF16)
    for j in range(ng):
        for u, cols in enumerate(tiles):
            qt_ref[0, j, u] = jnp.concatenate(
                [q_t[(hg * j + hh) * dh:(hg * j + hh + 1) * dh, cols] for hh in range(hg)], axis=1)

    kv_cmp = _dot(hb, w_ref[:, C_CMP[0]:C_CMP[1]])
    for c in range(2 * D_KV // LANES):
        cmp_ref[0, c] = kv_cmp[:, c * LANES:(c + 1) * LANES]

    keys = _dot(hb, w_ref[:, C_K[0]:C_K[1]])
    row_pos = pl.program_id(1) * tm + lax.broadcasted_iota(jnp.int32, (tm, K_ROW - dh), 0)
    lane = lax.broadcasted_iota(jnp.int32, (tm, K_ROW - dh), 1)
    onehot = jnp.where(lax.shift_right_logical(row_pos, SEL_SHIFT) == lane, 1.0, 0.0)
    for j in range(ng):
        ks_ref[0, j] = jnp.concatenate([keys[:, j * dh:(j + 1) * dh], onehot], axis=1).astype(BF16)
        kw_ref[0, j] = jnp.concatenate(
            [keys[:, D_KV + j * dh:D_KV + (j + 1) * dh], onehot], axis=1).astype(BF16)

    v_t = _dot(hb, w_ref[:, C_V[0]:C_V[1]]).T.astype(BF16)
    ones = jnp.ones((V_ROWS - dh, TQ), BF16)
    for j in range(ng):
        for u, cols in enumerate(tiles):
            vst_ref[0, j, u] = jnp.concatenate([v_t[j * dh:(j + 1) * dh, cols], ones], axis=0)
            vwt_ref[0, j, u] = jnp.concatenate([v_t[D_KV + j * dh:D_KV + (j + 1) * dh, cols], ones], axis=0)

    half = (C_R[0] + C_R[1]) // 2
    r_ref[0, :, 0:D_RNN] = _dot(hb, w_ref[:, C_R[0]:half])
    r_ref[0, :, D_RNN:2 * D_RNN] = _dot(hb, w_ref[:, half:C_R[1]])

    gl_t = _dot(hb, w_ref[:, C_GL[0]:C_GL[1]]).T
    for j in range(ng):
        glt_ref[0, j] = gl_t[N_GATE * j:N_GATE * (j + 1), :]


def _inproj(x, g, mod6, w_perm, tm=512):
    bsz, s, d = x.shape
    ng = N_KV_GROUPS
    nt = tm // TQ
    row = lambda width: pl.BlockSpec((1, tm, width), lambda b, i: (b, i, 0))
    grp_rows = pl.BlockSpec((1, ng, tm, K_ROW), lambda b, i: (b, 0, i, 0))
    grp_tiles = lambda r, c: pl.BlockSpec((1, ng, nt, r, c), lambda b, i: (b, 0, i, 0, 0))
    sds = jax.ShapeDtypeStruct
    return pl.pallas_call(
        _inproj_kernel,
        grid=(bsz, s // tm),
        in_specs=[
            row(d),
            pl.BlockSpec((1, d), lambda b, i: (0, 0)),
            pl.BlockSpec((1, 1, 1, d), lambda b, i: (1, b, 0, 0)),
            pl.BlockSpec((1, 1, 1, d), lambda b, i: (0, b, 0, 0)),
            pl.BlockSpec((d, D_IN_PAD), lambda b, i: (0, 0), pipeline_mode=pl.Buffered(1)),
        ],
        out_specs=[
            grp_tiles(HEAD_DIM, HEADS_PER_GROUP * TQ),
            pl.BlockSpec((1, 2 * D_KV // LANES, tm, LANES), lambda b, i: (b, 0, i, 0)),
            grp_rows, grp_rows,
            grp_tiles(V_ROWS, TQ), grp_tiles(V_ROWS, TQ), row(2 * D_RNN),
            pl.BlockSpec((1, ng, N_GATE, tm), lambda b, i: (b, 0, 0, i)),
        ],
        out_shape=[
            sds((bsz, ng, s // TQ, HEAD_DIM, HEADS_PER_GROUP * TQ), BF16),
            sds((bsz, 2 * D_KV // LANES, s, LANES), F32),
            sds((bsz, ng, s, K_ROW), BF16), sds((bsz, ng, s, K_ROW), BF16),
            sds((bsz, ng, s // TQ, V_ROWS, TQ), BF16), sds((bsz, ng, s // TQ, V_ROWS, TQ), BF16),
            sds((bsz, s, 2 * D_RNN), F32),
            sds((bsz, ng, N_GATE, s), F32),
        ],
        compiler_params=_params(("parallel", "arbitrary")),
        name="in_proj",
    )(x, g.reshape(1, d), mod6, mod6, w_perm)


def _compress_kernel(x_ref, pe_ref, w1_ref, w2_ref, o_ref):
    bsz = x_ref.shape[0]
    dh = HEAD_DIM
    n_rows = x_ref.shape[2] // CMP_STRIDE
    rows = bsz * n_rows
    first = jnp.zeros((rows, w1_ref.shape[3]), F32)
    second = jnp.zeros((rows, w1_ref.shape[3]), F32)
    for l in range(CMP_STRIDE):
        x_l = x_ref[:, 0, pl.ds(l, n_rows, stride=CMP_STRIDE), :].reshape(rows, LANES)
        first = first + _dot((x_l + pe_ref[0, l:l + 1, :]).astype(BF16), w1_ref[0, l])
        second = second + _dot(
            (x_l + pe_ref[0, CMP_STRIDE + l:CMP_STRIDE + l + 1, :]).astype(BF16), w1_ref[0, CMP_STRIDE + l])
    hid = first + pltpu.roll(second, rows - 1, axis=0)
    out = _dot(jax.nn.gelu(hid).astype(BF16), w2_ref[0]).astype(BF16)
    for b in range(bsz):
        for j in range(2):
            o_ref[0, b, j] = out[b * n_rows:(b + 1) * n_rows, j * dh:(j + 1) * dh]


def _compress(kvc4, pe, w1, w2):
    bsz, n_tiles, s, _ = kvc4.shape
    n_rows = s // CMP_STRIDE
    per_kind = lambda a: pl.BlockSpec((1,) + a.shape[1:], lambda t, p: (t,) + (0,) * (a.ndim - 1))
    return pl.pallas_call(
        _compress_kernel,
        grid=(2, n_tiles // 2),
        in_specs=[pl.BlockSpec((bsz, 1, s, LANES), lambda t, p: (0, 2 * t + p, 0, 0)),
                  per_kind(pe), per_kind(w1), per_kind(w2)],
        out_specs=pl.BlockSpec((1, bsz, 2, n_rows, HEAD_DIM), lambda t, p: (t, 0, p, 0, 0)),
        out_shape=jax.ShapeDtypeStruct((2, bsz, N_KV_GROUPS, n_rows, HEAD_DIM), BF16),
        compiler_params=_params(("arbitrary", "arbitrary")),
        name="compress_kv",
    )(kvc4, pe, w1, w2)


def _cmp_select_kernel(qt_ref, kc_ref, vc_ref, oc_ref, bias_ref):
    hg = HEADS_PER_GROUP
    kc = kc_ref[0, 0, 0]
    vct = vc_ref[0, 0, 0].astype(F32).T.astype(BF16)
    jj = lax.broadcasted_iota(jnp.int32, (N_SEL, N_CMP_PAD), 0) * SEL_BLOCK
    nn = lax.broadcasted_iota(jnp.int32, (N_SEL, N_CMP_PAD), 1) * CMP_STRIDE
    ov = jnp.minimum(nn + CMP_BLOCK, jj + SEL_BLOCK) - jnp.maximum(nn, jj)
    w_sel = (jnp.maximum(ov, 0).astype(F32) * (1.0 / CMP_BLOCK)).astype(BF16)
    nrow = lax.broadcasted_iota(jnp.int32, (N_CMP_PAD, TQ), 0)
    blk = lax.broadcasted_iota(jnp.int32, (N_SEL, TQ), 0)

    for qi in range(qt_ref.shape[2]):
        qt = qt_ref[0, 0, qi]
        pos = qi * TQ + lax.broadcasted_iota(jnp.int32, (1, TQ), 1)
        mask_c = nrow * CMP_STRIDE + (CMP_BLOCK - 1) <= pos

        def per_head_where(a, fill):
            return jnp.concatenate(
                [jnp.where(mask_c, a[:, h * TQ:(h + 1) * TQ], fill) for h in range(hg)], axis=1)

        s = per_head_where(_dot(kc, qt), NEG)
        e = jnp.exp2(s - jnp.max(s, axis=0, keepdims=True))
        p = per_head_where(e * (1.0 / jnp.sum(e, axis=0, keepdims=True)), 0.0)
        oc_ref[0, 0, qi] = _dot(vct, p.astype(BF16))
        psum = p[:, 0:TQ]
        for h in range(1, hg):
            psum = psum + p[:, h * TQ:(h + 1) * TQ]

        p_hi = psum.astype(BF16)
        p_lo = (psum - p_hi.astype(F32)).astype(BF16)
        imp = _dot(w_sel, p_hi) + _dot(w_sel, p_lo)
        cur = lax.shift_right_logical(pos, SEL_SHIFT)
        free = jnp.where(blk * SEL_BLOCK <= pos, imp, -FORCE_SCORE)
        score = jnp.where(blk == 0, FORCE_SCORE,
                          jnp.where(blk == cur, FORCE_SCORE, jnp.where(blk == cur - 1, FORCE_SCORE, free)))
        rank = jnp.zeros((N_SEL, TQ), F32)
        for k in range(N_SEL):
            sk = score[k:k + 1, :]
            tie = jnp.where(blk > k, 1.0, 0.0)
            rank = rank + jnp.where(sk > score, 1.0, jnp.where(sk == score, tie, 0.0))
        bias_ref[0, 0, qi] = jnp.where(rank < SEL_TOP_N, 0.0, NEG).astype(BF16)


def _cmp_select(qt, kv_cmp):
    bsz, g, nq, dh, wide = qt.shape
    per_bg = lambda *shape: pl.BlockSpec((1, 1) + shape, lambda b, j: (b, j) + (0,) * len(shape))
    kind = lambda t: pl.BlockSpec((1, 1, 1, N_CMP_PAD, dh), lambda b, j: (t, b, j, 0, 0))
    return pl.pallas_call(
        _cmp_select_kernel,
        grid=(bsz, g),
        in_specs=[per_bg(nq, dh, wide), kind(0), kind(1)],
        out_specs=[per_bg(nq, dh, wide), per_bg(nq, N_SEL, TQ)],
        out_shape=[jax.ShapeDtypeStruct((bsz, g, nq, dh, wide), F32),
                   jax.ShapeDtypeStruct((bsz, g, nq, N_SEL, TQ), BF16)],
        compiler_params=_params(("parallel", "arbitrary")),
        name="cmp_select",
    )(qt, kv_cmp, kv_cmp)


def _attn_kernel(qt_ref, oc_ref, bias_ref, ks_ref, kw_ref, vst_ref, vwt_ref, gl_ref, o_ref,
                 m_ref, acc_ref, sc_ref):
    qi = pl.program_id(2)
    last = pl.num_programs(2) - 1
    hg = HEADS_PER_GROUP
    dh = HEAD_DIM
    wide = hg * TQ
    qt = qt_ref[0, 0, 0]

    def per_head_where(mask, a, fill):
        return jnp.concatenate(
            [jnp.where(mask, a[:, h * TQ:(h + 1) * TQ], fill) for h in range(hg)], axis=1)

    def block_bias(value):
        return jnp.full((N_SEL, wide), value, F32).astype(BF16)

    zero_rows = jnp.zeros((K_ROW - dh - N_SEL, wide), BF16)

    def scores(k_ref, kt, bias):
        k_t = k_ref[0, 0, pl.ds(pl.multiple_of(kt * TQ, TQ), TQ), :]
        return _dot(k_t, jnp.concatenate([qt, bias, zero_rows], axis=0))

    rel = (lax.broadcasted_iota(jnp.int32, (TQ, TQ), 0)
           - lax.broadcasted_iota(jnp.int32, (TQ, TQ), 1))

    k1 = jnp.maximum(qi - 1, 0)
    k2 = jnp.maximum(qi - 2, 0)
    def online(state, sc, v_t):
        m_tile = jnp.max(sc, axis=0, keepdims=True)
        if state is None:
            return m_tile, _dot(v_t, jnp.exp2(sc - m_tile).astype(BF16))
        m_old, acc_old = state
        m_new = jnp.maximum(m_old, m_tile)
        return m_new, jnp.exp2(m_old - m_new) * acc_old + _dot(v_t, jnp.exp2(sc - m_new).astype(BF16))

    s0 = per_head_where(rel <= 0, scores(kw_ref, qi, block_bias(0.0)), NEG)
    s1 = scores(kw_ref, k1, block_bias(jnp.where(qi >= 1, 0.0, NEG)))
    s2 = per_head_where(rel > 0, scores(kw_ref, k2, block_bias(jnp.where(qi >= 2, 0.0, NEG))), NEG)
    win = online(None, s0, vwt_ref[0, 0, qi])
    win = online(win, s1, vwt_ref[0, 0, k1])
    win = online(win, s2, vwt_ref[0, 0, k2])
    acc_ref[1] = win[1]

    bias_sel = jnp.concatenate([bias_ref[0, 0, 0]] * hg, axis=1)

    sd = per_head_where(rel <= 0, scores(ks_ref, qi, bias_sel), NEG)
    m_ref[...], acc_ref[0] = online(None, sd, vst_ref[0, 0, qi])

    def sel_scores(kt):
        pad_bias = block_bias(jnp.where(kt < qi, 0.0, NEG))
        return scores(ks_ref, jnp.minimum(kt, last), jnp.minimum(bias_sel, pad_bias))

    def sel_accumulate(sc, kt):
        m_old = m_ref[...]
        m_new = jnp.maximum(m_old, jnp.max(sc, axis=0, keepdims=True))
        m_ref[...] = m_new
        acc_ref[0] = (jnp.exp2(m_old - m_new) * acc_ref[0]
                      + _dot(vst_ref[0, 0, jnp.minimum(kt, last)], jnp.exp2(sc - m_new).astype(BF16)))

    sc_ref[0] = sel_scores(0)

    def sel_pair(pi, carry):
        kt = 2 * pi
        sc_ref[1] = sel_scores(kt + 1)
        sel_accumulate(sc_ref[0], kt)
        sc_ref[0] = sel_scores(kt + 2)
        sel_accumulate(sc_ref[1], kt + 1)
        return carry

    lax.fori_loop(0, (qi + 1) // 2, sel_pair, 0)

    gate = jax.nn.sigmoid(gl_ref[0, 0])
    o_cmp = oc_ref[0, 0, 0]
    acc_s = acc_ref[0]
    acc_w = acc_ref[1]
    o_sel = acc_s[0:dh, :] * (1.0 / acc_s[dh:dh + 1, :])
    o_win = acc_w[0:dh, :] * (1.0 / acc_w[dh:dh + 1, :])
    outs = []
    for h in range(hg):
        lanes = slice(h * TQ, (h + 1) * TQ)
        g0 = gate[3 * h:3 * h + 1, :]
        g1 = gate[3 * h + 1:3 * h + 2, :]
        g2 = gate[3 * h + 2:3 * h + 3, :]
        outs.append(g0 * o_cmp[:, lanes] + g1 * o_sel[:, lanes] + g2 * o_win[:, lanes])
    o_ref[0] = jnp.concatenate(outs, axis=0).T


def _attention(qt, o_cmp, bias, ks, kw, vst, vwt, glt):
    bsz, g, nq = qt.shape[:3]
    s = nq * TQ
    width = HEADS_PER_GROUP * HEAD_DIM
    wide = HEADS_PER_GROUP * TQ
    per_bg = lambda *shape: pl.BlockSpec((1, 1) + shape, lambda b, j, i: (b, j) + (0,) * len(shape))
    per_tile = lambda *shape: pl.BlockSpec((1, 1, 1) + shape, lambda b, j, i: (b, j, i) + (0,) * len(shape))
    return pl.pallas_call(
        _attn_kernel,
        grid=(bsz, g, nq),
        in_specs=[
            per_tile(HEAD_DIM, wide),
            per_tile(HEAD_DIM, wide),
            per_tile(N_SEL, TQ),
            per_bg(s, K_ROW),
            per_bg(s, K_ROW),
            per_bg(nq, V_ROWS, TQ),
            per_bg(nq, V_ROWS, TQ),
            pl.BlockSpec((1, 1, HEADS_PER_GROUP * N_BRANCH, TQ), lambda b, j, i: (b, j, 0, i)),
        ],
        out_specs=pl.BlockSpec((1, TQ, width), lambda b, j, i: (b, i, j)),
        out_shape=jax.ShapeDtypeStruct((bsz, s, D_ATT), F32),
        scratch_shapes=[
            pltpu.VMEM((1, wide), F32),
            pltpu.VMEM((2, V_ROWS, wide), F32),
            pltpu.VMEM((2, TQ, wide), F32),
        ],
        compiler_params=_params(("parallel", "parallel", "arbitrary")),
        name="nsa_attention",
    )(qt, o_cmp, bias, ks, kw, vst, vwt, glt)


def _rglru_kernel(x_ref, y_ref, cw_ref, cb_ref, w_ref, b_ref, lam_ref, o_ref,
                  xpad_ref, a_ref, bt_ref):
    s = x_ref.shape[1]
    c = RNN_BLOCK_DIM
    pad = SUBLANES
    xpad_ref[0:pad, :] = jnp.zeros((pad, c), F32)
    xpad_ref[pad:pad + s, :] = x_ref[0]
    cw = cw_ref[...]
    w = w_ref[0]
    bias = b_ref[0]
    lam = lam_ref[...]
    neg_softplus = -(jnp.maximum(-lam, 0.0) + jnp.log1p(jnp.exp(-jnp.abs(lam))))
    sub = lax.broadcasted_iota(jnp.int32, (TT, c), 0) & (SUBLANES - 1)

    for ci in range(s // TT):
        t0 = ci * TT
        xc = cb_ref[...] + sum(
            xpad_ref[t0 + pad - (CONV_WIDTH - 1) + k:t0 + pad - (CONV_WIDTH - 1) + k + TT, :] * cw[k:k + 1, :]
            for k in range(CONV_WIDTH))
        gates = jax.nn.sigmoid(_dot(xc.astype(BF16), w) + bias)
        r = gates[:, 0:c]
        i = gates[:, c:2 * c]
        log_a = LRU_C * r * neg_softplus
        a = jnp.exp(log_a)
        var = -jnp.tanh(log_a) * (a * a + 1.0)
        bt = jnp.where(var > 0.0, var * lax.rsqrt(var), 0.0) * (i * xc)
        for d in (1, 2, 4):
            keep = sub >= d
            a_prev = jnp.where(keep, pltpu.roll(a, d, axis=0), 1.0)
            b_prev = jnp.where(keep, pltpu.roll(bt, d, axis=0), 0.0)
            bt = bt + a * b_prev
            a = a * a_prev
        a_ref[t0:t0 + TT, :] = a
        bt_ref[t0:t0 + TT, :] = bt

    def group(gi, h):
        r0 = pl.multiple_of(gi * SUBLANES, SUBLANES)
        hg = bt_ref[pl.ds(r0, SUBLANES), :] + a_ref[pl.ds(r0, SUBLANES), :] * h
        bt_ref[pl.ds(r0, SUBLANES), :] = hg
        return hg[SUBLANES - 1:SUBLANES, :]

    lax.fori_loop(0, s // SUBLANES, group, jnp.zeros((1, c), F32), unroll=8)

    for ci in range(s // TT):
        t0 = ci * TT
        o_ref[0, t0:t0 + TT, :] = jax.nn.gelu(y_ref[0, t0:t0 + TT, :]) * bt_ref[t0:t0 + TT, :]


def _rglru(r_in, conv_w, conv_b, w_cat, b_cat, lam):
    bsz, s, _ = r_in.shape
    c = RNN_BLOCK_DIM
    nb = RNN_BLOCKS
    return pl.pallas_call(
        _rglru_kernel,
        grid=(bsz, nb),
        in_specs=[
            pl.BlockSpec((1, s, c), lambda b, j: (b, 0, j)),
            pl.BlockSpec((1, s, c), lambda b, j: (b, 0, nb + j)),
            pl.BlockSpec((CONV_WIDTH, c), lambda b, j: (0, j)),
            pl.BlockSpec((1, c), lambda b, j: (0, j)),
            pl.BlockSpec((1, c, 2 * c), lambda b, j: (j, 0, 0)),
            pl.BlockSpec((1, 1, 2 * c), lambda b, j: (j, 0, 0)),
            pl.BlockSpec((1, c), lambda b, j: (0, j)),
        ],
        out_specs=pl.BlockSpec((1, s, c), lambda b, j: (b, 0, j)),
        out_shape=jax.ShapeDtypeStruct((bsz, s, D_RNN), F32),
        scratch_shapes=[
            pltpu.VMEM((s + SUBLANES, c), F32),
            pltpu.VMEM((s, c), F32),
            pltpu.VMEM((s, c), F32),
        ],
        compiler_params=_params(("parallel", "arbitrary")),
        name="rg_lru",
    )(r_in, r_in, conv_w, conv_b, w_cat, b_cat, lam)


def _outproj_kernel(oa_ref, or_ref, ga_ref, gr_ref, w_ref, x_ref, gt_ref, gp_ref, o_ref):
    a = _rms(oa_ref[0], ga_ref[...]).astype(BF16)
    r = _rms(or_ref[0], gr_ref[...]).astype(BF16)
    mix = _dot(a, w_ref[0:D_ATT, :]) + _dot(r, w_ref[D_ATT:D_ATT + D_RNN, :])
    o_ref[0] = x_ref[0] + gt_ref[0, 0] * _rms(mix, gp_ref[...])


def _outproj(o_att, o_rnn, g_att, g_rnn, w_out, x, mod6, g_post, tm=512):
    bsz, s, d = x.shape
    row = lambda width: pl.BlockSpec((1, tm, width), lambda b, i: (b, i, 0))
    vec = lambda width: pl.BlockSpec((1, width), lambda b, i: (0, 0))
    return pl.pallas_call(
        _outproj_kernel,
        grid=(bsz, s // tm),
        in_specs=[
            row(D_ATT), row(D_RNN), vec(D_ATT), vec(D_RNN),
            pl.BlockSpec((D_ATT + D_RNN, d), lambda b, i: (0, 0), pipeline_mode=pl.Buffered(1)),
            row(d),
            pl.BlockSpec((1, 1, 1, d), lambda b, i: (2, b, 0, 0)),
            vec(d),
        ],
        out_specs=row(d),
        out_shape=jax.ShapeDtypeStruct((bsz, s, d), F32),
        compiler_params=_params(("parallel", "arbitrary")),
        name="out_proj",
    )(o_att, o_rnn, g_att.reshape(1, -1), g_rnn.reshape(1, -1), w_out, x, mod6, g_post.reshape(1, d))


def _mlp_kernel(x_ref, g_ref, sc_ref, sh_ref, w1_ref, w2_ref, gt_ref, gp_ref, o_ref,
                h_ref, a_ref, acc_ref):
    j = pl.program_id(2)
    n_chunks = pl.num_programs(2) - 1
    slot = j % 2

    def up():
        u = jnp.maximum(_dot(h_ref[...], w1_ref[...]), 0.0)
        a_ref[slot] = (u * u).astype(BF16)

    def down():
        acc_ref[...] += _dot(a_ref[1 - slot], w2_ref[...])

    @pl.when(j == 0)
    def _():
        h = _rms(x_ref[0], g_ref[...]) * (1.0 + sc_ref[0, 0]) + sh_ref[0, 0]
        h_ref[...] = h.astype(BF16)
        acc_ref[...] = jnp.zeros(acc_ref.shape, F32)
        up()

    @pl.when((j > 0) & (j < n_chunks))
    def _():
        down()
        up()

    @pl.when(j == n_chunks)
    def _():
        down()
        o_ref[0] = x_ref[0] + gt_ref[0, 0] * _rms(acc_ref[...], gp_ref[...])


def _mlp(x, g_pre, mod6, w1, w2, g_post, tm=512, tf=1024):
    bsz, s, d = x.shape
    n_chunks = w1.shape[1] // tf
    row = pl.BlockSpec((1, tm, d), lambda b, i, j: (b, i, 0))
    vec = pl.BlockSpec((1, d), lambda b, i, j: (0, 0))
    modk = lambda k: pl.BlockSpec((1, 1, 1, d), lambda b, i, j: (k, b, 0, 0))
    return pl.pallas_call(
        _mlp_kernel,
        grid=(bsz, s // tm, n_chunks + 1),
        in_specs=[
            row, vec, modk(4), modk(3),
            pl.BlockSpec((d, tf), lambda b, i, j: (0, jnp.minimum(j, n_chunks - 1))),
            pl.BlockSpec((tf, d), lambda b, i, j: (jnp.maximum(j - 1, 0), 0)),
            modk(5), vec,
        ],
        out_specs=row,
        out_shape=jax.ShapeDtypeStruct((bsz, s, d), F32),
        scratch_shapes=[pltpu.VMEM((tm, d), BF16), pltpu.VMEM((2, tm, tf), BF16),
                        pltpu.VMEM((tm, d), F32)],
        compiler_params=_params(("parallel", "parallel", "arbitrary")),
        name="mlp",
    )(x, g_pre.reshape(1, d), mod6, mod6, w1, w2, mod6, g_post.reshape(1, d))


def _layer(x, c, w_ada, b_ada, g_pre_mix, g_post_mix, g_pre_mlp, g_post_mlp, w_in,
           cmp_w1_k, cmp_w2_k, cmp_pe_k, cmp_w1_v, cmp_w2_v, cmp_pe_v,
           conv_w, conv_b, w_rg_a, b_rg_a, w_rg_x, b_rg_x, lru_lambda,
           g_grp_att, g_grp_rnn, w_out, w_ff1, w_ff2):
    bsz, s, d = x.shape
    g = N_KV_GROUPS
    dh = HEAD_DIM
    nq = s // TQ

    mod = _ada(c, w_ada, b_ada)
    mod6 = mod.reshape(bsz, 6, 1, d).transpose(1, 0, 2, 3)

    n_gl = N_BRANCH * N_Q_HEADS
    c_gl0 = D_ATT + 6 * D_KV
    w_b = w_in.astype(BF16)
    kv = lambda n: w_b[:, D_ATT + n * D_KV:D_ATT + (n + 1) * D_KV]
    w_perm = jnp.concatenate(
        [w_b[:, :D_ATT + 3 * D_KV], kv(4), kv(3), kv(5),
         w_b[:, c_gl0 + n_gl:], w_b[:, c_gl0:c_gl0 + n_gl],
         jnp.zeros((d, GL_PAD - n_gl), BF16)], axis=1)
    qt, kvc, ks, kw, vst, vwt, r_in, glt = _inproj(x, g_pre_mix, mod6, w_perm)

    def pair_diag(w, axis):
        z = jnp.zeros_like(w)
        return jnp.concatenate([jnp.concatenate([w, z], axis=-1), jnp.concatenate([z, w], axis=-1)], axis=axis)

    pe = jnp.tile(jnp.stack([cmp_pe_k, cmp_pe_v]), (1, 1, 2))
    w1c = pair_diag(jnp.stack([cmp_w1_k, cmp_w1_v]).astype(BF16).reshape(2, CMP_BLOCK, dh, -1), 2)
    w2c = pair_diag(jnp.stack([cmp_w2_k, cmp_w2_v]).astype(BF16), 1)
    kv_cmp = _compress(kvc, pe, w1c, w2c)

    o_cmp, sel_bias = _cmp_select(qt, kv_cmp)
    o_att = _attention(qt, o_cmp, sel_bias, ks, kw, vst, vwt, glt)

    w_cat = jnp.concatenate([w_rg_a, w_rg_x], axis=-1).astype(BF16)
    b_cat = jnp.concatenate([b_rg_a.reshape(RNN_BLOCKS, 1, RNN_BLOCK_DIM),
                             b_rg_x.reshape(RNN_BLOCKS, 1, RNN_BLOCK_DIM)], axis=-1)
    o_rnn = _rglru(r_in, conv_w, conv_b.reshape(1, -1), w_cat, b_cat, lru_lambda.reshape(1, -1))

    x1 = _outproj(o_att, o_rnn, g_grp_att, g_grp_rnn, w_out.astype(BF16), x, mod6, g_post_mix)
    return _mlp(x1, g_pre_mlp, mod6, w_ff1.astype(BF16), w_ff2.astype(BF16), g_post_mlp)


def kernel(x, c, w_ada, b_ada, g_pre_mix, g_post_mix, g_pre_mlp, g_post_mlp, w_in, cmp_w1_k, cmp_w2_k, cmp_pe_k, cmp_w1_v, cmp_w2_v, cmp_pe_v, conv_w, conv_b, w_rg_a, b_rg_a, w_rg_x, b_rg_x, lru_lambda, g_grp_att, g_grp_rnn, w_out, w_ff1, w_ff2):
    depth = w_ada.shape[0]
    for l in range(depth):
        x = _layer(x, c, w_ada[l], b_ada[l], g_pre_mix[l], g_post_mix[l], g_pre_mlp[l], g_post_mlp[l],
                   w_in[l], cmp_w1_k[l], cmp_w2_k[l], cmp_pe_k[l], cmp_w1_v[l], cmp_w2_v[l], cmp_pe_v[l],
                   conv_w[l], conv_b[l], w_rg_a[l], b_rg_a[l], w_rg_x[l], b_rg_x[l], lru_lambda[l],
                   g_grp_att[l], g_grp_rnn[l], w_out[l], w_ff1[l], w_ff2[l])
    return x
```

```python
import functools

import jax
import jax.numpy as jnp
from jax import lax
from jax.experimental import pallas as pl
from jax.experimental.pallas import tpu as pltpu

F32 = jnp.float32
BF16 = jnp.bfloat16

D_MODEL = 2048
D_ATT = 1024
D_RNN = 1024
N_Q_HEADS = 16
N_KV_GROUPS = 4
HEADS_PER_GROUP = 4
HEAD_DIM = 64
D_KV = 256
CMP_BLOCK = 32
CMP_STRIDE = 16
SEL_BLOCK = 64
SEL_SHIFT = 6
SEL_TOP_N = 8
WINDOW = 512
N_BRANCH = 3
RNN_BLOCKS = 8
RNN_BLOCK_DIM = 128
CONV_WIDTH = 4
LRU_C = 8.0
D_FF = 4 * D_MODEL
EPS = 1e-6
NEG = -1e30
FORCE_SCORE = 1e9
LOG2_E = 1.4426950408889634

LANES = 128
SUBLANES = 8
VMEM_LIMIT = 56 * 1024 * 1024

TQ = 256
N_CMP_PAD = 128
N_SEL = 32
GL_PAD = 128
TT = 256


def _params(sem):
    return pltpu.CompilerParams(dimension_semantics=sem, vmem_limit_bytes=VMEM_LIMIT)


def _dot(a, b):
    return jnp.dot(a, b, preferred_element_type=F32)


def _rms(x, g):
    return x * lax.rsqrt(jnp.mean(x * x, axis=-1, keepdims=True) + EPS) * g


def _ada_kernel(c_ref, w_ref, b_ref, o_ref):
    c = c_ref[...]
    ca = (c * jax.nn.sigmoid(c)).astype(BF16)
    o_ref[...] = _dot(ca, w_ref[...].astype(BF16)) + b_ref[...]


def _ada(c, w, b):
    bsz, d = c.shape
    n = w.shape[1]
    tn = 1024
    return pl.pallas_call(
        _ada_kernel,
        grid=(n // tn,),
        in_specs=[
            pl.BlockSpec((bsz, d), lambda j: (0, 0)),
            pl.BlockSpec((d, tn), lambda j: (0, j)),
            pl.BlockSpec((1, tn), lambda j: (0, j)),
        ],
        out_specs=pl.BlockSpec((bsz, tn), lambda j: (0, j)),
        out_shape=jax.ShapeDtypeStruct((bsz, n), F32),
        compiler_params=_params(("arbitrary",)),
        name="ada_mod",
    )(c, w, b.reshape(1, n))


C_Q = (0, D_ATT)
C_CMP = (C_Q[1], C_Q[1] + 2 * D_KV)
C_K = (C_CMP[1], C_CMP[1] + 2 * D_KV)
C_V = (C_K[1], C_K[1] + 2 * D_KV)
C_R = (C_V[1], C_V[1] + 2 * D_RNN)
C_GL = (C_R[1], C_R[1] + GL_PAD)
D_IN_PAD = C_GL[1]
K_ROW = LANES
V_ROWS = HEAD_DIM + 16
N_GATE = HEADS_PER_GROUP * N_BRANCH


def _w_in_prep_kernel(w_ref, o_ref):
    cast = lambda lo, hi: w_ref[:, lo:hi].astype(BF16)
    kv0 = D_ATT
    gl0 = D_ATT + 6 * D_KV
    n_gl = N_BRANCH * N_Q_HEADS
    o_ref[:, 0:C_K[0] + D_KV] = cast(0, kv0 + 3 * D_KV)
    o_ref[:, C_K[0] + D_KV:C_K[1]] = cast(kv0 + 4 * D_KV, kv0 + 5 * D_KV)
    o_ref[:, C_V[0]:C_V[0] + D_KV] = cast(kv0 + 3 * D_KV, kv0 + 4 * D_KV)
    o_ref[:, C_V[0] + D_KV:C_V[1]] = cast(kv0 + 5 * D_KV, kv0 + 6 * D_KV)
    o_ref[:, C_R[0]:C_R[1]] = cast(gl0 + n_gl, gl0 + n_gl + 2 * D_RNN)
    o_ref[:, C_GL[0]:C_GL[1]] = jnp.concatenate(
        [cast(gl0, gl0 + n_gl), jnp.zeros((w_ref.shape[0], GL_PAD - n_gl), BF16)], axis=1)


def _w_in_prep(w_in, tk=256):
    d, n = w_in.shape
    return pl.pallas_call(
        _w_in_prep_kernel,
        grid=(d // tk,),
        in_specs=[pl.BlockSpec((tk, n), lambda i: (i, 0))],
        out_specs=pl.BlockSpec((tk, D_IN_PAD), lambda i: (i, 0)),
        out_shape=jax.ShapeDtypeStruct((d, D_IN_PAD), BF16),
        compiler_params=_params(("arbitrary",)),
        name="w_in_prep",
    )(w_in)


def _inproj_kernel(x_ref, g_ref, sc_ref, sh_ref, w_ref,
                   qt_ref, cmp_ref, ks_ref, kw_ref, vst_ref, vwt_ref, r_ref, glt_ref):
    tm = x_ref.shape[1]
    ng, hg, dh = N_KV_GROUPS, HEADS_PER_GROUP, HEAD_DIM
    x = x_ref[0]
    h = _rms(x, g_ref[...]) * (1.0 + sc_ref[0, 0]) + sh_ref[0, 0]
    hb = h.astype(BF16)
    tiles = [slice(u * TQ, (u + 1) * TQ) for u in range(tm // TQ)]

    q_t = (_dot(hb, w_ref[:, C_Q[0]:C_Q[1]]) * (dh ** -0.5 * LOG2_E)).T.astype(BF16)
    for j in range(ng):
        for u, cols in enumerate(tiles):
            qt_ref[0, j, u] = jnp.concatenate(
                [q_t[(hg * j + hh) * dh:(hg * j + hh + 1) * dh, cols] for hh in range(hg)], axis=1)

    kv_cmp = _dot(hb, w_ref[:, C_CMP[0]:C_CMP[1]])
    for c in range(2 * D_KV // LANES):
        cmp_ref[0, c] = kv_cmp[:, c * LANES:(c + 1) * LANES]

    keys = _dot(hb, w_ref[:, C_K[0]:C_K[1]])
    row_pos = pl.program_id(1) * tm + lax.broadcasted_iota(jnp.int32, (tm, K_ROW - dh), 0)
    lane = lax.broadcasted_iota(jnp.int32, (tm, K_ROW - dh), 1)
    onehot = jnp.where(lax.shift_right_logical(row_pos, SEL_SHIFT) == lane, 1.0, 0.0)
    for j in range(ng):
        ks_ref[0, j] = jnp.concatenate([keys[:, j * dh:(j + 1) * dh], onehot], axis=1).astype(BF16)
        kw_ref[0, j] = jnp.concatenate(
            [keys[:, D_KV + j * dh:D_KV + (j + 1) * dh], onehot], axis=1).astype(BF16)

    v_t = _dot(hb, w_ref[:, C_V[0]:C_V[1]]).T.astype(BF16)
    ones = jnp.ones((V_ROWS - dh, TQ), BF16)
    for j in range(ng):
        for u, cols in enumerate(tiles):
            vst_ref[0, j, u] = jnp.concatenate([v_t[j * dh:(j + 1) * dh, cols], ones], axis=0)
            vwt_ref[0, j, u] = jnp.concatenate([v_t[D_KV + j * dh:D_KV + (j + 1) * dh, cols], ones], axis=0)

    half = (C_R[0] + C_R[1]) // 2
    r_ref[0, :, 0:D_RNN] = _dot(hb, w_ref[:, C_R[0]:half])
    r_ref[0, :, D_RNN:2 * D_RNN] = _dot(hb, w_ref[:, half:C_R[1]])

    gl_t = _dot(hb, w_ref[:, C_GL[0]:C_GL[1]]).T
    for j in range(ng):
        glt_ref[0, j] = gl_t[N_GATE * j:N_GATE * (j + 1), :]


def _inproj(x, g, mod6, w_perm, tm=512):
    bsz, s, d = x.shape
    ng = N_KV_GROUPS
    nt = tm // TQ
    row = lambda width: pl.BlockSpec((1, tm, width), lambda b, i: (b, i, 0))
    grp_rows = pl.BlockSpec((1, ng, tm, K_ROW), lambda b, i: (b, 0, i, 0))
    grp_tiles = lambda r, c: pl.BlockSpec((1, ng, nt, r, c), lambda b, i: (b, 0, i, 0, 0))
    sds = jax.ShapeDtypeStruct
    return pl.pallas_call(
        _inproj_kernel,
        grid=(bsz, s // tm),
        in_specs=[
            row(d),
            pl.BlockSpec((1, d), lambda b, i: (0, 0)),
            pl.BlockSpec((1, 1, 1, d), lambda b, i: (1, b, 0, 0)),
            pl.BlockSpec((1, 1, 1, d), lambda b, i: (0, b, 0, 0)),
            pl.BlockSpec((d, D_IN_PAD), lambda b, i: (0, 0), pipeline_mode=pl.Buffered(1)),
        ],
        out_specs=[
            grp_tiles(HEAD_DIM, HEADS_PER_GROUP * TQ),
            pl.BlockSpec((1, 2 * D_KV // LANES, tm, LANES), lambda b, i: (b, 0, i, 0)),
            grp_rows, grp_rows,
            grp_tiles(V_ROWS, TQ), grp_tiles(V_ROWS, TQ), row(2 * D_RNN),
            pl.BlockSpec((1, ng, N_GATE, tm), lambda b, i: (b, 0, 0, i)),
        ],
        out_shape=[
            sds((bsz, ng, s // TQ, HEAD_DIM, HEADS_PER_GROUP * TQ), BF16),
            sds((bsz, 2 * D_KV // LANES, s, LANES), F32),
            sds((bsz, ng, s, K_ROW), BF16), sds((bsz, ng, s, K_ROW), BF16),
            sds((bsz, ng, s // TQ, V_ROWS, TQ), BF16), sds((bsz, ng, s // TQ, V_ROWS, TQ), BF16),
            sds((bsz, s, 2 * D_RNN), F32),
            sds((bsz, ng, N_GATE, s), F32),
        ],
        compiler_params=_params(("parallel", "arbitrary")),
        name="in_proj",
    )(x, g.reshape(1, d), mod6, mod6, w_perm)


def _compress_kernel(x_ref, pe_ref, w1_ref, w2_ref, o_ref):
    bsz = x_ref.shape[0]
    dh = HEAD_DIM
    n_rows = x_ref.shape[2] // CMP_STRIDE
    rows = bsz * n_rows
    first = jnp.zeros((rows, w1_ref.shape[3]), F32)
    second = jnp.zeros((rows, w1_ref.shape[3]), F32)
    for l in range(CMP_STRIDE):
        x_l = x_ref[:, 0, pl.ds(l, n_rows, stride=CMP_STRIDE), :].reshape(rows, LANES)
        first = first + _dot((x_l + pe_ref[0, l:l + 1, :]).astype(BF16), w1_ref[0, l])
        second = second + _dot(
            (x_l + pe_ref[0, CMP_STRIDE + l:CMP_STRIDE + l + 1, :]).astype(BF16), w1_ref[0, CMP_STRIDE + l])
    hid = first + pltpu.roll(second, rows - 1, axis=0)
    out = _dot(jax.nn.gelu(hid).astype(BF16), w2_ref[0]).astype(BF16)
    for b in range(bsz):
        for j in range(2):
            o_ref[0, b, j] = out[b * n_rows:(b + 1) * n_rows, j * dh:(j + 1) * dh]


def _compress(kvc4, pe, w1, w2):
    bsz, n_tiles, s, _ = kvc4.shape
    n_rows = s // CMP_STRIDE
    per_kind = lambda a: pl.BlockSpec((1,) + a.shape[1:], lambda t, p: (t,) + (0,) * (a.ndim - 1))
    return pl.pallas_call(
        _compress_kernel,
        grid=(2, n_tiles // 2),
        in_specs=[pl.BlockSpec((bsz, 1, s, LANES), lambda t, p: (0, 2 * t + p, 0, 0)),
                  per_kind(pe), per_kind(w1), per_kind(w2)],
        out_specs=pl.BlockSpec((1, bsz, 2, n_rows, HEAD_DIM), lambda t, p: (t, 0, p, 0, 0)),
        out_shape=jax.ShapeDtypeStruct((2, bsz, N_KV_GROUPS, n_rows, HEAD_DIM), BF16),
        compiler_params=_params(("arbitrary", "arbitrary")),
        name="compress_kv",
    )(kvc4, pe, w1, w2)


def _cmp_select_kernel(qt_ref, kc_ref, vc_ref, oc_ref, bias_ref):
    hg = HEADS_PER_GROUP
    kc = kc_ref[0, 0, 0]
    vct = vc_ref[0, 0, 0].astype(F32).T.astype(BF16)
    jj = lax.broadcasted_iota(jnp.int32, (N_SEL, N_CMP_PAD), 0) * SEL_BLOCK
    nn = lax.broadcasted_iota(jnp.int32, (N_SEL, N_CMP_PAD), 1) * CMP_STRIDE
    ov = jnp.minimum(nn + CMP_BLOCK, jj + SEL_BLOCK) - jnp.maximum(nn, jj)
    w_sel = (jnp.maximum(ov, 0).astype(F32) * (1.0 / CMP_BLOCK)).astype(BF16)
    nrow = lax.broadcasted_iota(jnp.int32, (N_CMP_PAD, TQ), 0)
    blk = lax.broadcasted_iota(jnp.int32, (N_SEL, TQ), 0)

    for qi in range(qt_ref.shape[2]):
        qt = qt_ref[0, 0, qi]
        pos = qi * TQ + lax.broadcasted_iota(jnp.int32, (1, TQ), 1)
        mask_c = nrow * CMP_STRIDE + (CMP_BLOCK - 1) <= pos

        def per_head_where(a, fill):
            return jnp.concatenate(
                [jnp.where(mask_c, a[:, h * TQ:(h + 1) * TQ], fill) for h in range(hg)], axis=1)

        s = per_head_where(_dot(kc, qt), NEG)
        e = jnp.exp2(s - jnp.max(s, axis=0, keepdims=True))
        p = per_head_where(e * (1.0 / jnp.sum(e, axis=0, keepdims=True)), 0.0)
        oc_ref[0, 0, qi] = _dot(vct, p.astype(BF16))
        psum = p[:, 0:TQ]
        for h in range(1, hg):
            psum = psum + p[:, h * TQ:(h + 1) * TQ]

        p_hi = psum.astype(BF16)
        p_lo = (psum - p_hi.astype(F32)).astype(BF16)
        imp = _dot(w_sel, p_hi) + _dot(w_sel, p_lo)
        cur = lax.shift_right_logical(pos, SEL_SHIFT)
        free = jnp.where(blk * SEL_BLOCK <= pos, imp, -FORCE_SCORE)
        score = jnp.where(blk == 0, FORCE_SCORE,
                          jnp.where(blk == cur, FORCE_SCORE, jnp.where(blk == cur - 1, FORCE_SCORE, free)))
        rank = jnp.zeros((N_SEL, TQ), F32)
        for k in range(N_SEL):
            sk = score[k:k + 1, :]
            tie = jnp.where(blk > k, 1.0, 0.0)
            rank = rank + jnp.where(sk > score, 1.0, jnp.where(sk == score, tie, 0.0))
        bias_ref[0, 0, qi] = jnp.where(rank < SEL_TOP_N, 0.0, NEG).astype(BF16)


def _cmp_select(qt, kv_cmp):
    bsz, g, nq, dh, wide = qt.shape
    per_bg = lambda *shape: pl.BlockSpec((1, 1) + shape, lambda b, j: (b, j) + (0,) * len(shape))
    kind = lambda t: pl.BlockSpec((1, 1, 1, N_CMP_PAD, dh), lambda b, j: (t, b, j, 0, 0))
    return pl.pallas_call(
        _cmp_select_kernel,
        grid=(bsz, g),
        in_specs=[per_bg(nq, dh, wide), kind(0), kind(1)],
        out_specs=[per_bg(nq, dh, wide), per_bg(nq, N_SEL, TQ)],
        out_shape=[jax.ShapeDtypeStruct((bsz, g, nq, dh, wide), F32),
                   jax.ShapeDtypeStruct((bsz, g, nq, N_SEL, TQ), BF16)],
        compiler_params=_params(("parallel", "arbitrary")),
        name="cmp_select",
    )(qt, kv_cmp, kv_cmp)


def _attn_kernel(qt_ref, oc_ref, bias_ref, ks_ref, kw_ref, vst_ref, vwt_ref, gl_ref, o_ref,
                 m_ref, acc_ref, sc_ref):
    qi = pl.program_id(2)
    last = pl.num_programs(2) - 1
    hg = HEADS_PER_GROUP
    dh = HEAD_DIM
    wide = hg * TQ
    qt = qt_ref[0, 0, 0]

    def per_head_where(mask, a, fill):
        return jnp.concatenate(
            [jnp.where(mask, a[:, h * TQ:(h + 1) * TQ], fill) for h in range(hg)], axis=1)

    def block_bias(value):
        return jnp.full((N_SEL, wide), value, F32).astype(BF16)

    zero_rows = jnp.zeros((K_ROW - dh - N_SEL, wide), BF16)

    def scores(k_ref, kt, bias):
        k_t = k_ref[0, 0, pl.ds(pl.multiple_of(kt * TQ, TQ), TQ), :]
        return _dot(k_t, jnp.concatenate([qt, bias, zero_rows], axis=0))

    rel = (lax.broadcasted_iota(jnp.int32, (TQ, TQ), 0)
           - lax.broadcasted_iota(jnp.int32, (TQ, TQ), 1))

    k1 = jnp.maximum(qi - 1, 0)
    k2 = jnp.maximum(qi - 2, 0)
    def online(state, sc, v_t):
        m_tile = jnp.max(sc, axis=0, keepdims=True)
        if state is None:
            return m_tile, _dot(v_t, jnp.exp2(sc - m_tile).astype(BF16))
        m_old, acc_old = state
        m_new = jnp.maximum(m_old, m_tile)
        return m_new, jnp.exp2(m_old - m_new) * acc_old + _dot(v_t, jnp.exp2(sc - m_new).astype(BF16))

    s0 = per_head_where(rel <= 0, scores(kw_ref, qi, block_bias(0.0)), NEG)
    s1 = scores(kw_ref, k1, block_bias(jnp.where(qi >= 1, 0.0, NEG)))
    s2 = per_head_where(rel > 0, scores(kw_ref, k2, block_bias(jnp.where(qi >= 2, 0.0, NEG))), NEG)
    win = online(None, s0, vwt_ref[0, 0, qi])
    win = online(win, s1, vwt_ref[0, 0, k1])
    win = online(win, s2, vwt_ref[0, 0, k2])
    acc_ref[1] = win[1]

    bias_sel = jnp.concatenate([bias_ref[0, 0, 0]] * hg, axis=1)

    sd = per_head_where(rel <= 0, scores(ks_ref, qi, bias_sel), NEG)
    m_ref[...], acc_ref[0] = online(None, sd, vst_ref[0, 0, qi])

    def sel_scores(kt):
        pad_bias = block_bias(jnp.where(kt < qi, 0.0, NEG))
        return scores(ks_ref, jnp.minimum(kt, last), jnp.minimum(bias_sel, pad_bias))

    def sel_accumulate(sc, kt):
        m_old = m_ref[...]
        m_new = jnp.maximum(m_old, jnp.max(sc, axis=0, keepdims=True))
        m_ref[...] = m_new
        acc_ref[0] = (jnp.exp2(m_old - m_new) * acc_ref[0]
                      + _dot(vst_ref[0, 0, jnp.minimum(kt, last)], jnp.exp2(sc - m_new).astype(BF16)))

    sc_ref[0] = sel_scores(0)

    def sel_pair(pi, carry):
        kt = 2 * pi
        sc_ref[1] = sel_scores(kt + 1)
        sel_accumulate(sc_ref[0], kt)
        sc_ref[0] = sel_scores(kt + 2)
        sel_accumulate(sc_ref[1], kt + 1)
        return carry

    lax.fori_loop(0, (qi + 1) // 2, sel_pair, 0)

    gate = jax.nn.sigmoid(gl_ref[0, 0])
    o_cmp = oc_ref[0, 0, 0]
    acc_s = acc_ref[0]
    acc_w = acc_ref[1]
    o_sel = acc_s[0:dh, :] * (1.0 / acc_s[dh:dh + 1, :])
    o_win = acc_w[0:dh, :] * (1.0 / acc_w[dh:dh + 1, :])
    outs = []
    for h in range(hg):
        lanes = slice(h * TQ, (h + 1) * TQ)
        g0 = gate[3 * h:3 * h + 1, :]
        g1 = gate[3 * h + 1:3 * h + 2, :]
        g2 = gate[3 * h + 2:3 * h + 3, :]
        outs.append(g0 * o_cmp[:, lanes] + g1 * o_sel[:, lanes] + g2 * o_win[:, lanes])
    o_ref[0] = jnp.concatenate(outs, axis=0).T


def _attention(qt, o_cmp, bias, ks, kw, vst, vwt, glt):
    bsz, g, nq = qt.shape[:3]
    s = nq * TQ
    width = HEADS_PER_GROUP * HEAD_DIM
    wide = HEADS_PER_GROUP * TQ
    per_bg = lambda *shape: pl.BlockSpec((1, 1) + shape, lambda b, j, i: (b, j) + (0,) * len(shape))
    per_tile = lambda *shape: pl.BlockSpec((1, 1, 1) + shape, lambda b, j, i: (b, j, i) + (0,) * len(shape))
    return pl.pallas_call(
        _attn_kernel,
        grid=(bsz, g, nq),
        in_specs=[
            per_tile(HEAD_DIM, wide),
            per_tile(HEAD_DIM, wide),
            per_tile(N_SEL, TQ),
            per_bg(s, K_ROW),
            per_bg(s, K_ROW),
            per_bg(nq, V_ROWS, TQ),
            per_bg(nq, V_ROWS, TQ),
            pl.BlockSpec((1, 1, HEADS_PER_GROUP * N_BRANCH, TQ), lambda b, j, i: (b, j, 0, i)),
        ],
        out_specs=pl.BlockSpec((1, TQ, width), lambda b, j, i: (b, i, j)),
        out_shape=jax.ShapeDtypeStruct((bsz, s, D_ATT), F32),
        scratch_shapes=[
            pltpu.VMEM((1, wide), F32),
            pltpu.VMEM((2, V_ROWS, wide), F32),
            pltpu.VMEM((2, TQ, wide), F32),
        ],
        compiler_params=_params(("parallel", "parallel", "arbitrary")),
        name="nsa_attention",
    )(qt, o_cmp, bias, ks, kw, vst, vwt, glt)


def _rglru_kernel(x_ref, y_ref, cw_ref, cb_ref, w_ref, b_ref, lam_ref, o_ref,
                  xpad_ref, a_ref, bt_ref):
    s = x_ref.shape[1]
    c = RNN_BLOCK_DIM
    pad = SUBLANES
    xpad_ref[0:pad, :] = jnp.zeros((pad, c), F32)
    xpad_ref[pad:pad + s, :] = x_ref[0]
    cw = cw_ref[...]
    w = w_ref[0]
    bias = b_ref[0]
    lam = lam_ref[...]
    neg_softplus = -(jnp.maximum(-lam, 0.0) + jnp.log1p(jnp.exp(-jnp.abs(lam))))
    sub = lax.broadcasted_iota(jnp.int32, (TT, c), 0) & (SUBLANES - 1)

    for ci in range(s // TT):
        t0 = ci * TT
        xc = cb_ref[...] + sum(
            xpad_ref[t0 + pad - (CONV_WIDTH - 1) + k:t0 + pad - (CONV_WIDTH - 1) + k + TT, :] * cw[k:k + 1, :]
            for k in range(CONV_WIDTH))
        gates = jax.nn.sigmoid(_dot(xc.astype(BF16), w) + bias)
        r = gates[:, 0:c]
        i = gates[:, c:2 * c]
        log_a = LRU_C * r * neg_softplus
        a = jnp.exp(log_a)
        var = -jnp.tanh(log_a) * (a * a + 1.0)
        bt = jnp.where(var > 0.0, var * lax.rsqrt(var), 0.0) * (i * xc)
        for d in (1, 2, 4):
            keep = sub >= d
            a_prev = jnp.where(keep, pltpu.roll(a, d, axis=0), 1.0)
            b_prev = jnp.where(keep, pltpu.roll(bt, d, axis=0), 0.0)
            bt = bt + a * b_prev
            a = a * a_prev
        a_ref[t0:t0 + TT, :] = a
        bt_ref[t0:t0 + TT, :] = bt

    def group(gi, h):
        r0 = pl.multiple_of(gi * SUBLANES, SUBLANES)
        hg = bt_ref[pl.ds(r0, SUBLANES), :] + a_ref[pl.ds(r0, SUBLANES), :] * h
        bt_ref[pl.ds(r0, SUBLANES), :] = hg
        return hg[SUBLANES - 1:SUBLANES, :]

    lax.fori_loop(0, s // SUBLANES, group, jnp.zeros((1, c), F32), unroll=8)

    for ci in range(s // TT):
        t0 = ci * TT
        o_ref[0, t0:t0 + TT, :] = jax.nn.gelu(y_ref[0, t0:t0 + TT, :]) * bt_ref[t0:t0 + TT, :]


def _rglru(r_in, conv_w, conv_b, w_cat, b_cat, lam):
    bsz, s, _ = r_in.shape
    c = RNN_BLOCK_DIM
    nb = RNN_BLOCKS
    return pl.pallas_call(
        _rglru_kernel,
        grid=(bsz, nb),
        in_specs=[
            pl.BlockSpec((1, s, c), lambda b, j: (b, 0, j)),
            pl.BlockSpec((1, s, c), lambda b, j: (b, 0, nb + j)),
            pl.BlockSpec((CONV_WIDTH, c), lambda b, j: (0, j)),
            pl.BlockSpec((1, c), lambda b, j: (0, j)),
            pl.BlockSpec((1, c, 2 * c), lambda b, j: (j, 0, 0)),
            pl.BlockSpec((1, 1, 2 * c), lambda b, j: (j, 0, 0)),
            pl.BlockSpec((1, c), lambda b, j: (0, j)),
        ],
        out_specs=pl.BlockSpec((1, s, c), lambda b, j: (b, 0, j)),
        out_shape=jax.ShapeDtypeStruct((bsz, s, D_RNN), F32),
        scratch_shapes=[
            pltpu.VMEM((s + SUBLANES, c), F32),
            pltpu.VMEM((s, c), F32),
            pltpu.VMEM((s, c), F32),
        ],
        compiler_params=_params(("parallel", "arbitrary")),
        name="rg_lru",
    )(r_in, r_in, conv_w, conv_b, w_cat, b_cat, lam)


def _outproj_kernel(oa_ref, or_ref, ga_ref, gr_ref, w_ref, x_ref, gt_ref, gp_ref, o_ref):
    a = _rms(oa_ref[0], ga_ref[...]).astype(BF16)
    r = _rms(or_ref[0], gr_ref[...]).astype(BF16)
    mix = _dot(a, w_ref[0:D_ATT, :]) + _dot(r, w_ref[D_ATT:D_ATT + D_RNN, :])
    o_ref[0] = x_ref[0] + gt_ref[0, 0] * _rms(mix, gp_ref[...])


def _outproj(o_att, o_rnn, g_att, g_rnn, w_out, x, mod6, g_post, tm=512):
    bsz, s, d = x.shape
    row = lambda width: pl.BlockSpec((1, tm, width), lambda b, i: (b, i, 0))
    vec = lambda width: pl.BlockSpec((1, width), lambda b, i: (0, 0))
    return pl.pallas_call(
        _outproj_kernel,
        grid=(bsz, s // tm),
        in_specs=[
            row(D_ATT), row(D_RNN), vec(D_ATT), vec(D_RNN),
            pl.BlockSpec((D_ATT + D_RNN, d), lambda b, i: (0, 0), pipeline_mode=pl.Buffered(1)),
            row(d),
            pl.BlockSpec((1, 1, 1, d), lambda b, i: (2, b, 0, 0)),
            vec(d),
        ],
        out_specs=row(d),
        out_shape=jax.ShapeDtypeStruct((bsz, s, d), F32),
        compiler_params=_params(("parallel", "arbitrary")),
        name="out_proj",
    )(o_att, o_rnn, g_att.reshape(1, -1), g_rnn.reshape(1, -1), w_out, x, mod6, g_post.reshape(1, d))


def _mlp_kernel(x_ref, g_ref, sc_ref, sh_ref, w1_ref, w2_ref, gt_ref, gp_ref, o_ref, h_ref, a_ref):
    j = pl.program_id(2)
    n_chunks = pl.num_programs(2) - 1
    slot = j % 2

    def up():
        u = jnp.maximum(_dot(h_ref[...], w1_ref[...]), 0.0)
        a_ref[slot] = (u * u).astype(BF16)

    def down():
        o_ref[0] += _dot(a_ref[1 - slot], w2_ref[...])

    @pl.when(j == 0)
    def _():
        h = _rms(x_ref[0], g_ref[...]) * (1.0 + sc_ref[0, 0]) + sh_ref[0, 0]
        h_ref[...] = h.astype(BF16)
        o_ref[0] = jnp.zeros(o_ref.shape[1:], F32)
        up()

    @pl.when((j > 0) & (j < n_chunks))
    def _():
        down()
        up()

    @pl.when(j == n_chunks)
    def _():
        down()
        o_ref[0] = x_ref[0] + gt_ref[0, 0] * _rms(o_ref[0], gp_ref[...])


def _mlp(x, g_pre, mod6, w1, w2, g_post, tm=1024, tf=1024):
    bsz, s, d = x.shape
    n_chunks = w1.shape[1] // tf
    row = lambda **kw: pl.BlockSpec((1, tm, d), lambda b, i, j: (b, i, 0), **kw)
    vec = pl.BlockSpec((1, d), lambda b, i, j: (0, 0))
    modk = lambda k: pl.BlockSpec((1, 1, 1, d), lambda b, i, j: (k, b, 0, 0))
    return pl.pallas_call(
        _mlp_kernel,
        grid=(bsz, s // tm, n_chunks + 1),
        in_specs=[
            row(pipeline_mode=pl.Buffered(1)),
            vec, modk(4), modk(3),
            pl.BlockSpec((d, tf), lambda b, i, j: (0, jnp.minimum(j, n_chunks - 1))),
            pl.BlockSpec((tf, d), lambda b, i, j: (jnp.maximum(j - 1, 0), 0)),
            modk(5), vec,
        ],
        out_specs=row(),
        out_shape=jax.ShapeDtypeStruct((bsz, s, d), F32),
        scratch_shapes=[pltpu.VMEM((tm, d), BF16), pltpu.VMEM((2, tm, tf), BF16)],
        compiler_params=_params(("parallel", "parallel", "arbitrary")),
        name="mlp",
    )(x, g_pre.reshape(1, d), mod6, mod6, w1, w2, mod6, g_post.reshape(1, d))


def _layer(x, c, w_ada, b_ada, g_pre_mix, g_post_mix, g_pre_mlp, g_post_mlp, w_in,
           cmp_w1_k, cmp_w2_k, cmp_pe_k, cmp_w1_v, cmp_w2_v, cmp_pe_v,
           conv_w, conv_b, w_rg_a, b_rg_a, w_rg_x, b_rg_x, lru_lambda,
           g_grp_att, g_grp_rnn, w_out, w_ff1, w_ff2):
    bsz, s, d = x.shape
    dh = HEAD_DIM

    mod = _ada(c, w_ada, b_ada)
    mod6 = mod.reshape(bsz, 6, 1, d).transpose(1, 0, 2, 3)

    qt, kvc, ks, kw, vst, vwt, r_in, glt = _inproj(x, g_pre_mix, mod6, _w_in_prep(w_in))

    def pair_diag(w, axis):
        z = jnp.zeros_like(w)
        return jnp.concatenate([jnp.concatenate([w, z], axis=-1), jnp.concatenate([z, w], axis=-1)], axis=axis)

    pe = jnp.tile(jnp.stack([cmp_pe_k, cmp_pe_v]), (1, 1, 2))
    w1c = pair_diag(jnp.stack([cmp_w1_k, cmp_w1_v]).astype(BF16).reshape(2, CMP_BLOCK, dh, -1), 2)
    w2c = pair_diag(jnp.stack([cmp_w2_k, cmp_w2_v]).astype(BF16), 1)
    kv_cmp = _compress(kvc, pe, w1c, w2c)

    o_cmp, sel_bias = _cmp_select(qt, kv_cmp)
    o_att = _attention(qt, o_cmp, sel_bias, ks, kw, vst, vwt, glt)

    w_cat = jnp.concatenate([w_rg_a, w_rg_x], axis=-1).astype(BF16)
    b_cat = jnp.concatenate([b_rg_a.reshape(RNN_BLOCKS, 1, RNN_BLOCK_DIM),
                             b_rg_x.reshape(RNN_BLOCKS, 1, RNN_BLOCK_DIM)], axis=-1)
    o_rnn = _rglru(r_in, conv_w, conv_b.reshape(1, -1), w_cat, b_cat, lru_lambda.reshape(1, -1))

    x1 = _outproj(o_att, o_rnn, g_grp_att, g_grp_rnn, w_out.astype(BF16), x, mod6, g_post_mix)
    return _mlp(x1, g_pre_mlp, mod6, w_ff1.astype(BF16), w_ff2.astype(BF16), g_post_mlp)


def kernel(x, c, w_ada, b_ada, g_pre_mix, g_post_mix, g_pre_mlp, g_post_mlp, w_in, cmp_w1_k, cmp_w2_k, cmp_pe_k, cmp_w1_v, cmp_w2_v, cmp_pe_v, conv_w, conv_b, w_rg_a, b_rg_a, w_rg_x, b_rg_x, lru_lambda, g_grp_att, g_grp_rnn, w_out, w_ff1, w_ff2):
    depth = w_ada.shape[0]
    for l in range(depth):
        x = _layer(x, c, w_ada[l], b_ada[l], g_pre_mix[l], g_post_mix[l], g_pre_mlp[l], g_post_mlp[l],
                   w_in[l], cmp_w1_k[l], cmp_w2_k[l], cmp_pe_k[l], cmp_w1_v[l], cmp_w2_v[l], cmp_pe_v[l],
                   conv_w[l], conv_b[l], w_rg_a[l], b_rg_a[l], w_rg_x[l], b_rg_x[l], lru_lambda[l],
                   g_grp_att[l], g_grp_rnn[l], w_out[l], w_ff1[l], w_ff2[l])
    return x
```

```python
import functools

import jax
import jax.numpy as jnp
from jax import lax
from jax.experimental import pallas as pl
from jax.experimental.pallas import tpu as pltpu

F32 = jnp.float32
BF16 = jnp.bfloat16

D_MODEL = 2048
D_ATT = 1024
D_RNN = 1024
N_Q_HEADS = 16
N_KV_GROUPS = 4
HEADS_PER_GROUP = 4
HEAD_DIM = 64
D_KV = 256
CMP_BLOCK = 32
CMP_STRIDE = 16
SEL_BLOCK = 64
SEL_SHIFT = 6
SEL_TOP_N = 8
WINDOW = 512
N_BRANCH = 3
RNN_BLOCKS = 8
RNN_BLOCK_DIM = 128
CONV_WIDTH = 4
LRU_C = 8.0
D_FF = 4 * D_MODEL
EPS = 1e-6
NEG = -1e30
FORCE_SCORE = 1e9
LOG2_E = 1.4426950408889634

LANES = 128
SUBLANES = 8
VMEM_LIMIT = 60 * 1024 * 1024

TQ = 256
N_CMP_PAD = 128
N_SEL = 32
GL_PAD = 128
TT = 256


def _params(sem):
    return pltpu.CompilerParams(dimension_semantics=sem, vmem_limit_bytes=VMEM_LIMIT)


def _dot(a, b):
    return jnp.dot(a, b, preferred_element_type=F32)


def _rms(x, g):
    return x * lax.rsqrt(jnp.mean(x * x, axis=-1, keepdims=True) + EPS) * g


def _ada_kernel(c_ref, w_ref, b_ref, o_ref):
    c = c_ref[...]
    ca = (c * jax.nn.sigmoid(c)).astype(BF16)
    o_ref[...] = _dot(ca, w_ref[...].astype(BF16)) + b_ref[...]


def _ada(c, w, b):
    bsz, d = c.shape
    n = w.shape[1]
    tn = 1024
    return pl.pallas_call(
        _ada_kernel,
        grid=(n // tn,),
        in_specs=[
            pl.BlockSpec((bsz, d), lambda j: (0, 0)),
            pl.BlockSpec((d, tn), lambda j: (0, j)),
            pl.BlockSpec((1, tn), lambda j: (0, j)),
        ],
        out_specs=pl.BlockSpec((bsz, tn), lambda j: (0, j)),
        out_shape=jax.ShapeDtypeStruct((bsz, n), F32),
        compiler_params=_params(("arbitrary",)),
        name="ada_mod",
    )(c, w, b.reshape(1, n))


C_Q = (0, D_ATT)
C_CMP = (C_Q[1], C_Q[1] + 2 * D_KV)
C_K = (C_CMP[1], C_CMP[1] + 2 * D_KV)
C_V = (C_K[1], C_K[1] + 2 * D_KV)
C_R = (C_V[1], C_V[1] + 2 * D_RNN)
C_GL = (C_R[1], C_R[1] + GL_PAD)
D_IN_PAD = C_GL[1]
K_ROW = LANES
V_ROWS = HEAD_DIM + 16
N_GATE = HEADS_PER_GROUP * N_BRANCH


def _w_in_prep_kernel(w_ref, o_ref):
    cast = lambda lo, hi: w_ref[0, :, lo:hi].astype(BF16)
    kv0 = D_ATT
    gl0 = D_ATT + 6 * D_KV
    n_gl = N_BRANCH * N_Q_HEADS
    o_ref[:, 0:C_K[0] + D_KV] = cast(0, kv0 + 3 * D_KV)
    o_ref[:, C_K[0] + D_KV:C_K[1]] = cast(kv0 + 4 * D_KV, kv0 + 5 * D_KV)
    o_ref[:, C_V[0]:C_V[0] + D_KV] = cast(kv0 + 3 * D_KV, kv0 + 4 * D_KV)
    o_ref[:, C_V[0] + D_KV:C_V[1]] = cast(kv0 + 5 * D_KV, kv0 + 6 * D_KV)
    o_ref[:, C_R[0]:C_R[1]] = cast(gl0 + n_gl, gl0 + n_gl + 2 * D_RNN)
    o_ref[:, C_GL[0]:C_GL[1]] = jnp.concatenate(
        [cast(gl0, gl0 + n_gl), jnp.zeros((w_ref.shape[1], GL_PAD - n_gl), BF16)], axis=1)


def _w_in_prep(w_in_all, layer, tk=256):
    _, d, n = w_in_all.shape
    return pl.pallas_call(
        _w_in_prep_kernel,
        grid=(d // tk,),
        in_specs=[pl.BlockSpec((1, tk, n), lambda i: (layer, i, 0))],
        out_specs=pl.BlockSpec((tk, D_IN_PAD), lambda i: (i, 0)),
        out_shape=jax.ShapeDtypeStruct((d, D_IN_PAD), BF16),
        compiler_params=_params(("arbitrary",)),
        name="w_in_prep",
    )(w_in_all)


def _inproj_kernel(x_ref, g_ref, sc_ref, sh_ref, w_ref,
                   qt_ref, cmp_ref, ks_ref, kw_ref, vst_ref, vwt_ref, r_ref, glt_ref):
    tm = x_ref.shape[1]
    ng, hg, dh = N_KV_GROUPS, HEADS_PER_GROUP, HEAD_DIM
    x = x_ref[0]
    h = _rms(x, g_ref[...]) * (1.0 + sc_ref[0, 0]) + sh_ref[0, 0]
    hb = h.astype(BF16)
    tiles = [slice(u * TQ, (u + 1) * TQ) for u in range(tm // TQ)]

    q_t = (_dot(hb, w_ref[:, C_Q[0]:C_Q[1]]) * (dh ** -0.5 * LOG2_E)).T.astype(BF16)
    for j in range(ng):
        for u, cols in enumerate(tiles):
            qt_ref[0, j, u] = jnp.concatenate(
                [q_t[(hg * j + hh) * dh:(hg * j + hh + 1) * dh, cols] for hh in range(hg)], axis=1)

    kv_cmp = _dot(hb, w_ref[:, C_CMP[0]:C_CMP[1]])
    for c in range(2 * D_KV // LANES):
        cmp_ref[0, c] = kv_cmp[:, c * LANES:(c + 1) * LANES]

    keys = _dot(hb, w_ref[:, C_K[0]:C_K[1]])
    row_pos = pl.program_id(1) * tm + lax.broadcasted_iota(jnp.int32, (tm, K_ROW - dh), 0)
    lane = lax.broadcasted_iota(jnp.int32, (tm, K_ROW - dh), 1)
    onehot = jnp.where(lax.shift_right_logical(row_pos, SEL_SHIFT) == lane, 1.0, 0.0)
    for j in range(ng):
        ks_ref[0, j] = jnp.concatenate([keys[:, j * dh:(j + 1) * dh], onehot], axis=1).astype(BF16)
        kw_ref[0, j] = jnp.concatenate(
            [keys[:, D_KV + j * dh:D_KV + (j + 1) * dh], onehot], axis=1).astype(BF16)

    v_t = _dot(hb, w_ref[:, C_V[0]:C_V[1]]).T.astype(BF16)
    ones = jnp.ones((V_ROWS - dh, TQ), BF16)
    for j in range(ng):
        for u, cols in enumerate(tiles):
            vst_ref[0, j, u] = jnp.concatenate([v_t[j * dh:(j + 1) * dh, cols], ones], axis=0)
            vwt_ref[0, j, u] = jnp.concatenate([v_t[D_KV + j * dh:D_KV + (j + 1) * dh, cols], ones], axis=0)

    half = (C_R[0] + C_R[1]) // 2
    r_ref[0, :, 0:D_RNN] = _dot(hb, w_ref[:, C_R[0]:half])
    r_ref[0, :, D_RNN:2 * D_RNN] = _dot(hb, w_ref[:, half:C_R[1]])

    gl_t = _dot(hb, w_ref[:, C_GL[0]:C_GL[1]]).T
    for j in range(ng):
        glt_ref[0, j] = gl_t[N_GATE * j:N_GATE * (j + 1), :]


def _inproj(x, g, mod6, w_perm, tm=512):
    bsz, s, d = x.shape
    ng = N_KV_GROUPS
    nt = tm // TQ
    row = lambda width: pl.BlockSpec((1, tm, width), lambda b, i: (b, i, 0))
    grp_rows = pl.BlockSpec((1, ng, tm, K_ROW), lambda b, i: (b, 0, i, 0))
    grp_tiles = lambda r, c: pl.BlockSpec((1, ng, nt, r, c), lambda b, i: (b, 0, i, 0, 0))
    sds = jax.ShapeDtypeStruct
    return pl.pallas_call(
        _inproj_kernel,
        grid=(bsz, s // tm),
        in_specs=[
            row(d),
            pl.BlockSpec((1, d), lambda b, i: (0, 0)),
            pl.BlockSpec((1, 1, 1, d), lambda b, i: (1, b, 0, 0)),
            pl.BlockSpec((1, 1, 1, d), lambda b, i: (0, b, 0, 0)),
            pl.BlockSpec((d, D_IN_PAD), lambda b, i: (0, 0), pipeline_mode=pl.Buffered(1)),
        ],
        out_specs=[
            grp_tiles(HEAD_DIM, HEADS_PER_GROUP * TQ),
            pl.BlockSpec((1, 2 * D_KV // LANES, tm, LANES), lambda b, i: (b, 0, i, 0)),
            grp_rows, grp_rows,
            grp_tiles(V_ROWS, TQ), grp_tiles(V_ROWS, TQ), row(2 * D_RNN),
            pl.BlockSpec((1, ng, N_GATE, tm), lambda b, i: (b, 0, 0, i)),
        ],
        out_shape=[
            sds((bsz, ng, s // TQ, HEAD_DIM, HEADS_PER_GROUP * TQ), BF16),
            sds((bsz, 2 * D_KV // LANES, s, LANES), F32),
            sds((bsz, ng, s, K_ROW), BF16), sds((bsz, ng, s, K_ROW), BF16),
            sds((bsz, ng, s // TQ, V_ROWS, TQ), BF16), sds((bsz, ng, s // TQ, V_ROWS, TQ), BF16),
            sds((bsz, s, 2 * D_RNN), F32),
            sds((bsz, ng, N_GATE, s), F32),
        ],
        compiler_params=_params(("parallel", "arbitrary")),
        name="in_proj",
    )(x, g.reshape(1, d), mod6, mod6, w_perm)


def _compress_kernel(x_ref, pe_ref, w1_ref, w2_ref, o_ref):
    bsz = x_ref.shape[0]
    dh = HEAD_DIM
    n_rows = x_ref.shape[2] // CMP_STRIDE
    rows = bsz * n_rows
    first = jnp.zeros((rows, w1_ref.shape[3]), F32)
    second = jnp.zeros((rows, w1_ref.shape[3]), F32)
    for l in range(CMP_STRIDE):
        x_l = x_ref[:, 0, pl.ds(l, n_rows, stride=CMP_STRIDE), :].reshape(rows, LANES)
        first = first + _dot((x_l + pe_ref[0, l:l + 1, :]).astype(BF16), w1_ref[0, l])
        second = second + _dot(
            (x_l + pe_ref[0, CMP_STRIDE + l:CMP_STRIDE + l + 1, :]).astype(BF16), w1_ref[0, CMP_STRIDE + l])
    hid = first + pltpu.roll(second, rows - 1, axis=0)
    out = _dot(jax.nn.gelu(hid).astype(BF16), w2_ref[0]).astype(BF16)
    for b in range(bsz):
        for j in range(2):
            o_ref[0, b, j] = out[b * n_rows:(b + 1) * n_rows, j * dh:(j + 1) * dh]


def _compress(kvc4, pe, w1, w2):
    bsz, n_tiles, s, _ = kvc4.shape
    n_rows = s // CMP_STRIDE
    per_kind = lambda a: pl.BlockSpec((1,) + a.shape[1:], lambda t, p: (t,) + (0,) * (a.ndim - 1))
    return pl.pallas_call(
        _compress_kernel,
        grid=(2, n_tiles // 2),
        in_specs=[pl.BlockSpec((bsz, 1, s, LANES), lambda t, p: (0, 2 * t + p, 0, 0)),
                  per_kind(pe), per_kind(w1), per_kind(w2)],
        out_specs=pl.BlockSpec((1, bsz, 2, n_rows, HEAD_DIM), lambda t, p: (t, 0, p, 0, 0)),
        out_shape=jax.ShapeDtypeStruct((2, bsz, N_KV_GROUPS, n_rows, HEAD_DIM), BF16),
        compiler_params=_params(("arbitrary", "arbitrary")),
        name="compress_kv",
    )(kvc4, pe, w1, w2)


def _cmp_select_kernel(qt_ref, kc_ref, vc_ref, oc_ref, bias_ref):
    hg = HEADS_PER_GROUP
    kc = kc_ref[0, 0, 0]
    vct = vc_ref[0, 0, 0].astype(F32).T.astype(BF16)
    jj = lax.broadcasted_iota(jnp.int32, (N_SEL, N_CMP_PAD), 0) * SEL_BLOCK
    nn = lax.broadcasted_iota(jnp.int32, (N_SEL, N_CMP_PAD), 1) * CMP_STRIDE
    ov = jnp.minimum(nn + CMP_BLOCK, jj + SEL_BLOCK) - jnp.maximum(nn, jj)
    w_sel = (jnp.maximum(ov, 0).astype(F32) * (1.0 / CMP_BLOCK)).astype(BF16)
    nrow = lax.broadcasted_iota(jnp.int32, (N_CMP_PAD, TQ), 0)
    blk = lax.broadcasted_iota(jnp.int32, (N_SEL, TQ), 0)

    for qi in range(qt_ref.shape[2]):
        qt = qt_ref[0, 0, qi]
        pos = qi * TQ + lax.broadcasted_iota(jnp.int32, (1, TQ), 1)
        mask_c = nrow * CMP_STRIDE + (CMP_BLOCK - 1) <= pos

        def per_head_where(a, fill):
            return jnp.concatenate(
                [jnp.where(mask_c, a[:, h * TQ:(h + 1) * TQ], fill) for h in range(hg)], axis=1)

        s = per_head_where(_dot(kc, qt), NEG)
        e = jnp.exp2(s - jnp.max(s, axis=0, keepdims=True))
        p = per_head_where(e * (1.0 / jnp.sum(e, axis=0, keepdims=True)), 0.0)
        oc_ref[0, 0, qi] = _dot(vct, p.astype(BF16))
        psum = p[:, 0:TQ]
        for h in range(1, hg):
            psum = psum + p[:, h * TQ:(h + 1) * TQ]

        p_hi = psum.astype(BF16)
        p_lo = (psum - p_hi.astype(F32)).astype(BF16)
        imp = _dot(w_sel, p_hi) + _dot(w_sel, p_lo)
        cur = lax.shift_right_logical(pos, SEL_SHIFT)
        free = jnp.where(blk * SEL_BLOCK <= pos, imp, -FORCE_SCORE)
        score = jnp.where(blk == 0, FORCE_SCORE,
                          jnp.where(blk == cur, FORCE_SCORE, jnp.where(blk == cur - 1, FORCE_SCORE, free)))
        rank = jnp.zeros((N_SEL, TQ), F32)
        for k in range(N_SEL):
            sk = score[k:k + 1, :]
            tie = jnp.where(blk > k, 1.0, 0.0)
            rank = rank + jnp.where(sk > score, 1.0, jnp.where(sk == score, tie, 0.0))
        bias_ref[0, 0, qi] = jnp.where(rank < SEL_TOP_N, 0.0, NEG).astype(BF16)


def _cmp_select(qt, kv_cmp):
    bsz, g, nq, dh, wide = qt.shape
    per_bg = lambda *shape: pl.BlockSpec((1, 1) + shape, lambda b, j: (b, j) + (0,) * len(shape))
    kind = lambda t: pl.BlockSpec((1, 1, 1, N_CMP_PAD, dh), lambda b, j: (t, b, j, 0, 0))
    return pl.pallas_call(
        _cmp_select_kernel,
        grid=(bsz, g),
        in_specs=[per_bg(nq, dh, wide), kind(0), kind(1)],
        out_specs=[per_bg(nq, dh, wide), per_bg(nq, N_SEL, TQ)],
        out_shape=[jax.ShapeDtypeStruct((bsz, g, nq, dh, wide), F32),
                   jax.ShapeDtypeStruct((bsz, g, nq, N_SEL, TQ), BF16)],
        compiler_params=_params(("parallel", "arbitrary")),
        name="cmp_select",
    )(qt, kv_cmp, kv_cmp)


def _attn_kernel(qt_ref, oc_ref, bias_ref, ks_ref, kw_ref, vst_ref, vwt_ref, gl_ref, o_ref,
                 m_ref, acc_ref, sc_ref):
    qi = pl.program_id(2)
    last = pl.num_programs(2) - 1
    hg = HEADS_PER_GROUP
    dh = HEAD_DIM
    wide = hg * TQ
    qt = qt_ref[0, 0, 0]

    def per_head_where(mask, a, fill):
        return jnp.concatenate(
            [jnp.where(mask, a[:, h * TQ:(h + 1) * TQ], fill) for h in range(hg)], axis=1)

    def block_bias(value):
        return jnp.full((N_SEL, wide), value, F32).astype(BF16)

    zero_rows = jnp.zeros((K_ROW - dh - N_SEL, wide), BF16)

    def scores(k_ref, kt, bias):
        k_t = k_ref[0, 0, pl.ds(pl.multiple_of(kt * TQ, TQ), TQ), :]
        return _dot(k_t, jnp.concatenate([qt, bias, zero_rows], axis=0))

    rel = (lax.broadcasted_iota(jnp.int32, (TQ, TQ), 0)
           - lax.broadcasted_iota(jnp.int32, (TQ, TQ), 1))

    def online(state, sc, v_t):
        m_tile = jnp.max(sc, axis=0, keepdims=True)
        if state is None:
            return m_tile, _dot(v_t, jnp.exp2(sc - m_tile).astype(BF16))
        m_old, acc_old = state
        m_new = jnp.maximum(m_old, m_tile)
        return m_new, jnp.exp2(m_old - m_new) * acc_old + _dot(v_t, jnp.exp2(sc - m_new).astype(BF16))

    k1 = jnp.maximum(qi - 1, 0)
    k2 = jnp.maximum(qi - 2, 0)
    s0 = per_head_where(rel <= 0, scores(kw_ref, qi, block_bias(0.0)), NEG)
    s1 = scores(kw_ref, k1, block_bias(jnp.where(qi >= 1, 0.0, NEG)))
    s2 = per_head_where(rel > 0, scores(kw_ref, k2, block_bias(jnp.where(qi >= 2, 0.0, NEG))), NEG)
    win = online(None, s0, vwt_ref[0, 0, qi])
    win = online(win, s1, vwt_ref[0, 0, k1])
    win = online(win, s2, vwt_ref[0, 0, k2])
    acc_ref[1] = win[1]

    bias_sel = jnp.concatenate([bias_ref[0, 0, 0]] * hg, axis=1)

    sd = per_head_where(rel <= 0, scores(ks_ref, qi, bias_sel), NEG)
    m_ref[...], acc_ref[0] = online(None, sd, vst_ref[0, 0, qi])

    def sel_scores(kt):
        pad_bias = block_bias(jnp.where(kt < qi, 0.0, NEG))
        return scores(ks_ref, jnp.minimum(kt, last), jnp.minimum(bias_sel, pad_bias))

    def sel_accumulate(sc, kt):
        m_ref[...], acc_ref[0] = online((m_ref[...], acc_ref[0]), sc, vst_ref[0, 0, jnp.minimum(kt, last)])

    sc_ref[0] = sel_scores(0)

    def sel_pair(pi, carry):
        kt = 2 * pi
        sc_ref[1] = sel_scores(kt + 1)
        sel_accumulate(sc_ref[0], kt)
        sc_ref[0] = sel_scores(kt + 2)
        sel_accumulate(sc_ref[1], kt + 1)
        return carry

    lax.fori_loop(0, (qi + 1) // 2, sel_pair, 0)

    gate = jax.nn.sigmoid(gl_ref[0, 0])
    o_cmp = oc_ref[0, 0, 0]
    acc_s = acc_ref[0]
    acc_w = acc_ref[1]
    o_sel = acc_s[0:dh, :] * (1.0 / acc_s[dh:dh + 1, :])
    o_win = acc_w[0:dh, :] * (1.0 / acc_w[dh:dh + 1, :])
    outs = []
    for h in range(hg):
        lanes = slice(h * TQ, (h + 1) * TQ)
        g0 = gate[3 * h:3 * h + 1, :]
        g1 = gate[3 * h + 1:3 * h + 2, :]
        g2 = gate[3 * h + 2:3 * h + 3, :]
        outs.append(g0 * o_cmp[:, lanes] + g1 * o_sel[:, lanes] + g2 * o_win[:, lanes])
    o_ref[0] = jnp.concatenate(outs, axis=0).T


def _attention(qt, o_cmp, bias, ks, kw, vst, vwt, glt):
    bsz, g, nq = qt.shape[:3]
    s = nq * TQ
    width = HEADS_PER_GROUP * HEAD_DIM
    wide = HEADS_PER_GROUP * TQ
    per_bg = lambda *shape: pl.BlockSpec((1, 1) + shape, lambda b, j, i: (b, j) + (0,) * len(shape))
    per_tile = lambda *shape: pl.BlockSpec((1, 1, 1) + shape, lambda b, j, i: (b, j, i) + (0,) * len(shape))
    return pl.pallas_call(
        _attn_kernel,
        grid=(bsz, g, nq),
        in_specs=[
            per_tile(HEAD_DIM, wide),
            per_tile(HEAD_DIM, wide),
            per_tile(N_SEL, TQ),
            per_bg(s, K_ROW),
            per_bg(s, K_ROW),
            per_bg(nq, V_ROWS, TQ),
            per_bg(nq, V_ROWS, TQ),
            pl.BlockSpec((1, 1, HEADS_PER_GROUP * N_BRANCH, TQ), lambda b, j, i: (b, j, 0, i)),
        ],
        out_specs=pl.BlockSpec((1, TQ, width), lambda b, j, i: (b, i, j)),
        out_shape=jax.ShapeDtypeStruct((bsz, s, D_ATT), F32),
        scratch_shapes=[
            pltpu.VMEM((1, wide), F32),
            pltpu.VMEM((2, V_ROWS, wide), F32),
            pltpu.VMEM((2, TQ, wide), F32),
        ],
        compiler_params=_params(("parallel", "parallel", "arbitrary")),
        name="nsa_attention",
    )(qt, o_cmp, bias, ks, kw, vst, vwt, glt)


def _rglru_kernel(x_ref, y_ref, cw_ref, cb_ref, w_ref, b_ref, lam_ref, o_ref,
                  xpad_ref, a_ref, bt_ref):
    s = x_ref.shape[1]
    c = RNN_BLOCK_DIM
    pad = SUBLANES
    xpad_ref[0:pad, :] = jnp.zeros((pad, c), F32)
    xpad_ref[pad:pad + s, :] = x_ref[0]
    cw = cw_ref[...]
    w = w_ref[0]
    bias = b_ref[0]
    lam = lam_ref[...]
    neg_softplus = -(jnp.maximum(-lam, 0.0) + jnp.log1p(jnp.exp(-jnp.abs(lam))))
    sub = lax.broadcasted_iota(jnp.int32, (TT, c), 0) & (SUBLANES - 1)

    for ci in range(s // TT):
        t0 = ci * TT
        xc = cb_ref[...] + sum(
            xpad_ref[t0 + pad - (CONV_WIDTH - 1) + k:t0 + pad - (CONV_WIDTH - 1) + k + TT, :] * cw[k:k + 1, :]
            for k in range(CONV_WIDTH))
        gates = jax.nn.sigmoid(_dot(xc.astype(BF16), w) + bias)
        r = gates[:, 0:c]
        i = gates[:, c:2 * c]
        log_a = LRU_C * r * neg_softplus
        a = jnp.exp(log_a)
        var = -jnp.tanh(log_a) * (a * a + 1.0)
        bt = jnp.where(var > 0.0, var * lax.rsqrt(var), 0.0) * (i * xc)
        for d in (1, 2, 4):
            keep = sub >= d
            a_prev = jnp.where(keep, pltpu.roll(a, d, axis=0), 1.0)
            b_prev = jnp.where(keep, pltpu.roll(bt, d, axis=0), 0.0)
            bt = bt + a * b_prev
            a = a * a_prev
        a_ref[t0:t0 + TT, :] = a
        bt_ref[t0:t0 + TT, :] = bt

    def group(gi, h):
        r0 = pl.multiple_of(gi * SUBLANES, SUBLANES)
        hg = bt_ref[pl.ds(r0, SUBLANES), :] + a_ref[pl.ds(r0, SUBLANES), :] * h
        bt_ref[pl.ds(r0, SUBLANES), :] = hg
        return hg[SUBLANES - 1:SUBLANES, :]

    lax.fori_loop(0, s // SUBLANES, group, jnp.zeros((1, c), F32), unroll=8)

    for ci in range(s // TT):
        t0 = ci * TT
        o_ref[0, t0:t0 + TT, :] = jax.nn.gelu(y_ref[0, t0:t0 + TT, :]) * bt_ref[t0:t0 + TT, :]


def _rglru(r_in, conv_w, conv_b, w_cat, b_cat, lam):
    bsz, s, _ = r_in.shape
    c = RNN_BLOCK_DIM
    nb = RNN_BLOCKS
    return pl.pallas_call(
        _rglru_kernel,
        grid=(bsz, nb),
        in_specs=[
            pl.BlockSpec((1, s, c), lambda b, j: (b, 0, j)),
            pl.BlockSpec((1, s, c), lambda b, j: (b, 0, nb + j)),
            pl.BlockSpec((CONV_WIDTH, c), lambda b, j: (0, j)),
            pl.BlockSpec((1, c), lambda b, j: (0, j)),
            pl.BlockSpec((1, c, 2 * c), lambda b, j: (j, 0, 0)),
            pl.BlockSpec((1, 1, 2 * c), lambda b, j: (j, 0, 0)),
            pl.BlockSpec((1, c), lambda b, j: (0, j)),
        ],
        out_specs=pl.BlockSpec((1, s, c), lambda b, j: (b, 0, j)),
        out_shape=jax.ShapeDtypeStruct((bsz, s, D_RNN), F32),
        scratch_shapes=[
            pltpu.VMEM((s + SUBLANES, c), F32),
            pltpu.VMEM((s, c), F32),
            pltpu.VMEM((s, c), F32),
        ],
        compiler_params=_params(("parallel", "arbitrary")),
        name="rg_lru",
    )(r_in, r_in, conv_w, conv_b, w_cat, b_cat, lam)


def _outproj_kernel(oa_ref, or_ref, ga_ref, gr_ref, w_ref, x_ref, gt_ref, gp_ref, o_ref):
    a = _rms(oa_ref[0], ga_ref[...]).astype(BF16)
    r = _rms(or_ref[0], gr_ref[...]).astype(BF16)
    mix = _dot(a, w_ref[0:D_ATT, :]) + _dot(r, w_ref[D_ATT:D_ATT + D_RNN, :])
    o_ref[0] = x_ref[0] + gt_ref[0, 0] * _rms(mix, gp_ref[...])


def _outproj(o_att, o_rnn, g_att, g_rnn, w_out, x, mod6, g_post, tm=512):
    bsz, s, d = x.shape
    row = lambda width: pl.BlockSpec((1, tm, width), lambda b, i: (b, i, 0))
    vec = lambda width: pl.BlockSpec((1, width), lambda b, i: (0, 0))
    return pl.pallas_call(
        _outproj_kernel,
        grid=(bsz, s // tm),
        in_specs=[
            row(D_ATT), row(D_RNN), vec(D_ATT), vec(D_RNN),
            pl.BlockSpec((D_ATT + D_RNN, d), lambda b, i: (0, 0), pipeline_mode=pl.Buffered(1)),
            row(d),
            pl.BlockSpec((1, 1, 1, d), lambda b, i: (2, b, 0, 0)),
            vec(d),
        ],
        out_specs=row(d),
        out_shape=jax.ShapeDtypeStruct((bsz, s, d), F32),
        compiler_params=_params(("parallel", "arbitrary")),
        name="out_proj",
    )(o_att, o_rnn, g_att.reshape(1, -1), g_rnn.reshape(1, -1), w_out, x, mod6, g_post.reshape(1, d))


def _mlp_kernel(x_ref, g_ref, sc_ref, sh_ref, w1_ref, w2_ref, gt_ref, gp_ref, o_ref, h_ref, a_ref):
    j = pl.program_id(2)
    n_chunks = pl.num_programs(2) - 1
    slot = j % 2

    def up():
        u = jnp.maximum(_dot(h_ref[...], w1_ref[...]), 0.0)
        a_ref[slot] = (u * u).astype(BF16)

    def down():
        o_ref[0] += _dot(a_ref[1 - slot], w2_ref[...])

    @pl.when(j == 0)
    def _():
        h = _rms(x_ref[0], g_ref[...]) * (1.0 + sc_ref[0, 0]) + sh_ref[0, 0]
        h_ref[...] = h.astype(BF16)
        o_ref[0] = jnp.zeros(o_ref.shape[1:], F32)
        up()

    @pl.when((j > 0) & (j < n_chunks))
    def _():
        down()
        up()

    @pl.when(j == n_chunks)
    def _():
        down()
        o_ref[0] = x_ref[0] + gt_ref[0, 0] * _rms(o_ref[0], gp_ref[...])


def _mlp(x, g_pre, mod6, w1, w2, g_post, tm=1024, tf=1024):
    bsz, s, d = x.shape
    n_chunks = w1.shape[1] // tf
    row = pl.BlockSpec((1, tm, d), lambda b, i, j: (b, i, 0))
    vec = pl.BlockSpec((1, d), lambda b, i, j: (0, 0))
    modk = lambda k: pl.BlockSpec((1, 1, 1, d), lambda b, i, j: (k, b, 0, 0))
    return pl.pallas_call(
        _mlp_kernel,
        grid=(bsz, s // tm, n_chunks + 1),
        in_specs=[
            row, vec, modk(4), modk(3),
            pl.BlockSpec((d, tf), lambda b, i, j: (0, jnp.minimum(j, n_chunks - 1))),
            pl.BlockSpec((tf, d), lambda b, i, j: (jnp.maximum(j - 1, 0), 0)),
            modk(5), vec,
        ],
        out_specs=row,
        out_shape=jax.ShapeDtypeStruct((bsz, s, d), F32),
        scratch_shapes=[pltpu.VMEM((tm, d), BF16), pltpu.VMEM((2, tm, tf), BF16)],
        compiler_params=_params(("parallel", "parallel", "arbitrary")),
        name="mlp",
    )(x, g_pre.reshape(1, d), mod6, mod6, w1, w2, mod6, g_post.reshape(1, d))


def _layer(x, c, w_ada, b_ada, g_pre_mix, g_post_mix, g_pre_mlp, g_post_mlp, w_in_perm,
           cmp_w1_k, cmp_w2_k, cmp_pe_k, cmp_w1_v, cmp_w2_v, cmp_pe_v,
           conv_w, conv_b, w_rg_a, b_rg_a, w_rg_x, b_rg_x, lru_lambda,
           g_grp_att, g_grp_rnn, w_out, w_ff1, w_ff2):
    bsz, s, d = x.shape
    dh = HEAD_DIM

    mod = _ada(c, w_ada, b_ada)
    mod6 = mod.reshape(bsz, 6, 1, d).transpose(1, 0, 2, 3)

    qt, kvc, ks, kw, vst, vwt, r_in, glt = _inproj(x, g_pre_mix, mod6, w_in_perm)

    def pair_diag(w, axis):
        z = jnp.zeros_like(w)
        return jnp.concatenate([jnp.concatenate([w, z], axis=-1), jnp.concatenate([z, w], axis=-1)], axis=axis)

    pe = jnp.tile(jnp.stack([cmp_pe_k, cmp_pe_v]), (1, 1, 2))
    w1c = pair_diag(jnp.stack([cmp_w1_k, cmp_w1_v]).astype(BF16).reshape(2, CMP_BLOCK, dh, -1), 2)
    w2c = pair_diag(jnp.stack([cmp_w2_k, cmp_w2_v]).astype(BF16), 1)
    kv_cmp = _compress(kvc, pe, w1c, w2c)

    o_cmp, sel_bias = _cmp_select(qt, kv_cmp)
    o_att = _attention(qt, o_cmp, sel_bias, ks, kw, vst, vwt, glt)

    w_cat = jnp.concatenate([w_rg_a, w_rg_x], axis=-1).astype(BF16)
    b_cat = jnp.concatenate([b_rg_a.reshape(RNN_BLOCKS, 1, RNN_BLOCK_DIM),
                             b_rg_x.reshape(RNN_BLOCKS, 1, RNN_BLOCK_DIM)], axis=-1)
    o_rnn = _rglru(r_in, conv_w, conv_b.reshape(1, -1), w_cat, b_cat, lru_lambda.reshape(1, -1))

    x1 = _outproj(o_att, o_rnn, g_grp_att, g_grp_rnn, w_out.astype(BF16), x, mod6, g_post_mix)
    return _mlp(x1, g_pre_mlp, mod6, w_ff1.astype(BF16), w_ff2.astype(BF16), g_post_mlp)


def kernel(x, c, w_ada, b_ada, g_pre_mix, g_post_mix, g_pre_mlp, g_post_mlp, w_in, cmp_w1_k, cmp_w2_k, cmp_pe_k, cmp_w1_v, cmp_w2_v, cmp_pe_v, conv_w, conv_b, w_rg_a, b_rg_a, w_rg_x, b_rg_x, lru_lambda, g_grp_att, g_grp_rnn, w_out, w_ff1, w_ff2):
    depth = w_ada.shape[0]
    for l in range(depth):
        x = _layer(x, c, w_ada[l], b_ada[l], g_pre_mix[l], g_post_mix[l], g_pre_mlp[l], g_post_mlp[l],
                   _w_in_prep(w_in, l), cmp_w1_k[l], cmp_w2_k[l], cmp_pe_k[l], cmp_w1_v[l], cmp_w2_v[l], cmp_pe_v[l],
                   conv_w[l], conv_b[l], w_rg_a[l], b_rg_a[l], w_rg_x[l], b_rg_x[l], lru_lambda[l],
                   g_grp_att[l], g_grp_rnn[l], w_out[l], w_ff1[l], w_ff2[l])
    return x
```

```python
import functools

import jax
import jax.numpy as jnp
from jax import lax
from jax.experimental import pallas as pl
from jax.experimental.pallas import tpu as pltpu

F32 = jnp.float32
BF16 = jnp.bfloat16

D_MODEL = 2048
D_ATT = 1024
D_RNN = 1024
N_Q_HEADS = 16
N_KV_GROUPS = 4
HEADS_PER_GROUP = 4
HEAD_DIM = 64
D_KV = 256
CMP_BLOCK = 32
CMP_STRIDE = 16
SEL_BLOCK = 64
SEL_SHIFT = 6
SEL_TOP_N = 8
WINDOW = 512
N_BRANCH = 3
RNN_BLOCKS = 8
RNN_BLOCK_DIM = 128
CONV_WIDTH = 4
LRU_C = 8.0
D_FF = 4 * D_MODEL
EPS = 1e-6
NEG = -1e30
FORCE_SCORE = 1e9
LOG2_E = 1.4426950408889634

LANES = 128
SUBLANES = 8
VMEM_LIMIT = 60 * 1024 * 1024

TQ = 256
N_CMP_PAD = 128
N_SEL = 32
GL_PAD = 128
TT = 256


def _params(sem):
    return pltpu.CompilerParams(dimension_semantics=sem, vmem_limit_bytes=VMEM_LIMIT)


def _dot(a, b):
    return jnp.dot(a, b, preferred_element_type=F32)


def _rms(x, g):
    return x * lax.rsqrt(jnp.mean(x * x, axis=-1, keepdims=True) + EPS) * g


def _ada_kernel(c_ref, w_ref, b_ref, o_ref):
    c = c_ref[...]
    ca = (c * jax.nn.sigmoid(c)).astype(BF16)
    o_ref[...] = _dot(ca, w_ref[...].astype(BF16)) + b_ref[...]


def _ada(c, w, b):
    bsz, d = c.shape
    n = w.shape[1]
    tn = 1024
    return pl.pallas_call(
        _ada_kernel,
        grid=(n // tn,),
        in_specs=[
            pl.BlockSpec((bsz, d), lambda j: (0, 0)),
            pl.BlockSpec((d, tn), lambda j: (0, j)),
            pl.BlockSpec((1, tn), lambda j: (0, j)),
        ],
        out_specs=pl.BlockSpec((bsz, tn), lambda j: (0, j)),
        out_shape=jax.ShapeDtypeStruct((bsz, n), F32),
        compiler_params=_params(("arbitrary",)),
        name="ada_mod",
    )(c, w, b.reshape(1, n))


C_Q = (0, D_ATT)
C_CMP = (C_Q[1], C_Q[1] + 2 * D_KV)
C_K = (C_CMP[1], C_CMP[1] + 2 * D_KV)
C_V = (C_K[1], C_K[1] + 2 * D_KV)
C_R = (C_V[1], C_V[1] + 2 * D_RNN)
C_GL = (C_R[1], C_R[1] + GL_PAD)
D_IN_PAD = C_GL[1]
K_ROW = LANES
V_ROWS = HEAD_DIM + 16
N_GATE = HEADS_PER_GROUP * N_BRANCH


def _w_in_prep_kernel(w_ref, o_ref):
    cast = lambda lo, hi: w_ref[0, :, lo:hi].astype(BF16)
    kv0 = D_ATT
    gl0 = D_ATT + 6 * D_KV
    n_gl = N_BRANCH * N_Q_HEADS
    o_ref[:, 0:C_K[0] + D_KV] = cast(0, kv0 + 3 * D_KV)
    o_ref[:, C_K[0] + D_KV:C_K[1]] = cast(kv0 + 4 * D_KV, kv0 + 5 * D_KV)
    o_ref[:, C_V[0]:C_V[0] + D_KV] = cast(kv0 + 3 * D_KV, kv0 + 4 * D_KV)
    o_ref[:, C_V[0] + D_KV:C_V[1]] = cast(kv0 + 5 * D_KV, kv0 + 6 * D_KV)
    o_ref[:, C_R[0]:C_R[1]] = cast(gl0 + n_gl, gl0 + n_gl + 2 * D_RNN)
    o_ref[:, C_GL[0]:C_GL[1]] = jnp.concatenate(
        [cast(gl0, gl0 + n_gl), jnp.zeros((w_ref.shape[1], GL_PAD - n_gl), BF16)], axis=1)


def _w_in_prep(w_in_all, layer, tk=256):
    _, d, n = w_in_all.shape
    return pl.pallas_call(
        _w_in_prep_kernel,
        grid=(d // tk,),
        in_specs=[pl.BlockSpec((1, tk, n), lambda i: (layer, i, 0))],
        out_specs=pl.BlockSpec((tk, D_IN_PAD), lambda i: (i, 0)),
        out_shape=jax.ShapeDtypeStruct((d, D_IN_PAD), BF16),
        compiler_params=_params(("arbitrary",)),
        name="w_in_prep",
    )(w_in_all)


def _inproj_kernel(x_ref, g_ref, sc_ref, sh_ref, w_ref,
                   qt_ref, cmp_ref, ks_ref, kw_ref, vst_ref, vwt_ref, r_ref, glt_ref):
    tm = x_ref.shape[1]
    ng, hg, dh = N_KV_GROUPS, HEADS_PER_GROUP, HEAD_DIM
    x = x_ref[0]
    h = _rms(x, g_ref[...]) * (1.0 + sc_ref[0, 0]) + sh_ref[0, 0]
    hb = h.astype(BF16)
    tiles = [slice(u * TQ, (u + 1) * TQ) for u in range(tm // TQ)]

    q_t = (_dot(hb, w_ref[:, C_Q[0]:C_Q[1]]) * (dh ** -0.5 * LOG2_E)).T.astype(BF16)
    for j in range(ng):
        for u, cols in enumerate(tiles):
            qt_ref[0, j, u] = jnp.concatenate(
                [q_t[(hg * j + hh) * dh:(hg * j + hh + 1) * dh, cols] for hh in range(hg)], axis=1)

    kv_cmp = _dot(hb, w_ref[:, C_CMP[0]:C_CMP[1]])
    for c in range(2 * D_KV // LANES):
        cmp_ref[0, c] = kv_cmp[:, c * LANES:(c + 1) * LANES]

    keys = _dot(hb, w_ref[:, C_K[0]:C_K[1]])
    row_pos = pl.program_id(1) * tm + lax.broadcasted_iota(jnp.int32, (tm, K_ROW - dh), 0)
    lane = lax.broadcasted_iota(jnp.int32, (tm, K_ROW - dh), 1)
    onehot = jnp.where(lax.shift_right_logical(row_pos, SEL_SHIFT) == lane, 1.0, 0.0)
    for j in range(ng):
        ks_ref[0, j] = jnp.concatenate([keys[:, j * dh:(j + 1) * dh], onehot], axis=1).astype(BF16)
        kw_ref[0, j] = jnp.concatenate(
            [keys[:, D_KV + j * dh:D_KV + (j + 1) * dh], onehot], axis=1).astype(BF16)

    v_t = _dot(hb, w_ref[:, C_V[0]:C_V[1]]).T.astype(BF16)
    ones = jnp.ones((V_ROWS - dh, TQ), BF16)
    for j in range(ng):
        for u, cols in enumerate(tiles):
            vst_ref[0, j, u] = jnp.concatenate([v_t[j * dh:(j + 1) * dh, cols], ones], axis=0)
            vwt_ref[0, j, u] = jnp.concatenate([v_t[D_KV + j * dh:D_KV + (j + 1) * dh, cols], ones], axis=0)

    half = (C_R[0] + C_R[1]) // 2
    r_ref[0, :, 0:D_RNN] = _dot(hb, w_ref[:, C_R[0]:half])
    r_ref[0, :, D_RNN:2 * D_RNN] = _dot(hb, w_ref[:, half:C_R[1]])

    gl_t = _dot(hb, w_ref[:, C_GL[0]:C_GL[1]]).T
    for j in range(ng):
        glt_ref[0, j] = gl_t[N_GATE * j:N_GATE * (j + 1), :]


def _inproj(x, g, mod6, w_perm, tm=512):
    bsz, s, d = x.shape
    ng = N_KV_GROUPS
    nt = tm // TQ
    row = lambda width: pl.BlockSpec((1, tm, width), lambda b, i: (b, i, 0))
    grp_rows = pl.BlockSpec((1, ng, tm, K_ROW), lambda b, i: (b, 0, i, 0))
    grp_tiles = lambda r, c: pl.BlockSpec((1, ng, nt, r, c), lambda b, i: (b, 0, i, 0, 0))
    sds = jax.ShapeDtypeStruct
    return pl.pallas_call(
        _inproj_kernel,
        grid=(bsz, s // tm),
        in_specs=[
            row(d),
            pl.BlockSpec((1, d), lambda b, i: (0, 0)),
            pl.BlockSpec((1, 1, 1, d), lambda b, i: (1, b, 0, 0)),
            pl.BlockSpec((1, 1, 1, d), lambda b, i: (0, b, 0, 0)),
            pl.BlockSpec((d, D_IN_PAD), lambda b, i: (0, 0), pipeline_mode=pl.Buffered(1)),
        ],
        out_specs=[
            grp_tiles(HEAD_DIM, HEADS_PER_GROUP * TQ),
            pl.BlockSpec((1, 2 * D_KV // LANES, tm, LANES), lambda b, i: (b, 0, i, 0)),
            grp_rows, grp_rows,
            grp_tiles(V_ROWS, TQ), grp_tiles(V_ROWS, TQ), row(2 * D_RNN),
            pl.BlockSpec((1, ng, N_GATE, tm), lambda b, i: (b, 0, 0, i)),
        ],
        out_shape=[
            sds((bsz, ng, s // TQ, HEAD_DIM, HEADS_PER_GROUP * TQ), BF16),
            sds((bsz, 2 * D_KV // LANES, s, LANES), F32),
            sds((bsz, ng, s, K_ROW), BF16), sds((bsz, ng, s, K_ROW), BF16),
            sds((bsz, ng, s // TQ, V_ROWS, TQ), BF16), sds((bsz, ng, s // TQ, V_ROWS, TQ), BF16),
            sds((bsz, s, 2 * D_RNN), F32),
            sds((bsz, ng, N_GATE, s), F32),
        ],
        compiler_params=_params(("parallel", "arbitrary")),
        name="in_proj",
    )(x, g.reshape(1, d), mod6, mod6, w_perm)


def _compress_kernel(x_ref, pe_ref, w1_ref, w2_ref, o_ref):
    bsz = x_ref.shape[0]
    dh = HEAD_DIM
    n_rows = x_ref.shape[2] // CMP_STRIDE
    rows = bsz * n_rows
    first = jnp.zeros((rows, w1_ref.shape[4]), F32)
    second = jnp.zeros((rows, w1_ref.shape[4]), F32)
    for lp in range(CMP_STRIDE // 2):
        x_2 = jnp.concatenate(
            [x_ref[:, 0, pl.ds(2 * lp + k, n_rows, stride=CMP_STRIDE), :].reshape(rows, LANES) for k in range(2)],
            axis=1)
        first = first + _dot((x_2 + pe_ref[0, 0, lp:lp + 1, :]).astype(BF16), w1_ref[0, 0, lp])
        second = second + _dot((x_2 + pe_ref[0, 1, lp:lp + 1, :]).astype(BF16), w1_ref[0, 1, lp])
    hid = first + pltpu.roll(second, rows - 1, axis=0)
    out = _dot(jax.nn.gelu(hid).astype(BF16), w2_ref[0]).astype(BF16)
    for b in range(bsz):
        for j in range(2):
            o_ref[0, b, j] = out[b * n_rows:(b + 1) * n_rows, j * dh:(j + 1) * dh]


def _compress(kvc4, pe, w1, w2):
    bsz, n_tiles, s, _ = kvc4.shape
    n_rows = s // CMP_STRIDE
    per_kind = lambda a: pl.BlockSpec((1,) + a.shape[1:], lambda t, p: (t,) + (0,) * (a.ndim - 1))
    return pl.pallas_call(
        _compress_kernel,
        grid=(2, n_tiles // 2),
        in_specs=[pl.BlockSpec((bsz, 1, s, LANES), lambda t, p: (0, 2 * t + p, 0, 0)),
                  per_kind(pe), per_kind(w1), per_kind(w2)],
        out_specs=pl.BlockSpec((1, bsz, 2, n_rows, HEAD_DIM), lambda t, p: (t, 0, p, 0, 0)),
        out_shape=jax.ShapeDtypeStruct((2, bsz, N_KV_GROUPS, n_rows, HEAD_DIM), BF16),
        compiler_params=_params(("arbitrary", "arbitrary")),
        name="compress_kv",
    )(kvc4, pe, w1, w2)


def _cmp_select_kernel(qt_ref, kc_ref, vc_ref, oc_ref, bias_ref):
    hg = HEADS_PER_GROUP
    kc = kc_ref[0, 0, 0]
    vct = vc_ref[0, 0, 0].astype(F32).T.astype(BF16)
    jj = lax.broadcasted_iota(jnp.int32, (N_SEL, N_CMP_PAD), 0) * SEL_BLOCK
    nn = lax.broadcasted_iota(jnp.int32, (N_SEL, N_CMP_PAD), 1) * CMP_STRIDE
    ov = jnp.minimum(nn + CMP_BLOCK, jj + SEL_BLOCK) - jnp.maximum(nn, jj)
    w_sel = (jnp.maximum(ov, 0).astype(F32) * (1.0 / CMP_BLOCK)).astype(BF16)
    nrow = lax.broadcasted_iota(jnp.int32, (N_CMP_PAD, TQ), 0)
    blk = lax.broadcasted_iota(jnp.int32, (N_SEL, TQ), 0)

    for qi in range(qt_ref.shape[2]):
        qt = qt_ref[0, 0, qi]
        pos = qi * TQ + lax.broadcasted_iota(jnp.int32, (1, TQ), 1)
        mask_c = nrow * CMP_STRIDE + (CMP_BLOCK - 1) <= pos

        def per_head_where(a, fill):
            return jnp.concatenate(
                [jnp.where(mask_c, a[:, h * TQ:(h + 1) * TQ], fill) for h in range(hg)], axis=1)

        s = per_head_where(_dot(kc, qt), NEG)
        e = jnp.exp2(s - jnp.max(s, axis=0, keepdims=True))
        p = per_head_where(e * (1.0 / jnp.sum(e, axis=0, keepdims=True)), 0.0)
        oc_ref[0, 0, qi] = _dot(vct, p.astype(BF16))
        psum = p[:, 0:TQ]
        for h in range(1, hg):
            psum = psum + p[:, h * TQ:(h + 1) * TQ]

        p_hi = psum.astype(BF16)
        p_lo = (psum - p_hi.astype(F32)).astype(BF16)
        imp = _dot(w_sel, p_hi) + _dot(w_sel, p_lo)
        cur = lax.shift_right_logical(pos, SEL_SHIFT)
        free = jnp.where(blk * SEL_BLOCK <= pos, imp, -FORCE_SCORE)
        score = jnp.where(blk == 0, FORCE_SCORE,
                          jnp.where(blk == cur, FORCE_SCORE, jnp.where(blk == cur - 1, FORCE_SCORE, free)))
        rank = jnp.zeros((N_SEL, TQ), F32)
        for k in range(N_SEL):
            sk = score[k:k + 1, :]
            tie = jnp.where(blk > k, 1.0, 0.0)
            rank = rank + jnp.where(sk > score, 1.0, jnp.where(sk == score, tie, 0.0))
        bias_ref[0, 0, qi] = jnp.where(rank < SEL_TOP_N, 0.0, NEG).astype(BF16)


def _cmp_select(qt, kv_cmp):
    bsz, g, nq, dh, wide = qt.shape
    per_bg = lambda *shape: pl.BlockSpec((1, 1) + shape, lambda b, j: (b, j) + (0,) * len(shape))
    kind = lambda t: pl.BlockSpec((1, 1, 1, N_CMP_PAD, dh), lambda b, j: (t, b, j, 0, 0))
    return pl.pallas_call(
        _cmp_select_kernel,
        grid=(bsz, g),
        in_specs=[per_bg(nq, dh, wide), kind(0), kind(1)],
        out_specs=[per_bg(nq, dh, wide), per_bg(nq, N_SEL, TQ)],
        out_shape=[jax.ShapeDtypeStruct((bsz, g, nq, dh, wide), F32),
                   jax.ShapeDtypeStruct((bsz, g, nq, N_SEL, TQ), BF16)],
        compiler_params=_params(("parallel", "arbitrary")),
        name="cmp_select",
    )(qt, kv_cmp, kv_cmp)


def _attn_kernel(qt_ref, oc_ref, bias_ref, ks_ref, kw_ref, vst_ref, vwt_ref, gl_ref, o_ref,
                 m_ref, acc_ref, sc_ref):
    qi = pl.program_id(2)
    last = pl.num_programs(2) - 1
    hg = HEADS_PER_GROUP
    dh = HEAD_DIM
    wide = hg * TQ
    qt = qt_ref[0, 0, 0]

    def per_head_where(mask, a, fill):
        return jnp.concatenate(
            [jnp.where(mask, a[:, h * TQ:(h + 1) * TQ], fill) for h in range(hg)], axis=1)

    def block_bias(value):
        return jnp.full((N_SEL, wide), value, F32).astype(BF16)

    zero_rows = jnp.zeros((K_ROW - dh - N_SEL, wide), BF16)

    def scores(k_ref, kt, bias):
        k_t = k_ref[0, 0, pl.ds(pl.multiple_of(kt * TQ, TQ), TQ), :]
        return _dot(k_t, jnp.concatenate([qt, bias, zero_rows], axis=0))

    rel = (lax.broadcasted_iota(jnp.int32, (TQ, TQ), 0)
           - lax.broadcasted_iota(jnp.int32, (TQ, TQ), 1))

    def online(state, sc, v_t):
        m_tile = jnp.max(sc, axis=0, keepdims=True)
        if state is None:
            return m_tile, _dot(v_t, jnp.exp2(sc - m_tile).astype(BF16))
        m_old, acc_old = state
        m_new = jnp.maximum(m_old, m_tile)
        return m_new, jnp.exp2(m_old - m_new) * acc_old + _dot(v_t, jnp.exp2(sc - m_new).astype(BF16))

    k1 = jnp.maximum(qi - 1, 0)
    k2 = jnp.maximum(qi - 2, 0)
    s0 = per_head_where(rel <= 0, scores(kw_ref, qi, block_bias(0.0)), NEG)
    s1 = scores(kw_ref, k1, block_bias(jnp.where(qi >= 1, 0.0, NEG)))
    s2 = per_head_where(rel > 0, scores(kw_ref, k2, block_bias(jnp.where(qi >= 2, 0.0, NEG))), NEG)
    win = online(None, s0, vwt_ref[0, 0, qi])
    win = online(win, s1, vwt_ref[0, 0, k1])
    win = online(win, s2, vwt_ref[0, 0, k2])
    acc_ref[1] = win[1]

    bias_sel = jnp.concatenate([bias_ref[0, 0, 0]] * hg, axis=1)

    sd = per_head_where(rel <= 0, scores(ks_ref, qi, bias_sel), NEG)
    m_ref[...], acc_ref[0] = online(None, sd, vst_ref[0, 0, qi])

    def sel_scores(kt):
        pad_bias = block_bias(jnp.where(kt < qi, 0.0, NEG))
        return scores(ks_ref, jnp.minimum(kt, last), jnp.minimum(bias_sel, pad_bias))

    def sel_accumulate(sc, kt):
        m_ref[...], acc_ref[0] = online((m_ref[...], acc_ref[0]), sc, vst_ref[0, 0, jnp.minimum(kt, last)])

    sc_ref[0] = sel_scores(0)

    def sel_pair(pi, carry):
        kt = 2 * pi
        sc_ref[1] = sel_scores(kt + 1)
        sel_accumulate(sc_ref[0], kt)
        sc_ref[0] = sel_scores(kt + 2)
        sel_accumulate(sc_ref[1], kt + 1)
        return carry

    lax.fori_loop(0, (qi + 1) // 2, sel_pair, 0)

    gate = jax.nn.sigmoid(gl_ref[0, 0])
    o_cmp = oc_ref[0, 0, 0]
    acc_s = acc_ref[0]
    acc_w = acc_ref[1]
    o_sel = acc_s[0:dh, :] * (1.0 / acc_s[dh:dh + 1, :])
    o_win = acc_w[0:dh, :] * (1.0 / acc_w[dh:dh + 1, :])
    outs = []
    for h in range(hg):
        lanes = slice(h * TQ, (h + 1) * TQ)
        g0 = gate[3 * h:3 * h + 1, :]
        g1 = gate[3 * h + 1:3 * h + 2, :]
        g2 = gate[3 * h + 2:3 * h + 3, :]
        outs.append(g0 * o_cmp[:, lanes] + g1 * o_sel[:, lanes] + g2 * o_win[:, lanes])
    o_ref[0] = jnp.concatenate(outs, axis=0).T


def _attention(qt, o_cmp, bias, ks, kw, vst, vwt, glt):
    bsz, g, nq = qt.shape[:3]
    s = nq * TQ
    width = HEADS_PER_GROUP * HEAD_DIM
    wide = HEADS_PER_GROUP * TQ
    per_bg = lambda *shape: pl.BlockSpec((1, 1) + shape, lambda b, j, i: (b, j) + (0,) * len(shape))
    per_tile = lambda *shape: pl.BlockSpec((1, 1, 1) + shape, lambda b, j, i: (b, j, i) + (0,) * len(shape))
    return pl.pallas_call(
        _attn_kernel,
        grid=(bsz, g, nq),
        in_specs=[
            per_tile(HEAD_DIM, wide),
            per_tile(HEAD_DIM, wide),
            per_tile(N_SEL, TQ),
            per_bg(s, K_ROW),
            per_bg(s, K_ROW),
            per_bg(nq, V_ROWS, TQ),
            per_bg(nq, V_ROWS, TQ),
            pl.BlockSpec((1, 1, HEADS_PER_GROUP * N_BRANCH, TQ), lambda b, j, i: (b, j, 0, i)),
        ],
        out_specs=pl.BlockSpec((1, TQ, width), lambda b, j, i: (b, i, j)),
        out_shape=jax.ShapeDtypeStruct((bsz, s, D_ATT), F32),
        scratch_shapes=[
            pltpu.VMEM((1, wide), F32),
            pltpu.VMEM((2, V_ROWS, wide), F32),
            pltpu.VMEM((2, TQ, wide), F32),
        ],
        compiler_params=_params(("parallel", "parallel", "arbitrary")),
        name="nsa_attention",
    )(qt, o_cmp, bias, ks, kw, vst, vwt, glt)


def _rglru_kernel(x_ref, y_ref, cw_ref, cb_ref, w_ref, b_ref, lam_ref, o_ref,
                  xpad_ref, a_ref, bt_ref):
    s = x_ref.shape[1]
    c = RNN_BLOCK_DIM
    pad = SUBLANES
    xpad_ref[0:pad, :] = jnp.zeros((pad, c), F32)
    xpad_ref[pad:pad + s, :] = x_ref[0]
    cw = cw_ref[...]
    w = w_ref[0]
    bias = b_ref[0]
    lam = lam_ref[...]
    neg_softplus = -(jnp.maximum(-lam, 0.0) + jnp.log1p(jnp.exp(-jnp.abs(lam))))
    sub = lax.broadcasted_iota(jnp.int32, (TT, c), 0) & (SUBLANES - 1)

    for ci in range(s // TT):
        t0 = ci * TT
        xc = cb_ref[...] + sum(
            xpad_ref[t0 + pad - (CONV_WIDTH - 1) + k:t0 + pad - (CONV_WIDTH - 1) + k + TT, :] * cw[k:k + 1, :]
            for k in range(CONV_WIDTH))
        gates = jax.nn.sigmoid(_dot(xc.astype(BF16), w) + bias)
        r = gates[:, 0:c]
        i = gates[:, c:2 * c]
        log_a = LRU_C * r * neg_softplus
        a = jnp.exp(log_a)
        var = -jnp.tanh(log_a) * (a * a + 1.0)
        bt = jnp.where(var > 0.0, var * lax.rsqrt(var), 0.0) * (i * xc)
        for d in (1, 2, 4):
            keep = sub >= d
            a_prev = jnp.where(keep, pltpu.roll(a, d, axis=0), 1.0)
            b_prev = jnp.where(keep, pltpu.roll(bt, d, axis=0), 0.0)
            bt = bt + a * b_prev
            a = a * a_prev
        a_ref[t0:t0 + TT, :] = a
        bt_ref[t0:t0 + TT, :] = bt

    def group(gi, h):
        r0 = pl.multiple_of(gi * SUBLANES, SUBLANES)
        hg = bt_ref[pl.ds(r0, SUBLANES), :] + a_ref[pl.ds(r0, SUBLANES), :] * h
        bt_ref[pl.ds(r0, SUBLANES), :] = hg
        return hg[SUBLANES - 1:SUBLANES, :]

    lax.fori_loop(0, s // SUBLANES, group, jnp.zeros((1, c), F32), unroll=8)

    for ci in range(s // TT):
        t0 = ci * TT
        o_ref[0, t0:t0 + TT, :] = jax.nn.gelu(y_ref[0, t0:t0 + TT, :]) * bt_ref[t0:t0 + TT, :]


def _rglru(r_in, conv_w, conv_b, w_cat, b_cat, lam):
    bsz, s, _ = r_in.shape
    c = RNN_BLOCK_DIM
    nb = RNN_BLOCKS
    return pl.pallas_call(
        _rglru_kernel,
        grid=(bsz, nb),
        in_specs=[
            pl.BlockSpec((1, s, c), lambda b, j: (b, 0, j)),
            pl.BlockSpec((1, s, c), lambda b, j: (b, 0, nb + j)),
            pl.BlockSpec((CONV_WIDTH, c), lambda b, j: (0, j)),
            pl.BlockSpec((1, c), lambda b, j: (0, j)),
            pl.BlockSpec((1, c, 2 * c), lambda b, j: (j, 0, 0)),
            pl.BlockSpec((1, 1, 2 * c), lambda b, j: (j, 0, 0)),
            pl.BlockSpec((1, c), lambda b, j: (0, j)),
        ],
        out_specs=pl.BlockSpec((1, s, c), lambda b, j: (b, 0, j)),
        out_shape=jax.ShapeDtypeStruct((bsz, s, D_RNN), F32),
        scratch_shapes=[
            pltpu.VMEM((s + SUBLANES, c), F32),
            pltpu.VMEM((s, c), F32),
            pltpu.VMEM((s, c), F32),
        ],
        compiler_params=_params(("parallel", "arbitrary")),
        name="rg_lru",
    )(r_in, r_in, conv_w, conv_b, w_cat, b_cat, lam)


def _outproj_kernel(oa_ref, or_ref, ga_ref, gr_ref, w_ref, x_ref, gt_ref, gp_ref, o_ref):
    a = _rms(oa_ref[0], ga_ref[...]).astype(BF16)
    r = _rms(or_ref[0], gr_ref[...]).astype(BF16)
    mix = _dot(a, w_ref[0:D_ATT, :]) + _dot(r, w_ref[D_ATT:D_ATT + D_RNN, :])
    o_ref[0] = x_ref[0] + gt_ref[0, 0] * _rms(mix, gp_ref[...])


def _outproj(o_att, o_rnn, g_att, g_rnn, w_out, x, mod6, g_post, tm=512):
    bsz, s, d = x.shape
    row = lambda width: pl.BlockSpec((1, tm, width), lambda b, i: (b, i, 0))
    vec = lambda width: pl.BlockSpec((1, width), lambda b, i: (0, 0))
    return pl.pallas_call(
        _outproj_kernel,
        grid=(bsz, s // tm),
        in_specs=[
            row(D_ATT), row(D_RNN), vec(D_ATT), vec(D_RNN),
            pl.BlockSpec((D_ATT + D_RNN, d), lambda b, i: (0, 0), pipeline_mode=pl.Buffered(1)),
            row(d),
            pl.BlockSpec((1, 1, 1, d), lambda b, i: (2, b, 0, 0)),
            vec(d),
        ],
        out_specs=row(d),
        out_shape=jax.ShapeDtypeStruct((bsz, s, d), F32),
        compiler_params=_params(("parallel", "arbitrary")),
        name="out_proj",
    )(o_att, o_rnn, g_att.reshape(1, -1), g_rnn.reshape(1, -1), w_out, x, mod6, g_post.reshape(1, d))


def _mlp_kernel(x_ref, g_ref, sc_ref, sh_ref, w1_ref, w2_ref, gt_ref, gp_ref, o_ref, h_ref, a_ref):
    j = pl.program_id(2)
    n_chunks = pl.num_programs(2) - 1
    slot = j % 2

    def up():
        u = jnp.maximum(_dot(h_ref[...], w1_ref[...]), 0.0)
        a_ref[slot] = (u * u).astype(BF16)

    def down():
        o_ref[0] += _dot(a_ref[1 - slot], w2_ref[...])

    @pl.when(j == 0)
    def _():
        h = _rms(x_ref[0], g_ref[...]) * (1.0 + sc_ref[0, 0]) + sh_ref[0, 0]
        h_ref[...] = h.astype(BF16)
        o_ref[0] = jnp.zeros(o_ref.shape[1:], F32)
        up()

    @pl.when((j > 0) & (j < n_chunks))
    def _():
        down()
        up()

    @pl.when(j == n_chunks)
    def _():
        down()
        o_ref[0] = x_ref[0] + gt_ref[0, 0] * _rms(o_ref[0], gp_ref[...])


def _mlp(x, g_pre, mod6, w1, w2, g_post, tm=1024, tf=1024):
    bsz, s, d = x.shape
    n_chunks = w1.shape[1] // tf
    row = pl.BlockSpec((1, tm, d), lambda b, i, j: (b, i, 0))
    vec = pl.BlockSpec((1, d), lambda b, i, j: (0, 0))
    modk = lambda k: pl.BlockSpec((1, 1, 1, d), lambda b, i, j: (k, b, 0, 0))
    return pl.pallas_call(
        _mlp_kernel,
        grid=(bsz, s // tm, n_chunks + 1),
        in_specs=[
            row, vec, modk(4), modk(3),
            pl.BlockSpec((d, tf), lambda b, i, j: (0, jnp.minimum(j, n_chunks - 1))),
            pl.BlockSpec((tf, d), lambda b, i, j: (jnp.maximum(j - 1, 0), 0)),
            modk(5), vec,
        ],
        out_specs=row,
        out_shape=jax.ShapeDtypeStruct((bsz, s, d), F32),
        scratch_shapes=[pltpu.VMEM((tm, d), BF16), pltpu.VMEM((2, tm, tf), BF16)],
        compiler_params=_params(("parallel", "parallel", "arbitrary")),
        name="mlp",
    )(x, g_pre.reshape(1, d), mod6, mod6, w1, w2, mod6, g_post.reshape(1, d))


def _layer(x, c, w_ada, b_ada, g_pre_mix, g_post_mix, g_pre_mlp, g_post_mlp, w_in_perm,
           cmp_w1_k, cmp_w2_k, cmp_pe_k, cmp_w1_v, cmp_w2_v, cmp_pe_v,
           conv_w, conv_b, w_rg_a, b_rg_a, w_rg_x, b_rg_x, lru_lambda,
           g_grp_att, g_grp_rnn, w_out, w_ff1, w_ff2):
    bsz, s, d = x.shape
    dh = HEAD_DIM

    mod = _ada(c, w_ada, b_ada)
    mod6 = mod.reshape(bsz, 6, 1, d).transpose(1, 0, 2, 3)

    qt, kvc, ks, kw, vst, vwt, r_in, glt = _inproj(x, g_pre_mix, mod6, w_in_perm)

    def pair_diag(w, axis):
        z = jnp.zeros_like(w)
        return jnp.concatenate([jnp.concatenate([w, z], axis=-1), jnp.concatenate([z, w], axis=-1)], axis=axis)

    n_pair = CMP_STRIDE // 2
    pe = jnp.tile(jnp.stack([cmp_pe_k, cmp_pe_v]), (1, 1, 2)).reshape(2, 2, n_pair, 4 * dh)
    w1c = pair_diag(jnp.stack([cmp_w1_k, cmp_w1_v]).astype(BF16).reshape(2, CMP_BLOCK, dh, -1), 2)
    w1c = w1c.reshape(2, 2, n_pair, 4 * dh, w1c.shape[-1])
    w2c = pair_diag(jnp.stack([cmp_w2_k, cmp_w2_v]).astype(BF16), 1)
    kv_cmp = _compress(kvc, pe, w1c, w2c)

    o_cmp, sel_bias = _cmp_select(qt, kv_cmp)
    o_att = _attention(qt, o_cmp, sel_bias, ks, kw, vst, vwt, glt)

    w_cat = jnp.concatenate([w_rg_a, w_rg_x], axis=-1).astype(BF16)
    b_cat = jnp.concatenate([b_rg_a.reshape(RNN_BLOCKS, 1, RNN_BLOCK_DIM),
                             b_rg_x.reshape(RNN_BLOCKS, 1, RNN_BLOCK_DIM)], axis=-1)
    o_rnn = _rglru(r_in, conv_w, conv_b.reshape(1, -1), w_cat, b_cat, lru_lambda.reshape(1, -1))

    x1 = _outproj(o_att, o_rnn, g_grp_att, g_grp_rnn, w_out.astype(BF16), x, mod6, g_post_mix)
    return _mlp(x1, g_pre_mlp, mod6, w_ff1.astype(BF16), w_ff2.astype(BF16), g_post_mlp)


def kernel(x, c, w_ada, b_ada, g_pre_mix, g_post_mix, g_pre_mlp, g_post_mlp, w_in, cmp_w1_k, cmp_w2_k, cmp_pe_k, cmp_w1_v, cmp_w2_v, cmp_pe_v, conv_w, conv_b, w_rg_a, b_rg_a, w_rg_x, b_rg_x, lru_lambda, g_grp_att, g_grp_rnn, w_out, w_ff1, w_ff2):
    depth = w_ada.shape[0]
    for l in range(depth):
        x = _layer(x, c, w_ada[l], b_ada[l], g_pre_mix[l], g_post_mix[l], g_pre_mlp[l], g_post_mlp[l],
                   _w_in_prep(w_in, l), cmp_w1_k[l], cmp_w2_k[l], cmp_pe_k[l], cmp_w1_v[l], cmp_w2_v[l], cmp_pe_v[l],
                   conv_w[l], conv_b[l], w_rg_a[l], b_rg_a[l], w_rg_x[l], b_rg_x[l], lru_lambda[l],
                   g_grp_att[l], g_grp_rnn[l], w_out[l], w_ff1[l], w_ff2[l])
    return x
```

```python
import functools

import jax
import jax.numpy as jnp
from jax import lax
from jax.experimental import pallas as pl
from jax.experimental.pallas import tpu as pltpu

F32 = jnp.float32
BF16 = jnp.bfloat16

D_MODEL = 2048
D_ATT = 1024
D_RNN = 1024
N_Q_HEADS = 16
N_KV_GROUPS = 4
HEADS_PER_GROUP = 4
HEAD_DIM = 64
D_KV = 256
CMP_BLOCK = 32
CMP_STRIDE = 16
SEL_BLOCK = 64
SEL_SHIFT = 6
SEL_TOP_N = 8
WINDOW = 512
N_BRANCH = 3
RNN_BLOCKS = 8
RNN_BLOCK_DIM = 128
CONV_WIDTH = 4
LRU_C = 8.0
D_FF = 4 * D_MODEL
EPS = 1e-6
NEG = -1e30
FORCE_SCORE = 1e9
LOG2_E = 1.4426950408889634

LANES = 128
SUBLANES = 8
VMEM_LIMIT = 60 * 1024 * 1024

TQ = 256
N_CMP_PAD = 128
N_SEL = 32
GL_PAD = 128
TT = 256


def _params(sem):
    return pltpu.CompilerParams(dimension_semantics=sem, vmem_limit_bytes=VMEM_LIMIT)


def _dot(a, b):
    return jnp.dot(a, b, preferred_element_type=F32)


def _rms(x, g):
    return x * lax.rsqrt(jnp.mean(x * x, axis=-1, keepdims=True) + EPS) * g


def _ada_kernel(c_ref, w_ref, b_ref, o_ref):
    c = c_ref[...]
    ca = (c * jax.nn.sigmoid(c)).astype(BF16)
    o_ref[...] = _dot(ca, w_ref[...].astype(BF16)) + b_ref[...]


def _ada(c, w, b):
    bsz, d = c.shape
    n = w.shape[1]
    tn = 1024
    return pl.pallas_call(
        _ada_kernel,
        grid=(n // tn,),
        in_specs=[
            pl.BlockSpec((bsz, d), lambda j: (0, 0)),
            pl.BlockSpec((d, tn), lambda j: (0, j)),
            pl.BlockSpec((1, tn), lambda j: (0, j)),
        ],
        out_specs=pl.BlockSpec((bsz, tn), lambda j: (0, j)),
        out_shape=jax.ShapeDtypeStruct((bsz, n), F32),
        compiler_params=_params(("arbitrary",)),
        name="ada_mod",
    )(c, w, b.reshape(1, n))


C_Q = (0, D_ATT)
C_CMP = (C_Q[1], C_Q[1] + 2 * D_KV)
C_K = (C_CMP[1], C_CMP[1] + 2 * D_KV)
C_V = (C_K[1], C_K[1] + 2 * D_KV)
C_R = (C_V[1], C_V[1] + 2 * D_RNN)
C_GL = (C_R[1], C_R[1] + GL_PAD)
D_IN_PAD = C_GL[1]
K_ROW = LANES
V_ROWS = HEAD_DIM + 16
N_GATE = HEADS_PER_GROUP * N_BRANCH


def _w_in_prep_kernel(w_ref, o_ref):
    cast = lambda lo, hi: w_ref[0, :, lo:hi].astype(BF16)
    kv0 = D_ATT
    gl0 = D_ATT + 6 * D_KV
    n_gl = N_BRANCH * N_Q_HEADS
    o_ref[:, 0:C_K[0] + D_KV] = cast(0, kv0 + 3 * D_KV)
    o_ref[:, C_K[0] + D_KV:C_K[1]] = cast(kv0 + 4 * D_KV, kv0 + 5 * D_KV)
    o_ref[:, C_V[0]:C_V[0] + D_KV] = cast(kv0 + 3 * D_KV, kv0 + 4 * D_KV)
    o_ref[:, C_V[0] + D_KV:C_V[1]] = cast(kv0 + 5 * D_KV, kv0 + 6 * D_KV)
    o_ref[:, C_R[0]:C_R[1]] = cast(gl0 + n_gl, gl0 + n_gl + 2 * D_RNN)
    o_ref[:, C_GL[0]:C_GL[1]] = jnp.concatenate(
        [cast(gl0, gl0 + n_gl), jnp.zeros((w_ref.shape[1], GL_PAD - n_gl), BF16)], axis=1)


def _w_in_prep(w_in_all, layer, tk=256):
    _, d, n = w_in_all.shape
    return pl.pallas_call(
        _w_in_prep_kernel,
        grid=(d // tk,),
        in_specs=[pl.BlockSpec((1, tk, n), lambda i: (layer, i, 0))],
        out_specs=pl.BlockSpec((tk, D_IN_PAD), lambda i: (i, 0)),
        out_shape=jax.ShapeDtypeStruct((d, D_IN_PAD), BF16),
        compiler_params=_params(("arbitrary",)),
        name="w_in_prep",
    )(w_in_all)


def _inproj_kernel(x_ref, g_ref, sc_ref, sh_ref, w_ref,
                   qt_ref, cmp_ref, ks_ref, kw_ref, vst_ref, vwt_ref, r_ref, glt_ref):
    tm = x_ref.shape[1]
    ng, hg, dh = N_KV_GROUPS, HEADS_PER_GROUP, HEAD_DIM
    x = x_ref[0]
    h = _rms(x, g_ref[...]) * (1.0 + sc_ref[0, 0]) + sh_ref[0, 0]
    hb = h.astype(BF16)
    tiles = [slice(u * TQ, (u + 1) * TQ) for u in range(tm // TQ)]

    q_t = (_dot(hb, w_ref[:, C_Q[0]:C_Q[1]]) * (dh ** -0.5 * LOG2_E)).T.astype(BF16)
    for j in range(ng):
        for u, cols in enumerate(tiles):
            qt_ref[0, j, u] = jnp.concatenate(
                [q_t[(hg * j + hh) * dh:(hg * j + hh + 1) * dh, cols] for hh in range(hg)], axis=1)

    kv_cmp = _dot(hb, w_ref[:, C_CMP[0]:C_CMP[1]])
    for c in range(2 * D_KV // LANES):
        cmp_ref[0, c] = kv_cmp[:, c * LANES:(c + 1) * LANES]

    keys = _dot(hb, w_ref[:, C_K[0]:C_K[1]])
    row_pos = pl.program_id(1) * tm + lax.broadcasted_iota(jnp.int32, (tm, K_ROW - dh), 0)
    lane = lax.broadcasted_iota(jnp.int32, (tm, K_ROW - dh), 1)
    onehot = jnp.where(lax.shift_right_logical(row_pos, SEL_SHIFT) == lane, 1.0, 0.0)
    for j in range(ng):
        ks_ref[0, j] = jnp.concatenate([keys[:, j * dh:(j + 1) * dh], onehot], axis=1).astype(BF16)
        kw_ref[0, j] = jnp.concatenate(
            [keys[:, D_KV + j * dh:D_KV + (j + 1) * dh], onehot], axis=1).astype(BF16)

    v_t = _dot(hb, w_ref[:, C_V[0]:C_V[1]]).T.astype(BF16)
    ones = jnp.ones((V_ROWS - dh, TQ), BF16)
    for j in range(ng):
        for u, cols in enumerate(tiles):
            vst_ref[0, j, u] = jnp.concatenate([v_t[j * dh:(j + 1) * dh, cols], ones], axis=0)
            vwt_ref[0, j, u] = jnp.concatenate([v_t[D_KV + j * dh:D_KV + (j + 1) * dh, cols], ones], axis=0)

    half = (C_R[0] + C_R[1]) // 2
    r_ref[0, :, 0:D_RNN] = _dot(hb, w_ref[:, C_R[0]:half])
    r_ref[0, :, D_RNN:2 * D_RNN] = _dot(hb, w_ref[:, half:C_R[1]])

    gl_t = _dot(hb, w_ref[:, C_GL[0]:C_GL[1]]).T
    for j in range(ng):
        glt_ref[0, j] = gl_t[N_GATE * j:N_GATE * (j + 1), :]


def _inproj(x, g, mod6, w_perm, tm=512):
    bsz, s, d = x.shape
    ng = N_KV_GROUPS
    nt = tm // TQ
    row = lambda width: pl.BlockSpec((1, tm, width), lambda b, i: (b, i, 0))
    grp_rows = pl.BlockSpec((1, ng, tm, K_ROW), lambda b, i: (b, 0, i, 0))
    grp_tiles = lambda r, c: pl.BlockSpec((1, ng, nt, r, c), lambda b, i: (b, 0, i, 0, 0))
    sds = jax.ShapeDtypeStruct
    return pl.pallas_call(
        _inproj_kernel,
        grid=(bsz, s // tm),
        in_specs=[
            row(d),
            pl.BlockSpec((1, d), lambda b, i: (0, 0)),
            pl.BlockSpec((1, 1, 1, d), lambda b, i: (1, b, 0, 0)),
            pl.BlockSpec((1, 1, 1, d), lambda b, i: (0, b, 0, 0)),
            pl.BlockSpec((d, D_IN_PAD), lambda b, i: (0, 0), pipeline_mode=pl.Buffered(1)),
        ],
        out_specs=[
            grp_tiles(HEAD_DIM, HEADS_PER_GROUP * TQ),
            pl.BlockSpec((1, 2 * D_KV // LANES, tm, LANES), lambda b, i: (b, 0, i, 0)),
            grp_rows, grp_rows,
            grp_tiles(V_ROWS, TQ), grp_tiles(V_ROWS, TQ), row(2 * D_RNN),
            pl.BlockSpec((1, ng, N_GATE, tm), lambda b, i: (b, 0, 0, i)),
        ],
        out_shape=[
            sds((bsz, ng, s // TQ, HEAD_DIM, HEADS_PER_GROUP * TQ), BF16),
            sds((bsz, 2 * D_KV // LANES, s, LANES), F32),
            sds((bsz, ng, s, K_ROW), BF16), sds((bsz, ng, s, K_ROW), BF16),
            sds((bsz, ng, s // TQ, V_ROWS, TQ), BF16), sds((bsz, ng, s // TQ, V_ROWS, TQ), BF16),
            sds((bsz, s, 2 * D_RNN), F32),
            sds((bsz, ng, N_GATE, s), F32),
        ],
        compiler_params=_params(("parallel", "arbitrary")),
        name="in_proj",
    )(x, g.reshape(1, d), mod6, mod6, w_perm)


def _compress_kernel(x_ref, pe_ref, w1_ref, w2_ref, o_ref):
    bsz = x_ref.shape[0]
    dh = HEAD_DIM
    n_rows = x_ref.shape[2] // CMP_STRIDE
    rows = bsz * n_rows
    first = jnp.zeros((rows, w1_ref.shape[4]), F32)
    second = jnp.zeros((rows, w1_ref.shape[4]), F32)
    for lp in range(CMP_STRIDE // 2):
        x_2 = jnp.concatenate(
            [x_ref[:, 0, pl.ds(2 * lp + k, n_rows, stride=CMP_STRIDE), :].reshape(rows, LANES) for k in range(2)],
            axis=1)
        first = first + _dot((x_2 + pe_ref[0, 0, lp:lp + 1, :]).astype(BF16), w1_ref[0, 0, lp])
        second = second + _dot((x_2 + pe_ref[0, 1, lp:lp + 1, :]).astype(BF16), w1_ref[0, 1, lp])
    hid = first + pltpu.roll(second, rows - 1, axis=0)
    out = _dot(jax.nn.gelu(hid).astype(BF16), w2_ref[0]).astype(BF16)
    for b in range(bsz):
        for j in range(2):
            o_ref[0, b, j] = out[b * n_rows:(b + 1) * n_rows, j * dh:(j + 1) * dh]


def _compress(kvc4, pe, w1, w2):
    bsz, n_tiles, s, _ = kvc4.shape
    n_rows = s // CMP_STRIDE
    per_kind = lambda a: pl.BlockSpec((1,) + a.shape[1:], lambda t, p: (t,) + (0,) * (a.ndim - 1))
    return pl.pallas_call(
        _compress_kernel,
        grid=(2, n_tiles // 2),
        in_specs=[pl.BlockSpec((bsz, 1, s, LANES), lambda t, p: (0, 2 * t + p, 0, 0)),
                  per_kind(pe), per_kind(w1), per_kind(w2)],
        out_specs=pl.BlockSpec((1, bsz, 2, n_rows, HEAD_DIM), lambda t, p: (t, 0, p, 0, 0)),
        out_shape=jax.ShapeDtypeStruct((2, bsz, N_KV_GROUPS, n_rows, HEAD_DIM), BF16),
        compiler_params=_params(("arbitrary", "arbitrary")),
        name="compress_kv",
    )(kvc4, pe, w1, w2)


def _cmp_select_kernel(qt_ref, kc_ref, vc_ref, oc_ref, bias_ref):
    hg = HEADS_PER_GROUP
    kc = kc_ref[0, 0, 0]
    vct = vc_ref[0, 0, 0].astype(F32).T.astype(BF16)
    jj = lax.broadcasted_iota(jnp.int32, (N_SEL, N_CMP_PAD), 0) * SEL_BLOCK
    nn = lax.broadcasted_iota(jnp.int32, (N_SEL, N_CMP_PAD), 1) * CMP_STRIDE
    ov = jnp.minimum(nn + CMP_BLOCK, jj + SEL_BLOCK) - jnp.maximum(nn, jj)
    w_sel = (jnp.maximum(ov, 0).astype(F32) * (1.0 / CMP_BLOCK)).astype(BF16)
    nrow = lax.broadcasted_iota(jnp.int32, (N_CMP_PAD, TQ), 0)
    blk = lax.broadcasted_iota(jnp.int32, (N_SEL, TQ), 0)

    for qi in range(qt_ref.shape[2]):
        qt = qt_ref[0, 0, qi]
        pos = qi * TQ + lax.broadcasted_iota(jnp.int32, (1, TQ), 1)
        mask_c = nrow * CMP_STRIDE + (CMP_BLOCK - 1) <= pos

        def per_head_where(a, fill):
            return jnp.concatenate(
                [jnp.where(mask_c, a[:, h * TQ:(h + 1) * TQ], fill) for h in range(hg)], axis=1)

        s = per_head_where(_dot(kc, qt), NEG)
        e = jnp.exp2(s - jnp.max(s, axis=0, keepdims=True))
        p = per_head_where(e * (1.0 / jnp.sum(e, axis=0, keepdims=True)), 0.0)
        oc_ref[0, 0, qi] = _dot(vct, p.astype(BF16))
        psum = p[:, 0:TQ]
        for h in range(1, hg):
            psum = psum + p[:, h * TQ:(h + 1) * TQ]

        p_hi = psum.astype(BF16)
        p_lo = (psum - p_hi.astype(F32)).astype(BF16)
        imp = _dot(w_sel, p_hi) + _dot(w_sel, p_lo)
        cur = lax.shift_right_logical(pos, SEL_SHIFT)
        free = jnp.where(blk * SEL_BLOCK <= pos, imp, -FORCE_SCORE)
        score = jnp.where(blk == 0, FORCE_SCORE,
                          jnp.where(blk == cur, FORCE_SCORE, jnp.where(blk == cur - 1, FORCE_SCORE, free)))
        rank = jnp.zeros((N_SEL, TQ), F32)
        for k in range(N_SEL):
            sk = score[k:k + 1, :]
            tie = jnp.where(blk > k, 1.0, 0.0)
            rank = rank + jnp.where(sk > score, 1.0, jnp.where(sk == score, tie, 0.0))
        bias_ref[0, 0, qi] = jnp.where(rank < SEL_TOP_N, 0.0, NEG).astype(BF16)


def _cmp_select(qt, kv_cmp):
    bsz, g, nq, dh, wide = qt.shape
    per_bg = lambda *shape: pl.BlockSpec((1, 1) + shape, lambda b, j: (b, j) + (0,) * len(shape))
    kind = lambda t: pl.BlockSpec((1, 1, 1, N_CMP_PAD, dh), lambda b, j: (t, b, j, 0, 0))
    return pl.pallas_call(
        _cmp_select_kernel,
        grid=(bsz, g),
        in_specs=[per_bg(nq, dh, wide), kind(0), kind(1)],
        out_specs=[per_bg(nq, dh, wide), per_bg(nq, N_SEL, TQ)],
        out_shape=[jax.ShapeDtypeStruct((bsz, g, nq, dh, wide), F32),
                   jax.ShapeDtypeStruct((bsz, g, nq, N_SEL, TQ), BF16)],
        compiler_params=_params(("parallel", "arbitrary")),
        name="cmp_select",
    )(qt, kv_cmp, kv_cmp)


def _attn_kernel(qt_ref, oc_ref, bias_ref, ks_ref, kw_ref, vst_ref, vwt_ref, gl_ref, o_ref,
                 m_ref, acc_ref, sc_ref):
    qi = pl.program_id(2)
    last = pl.num_programs(2) - 1
    hg = HEADS_PER_GROUP
    dh = HEAD_DIM
    wide = hg * TQ
    qt = qt_ref[0, 0, 0]

    def per_head_where(mask, a, fill):
        return jnp.concatenate(
            [jnp.where(mask, a[:, h * TQ:(h + 1) * TQ], fill) for h in range(hg)], axis=1)

    def block_bias(value):
        return jnp.full((N_SEL, wide), value, F32).astype(BF16)

    zero_rows = jnp.zeros((K_ROW - dh - N_SEL, wide), BF16)

    def scores(k_ref, kt, bias):
        k_t = k_ref[0, 0, pl.ds(pl.multiple_of(kt * TQ, TQ), TQ), :]
        return _dot(k_t, jnp.concatenate([qt, bias, zero_rows], axis=0))

    rel = (lax.broadcasted_iota(jnp.int32, (TQ, TQ), 0)
           - lax.broadcasted_iota(jnp.int32, (TQ, TQ), 1))

    def online(state, sc, v_t):
        m_tile = jnp.max(sc, axis=0, keepdims=True)
        if state is None:
            return m_tile, _dot(v_t, jnp.exp2(sc - m_tile).astype(BF16))
        m_old, acc_old = state
        m_new = jnp.maximum(m_old, m_tile)
        return m_new, jnp.exp2(m_old - m_new) * acc_old + _dot(v_t, jnp.exp2(sc - m_new).astype(BF16))

    k1 = jnp.maximum(qi - 1, 0)
    k2 = jnp.maximum(qi - 2, 0)
    s0 = per_head_where(rel <= 0, scores(kw_ref, qi, block_bias(0.0)), NEG)
    s1 = scores(kw_ref, k1, block_bias(jnp.where(qi >= 1, 0.0, NEG)))
    s2 = per_head_where(rel > 0, scores(kw_ref, k2, block_bias(jnp.where(qi >= 2, 0.0, NEG))), NEG)
    win = online(None, s0, vwt_ref[0, 0, qi])
    win = online(win, s1, vwt_ref[0, 0, k1])
    win = online(win, s2, vwt_ref[0, 0, k2])
    acc_ref[1] = win[1]

    bias_sel = jnp.concatenate([bias_ref[0, 0, 0]] * hg, axis=1)

    sd = per_head_where(rel <= 0, scores(ks_ref, qi, bias_sel), NEG)
    m_ref[...], acc_ref[0] = online(None, sd, vst_ref[0, 0, qi])

    def sel_scores(kt):
        pad_bias = block_bias(jnp.where(kt < qi, 0.0, NEG))
        return scores(ks_ref, jnp.minimum(kt, last), jnp.minimum(bias_sel, pad_bias))

    def sel_accumulate(sc, kt):
        m_ref[...], acc_ref[0] = online((m_ref[...], acc_ref[0]), sc, vst_ref[0, 0, jnp.minimum(kt, last)])

    sc_ref[0] = sel_scores(0)

    def sel_pair(pi, carry):
        kt = 2 * pi
        sc_ref[1] = sel_scores(kt + 1)
        sel_accumulate(sc_ref[0], kt)
        sc_ref[0] = sel_scores(kt + 2)
        sel_accumulate(sc_ref[1], kt + 1)
        return carry

    lax.fori_loop(0, (qi + 1) // 2, sel_pair, 0)

    gate = jax.nn.sigmoid(gl_ref[0, 0])
    o_cmp = oc_ref[0, 0, 0]
    acc_s = acc_ref[0]
    acc_w = acc_ref[1]
    o_sel = acc_s[0:dh, :] * (1.0 / acc_s[dh:dh + 1, :])
    o_win = acc_w[0:dh, :] * (1.0 / acc_w[dh:dh + 1, :])
    outs = []
    for h in range(hg):
        lanes = slice(h * TQ, (h + 1) * TQ)
        g0 = gate[3 * h:3 * h + 1, :]
        g1 = gate[3 * h + 1:3 * h + 2, :]
        g2 = gate[3 * h + 2:3 * h + 3, :]
        outs.append(g0 * o_cmp[:, lanes] + g1 * o_sel[:, lanes] + g2 * o_win[:, lanes])
    o_ref[0] = jnp.concatenate(outs, axis=0).T


def _attention(qt, o_cmp, bias, ks, kw, vst, vwt, glt):
    bsz, g, nq = qt.shape[:3]
    s = nq * TQ
    width = HEADS_PER_GROUP * HEAD_DIM
    wide = HEADS_PER_GROUP * TQ
    per_bg = lambda *shape: pl.BlockSpec((1, 1) + shape, lambda b, j, i: (b, j) + (0,) * len(shape))
    per_tile = lambda *shape: pl.BlockSpec((1, 1, 1) + shape, lambda b, j, i: (b, j, i) + (0,) * len(shape))
    return pl.pallas_call(
        _attn_kernel,
        grid=(bsz, g, nq),
        in_specs=[
            per_tile(HEAD_DIM, wide),
            per_tile(HEAD_DIM, wide),
            per_tile(N_SEL, TQ),
            per_bg(s, K_ROW),
            per_bg(s, K_ROW),
            per_bg(nq, V_ROWS, TQ),
            per_bg(nq, V_ROWS, TQ),
            pl.BlockSpec((1, 1, HEADS_PER_GROUP * N_BRANCH, TQ), lambda b, j, i: (b, j, 0, i)),
        ],
        out_specs=pl.BlockSpec((1, TQ, width), lambda b, j, i: (b, i, j)),
        out_shape=jax.ShapeDtypeStruct((bsz, s, D_ATT), F32),
        scratch_shapes=[
            pltpu.VMEM((1, wide), F32),
            pltpu.VMEM((2, V_ROWS, wide), F32),
            pltpu.VMEM((2, TQ, wide), F32),
        ],
        compiler_params=_params(("parallel", "parallel", "arbitrary")),
        name="nsa_attention",
    )(qt, o_cmp, bias, ks, kw, vst, vwt, glt)


def _rglru_kernel(x_ref, y_ref, cw_ref, cb_ref, w_ref, b_ref, lam_ref, o_ref,
                  xpad_ref, a_ref, bt_ref):
    s = x_ref.shape[1]
    c = RNN_BLOCK_DIM
    pad = SUBLANES
    for ref in (xpad_ref, a_ref, bt_ref):
        ref[0:pad, :] = jnp.zeros((pad, c), F32)
    xpad_ref[pad:pad + s, :] = x_ref[0]
    cw = cw_ref[...]
    w = w_ref[0]
    bias = b_ref[0]
    lam = lam_ref[...]
    neg_softplus = -(jnp.maximum(-lam, 0.0) + jnp.log1p(jnp.exp(-jnp.abs(lam))))
    sub = lax.broadcasted_iota(jnp.int32, (TT, c), 0) & (SUBLANES - 1)

    for ci in range(s // TT):
        t0 = ci * TT
        xc = cb_ref[...] + sum(
            xpad_ref[t0 + pad - (CONV_WIDTH - 1) + k:t0 + pad - (CONV_WIDTH - 1) + k + TT, :] * cw[k:k + 1, :]
            for k in range(CONV_WIDTH))
        gates = jax.nn.sigmoid(_dot(xc.astype(BF16), w) + bias)
        r = gates[:, 0:c]
        i = gates[:, c:2 * c]
        log_a = LRU_C * r * neg_softplus
        a = jnp.exp(log_a)
        var = -jnp.tanh(log_a) * (a * a + 1.0)
        bt = jnp.where(var > 0.0, var * lax.rsqrt(var), 0.0) * (i * xc)
        rows = slice(pad + t0, pad + t0 + TT)
        a_ref[rows, :] = a
        bt_ref[rows, :] = bt
        for d in (1, 2, 4):
            keep = sub >= d
            earlier = slice(pad + t0 - d, pad + t0 - d + TT)
            a_prev = jnp.where(keep, a_ref[earlier, :], 1.0)
            b_prev = jnp.where(keep, bt_ref[earlier, :], 0.0)
            bt = bt + a * b_prev
            a = a * a_prev
            a_ref[rows, :] = a
            bt_ref[rows, :] = bt

    def group(gi, h):
        r0 = pl.multiple_of(pad + gi * SUBLANES, SUBLANES)
        hg = bt_ref[pl.ds(r0, SUBLANES), :] + a_ref[pl.ds(r0, SUBLANES), :] * h
        bt_ref[pl.ds(r0, SUBLANES), :] = hg
        return hg[SUBLANES - 1:SUBLANES, :]

    lax.fori_loop(0, s // SUBLANES, group, jnp.zeros((1, c), F32), unroll=8)

    for ci in range(s // TT):
        t0 = ci * TT
        o_ref[0, t0:t0 + TT, :] = (jax.nn.gelu(y_ref[0, t0:t0 + TT, :])
                                   * bt_ref[pad + t0:pad + t0 + TT, :])


def _rglru(r_in, conv_w, conv_b, w_cat, b_cat, lam):
    bsz, s, _ = r_in.shape
    c = RNN_BLOCK_DIM
    nb = RNN_BLOCKS
    return pl.pallas_call(
        _rglru_kernel,
        grid=(bsz, nb),
        in_specs=[
            pl.BlockSpec((1, s, c), lambda b, j: (b, 0, j)),
            pl.BlockSpec((1, s, c), lambda b, j: (b, 0, nb + j)),
            pl.BlockSpec((CONV_WIDTH, c), lambda b, j: (0, j)),
            pl.BlockSpec((1, c), lambda b, j: (0, j)),
            pl.BlockSpec((1, c, 2 * c), lambda b, j: (j, 0, 0)),
            pl.BlockSpec((1, 1, 2 * c), lambda b, j: (j, 0, 0)),
            pl.BlockSpec((1, c), lambda b, j: (0, j)),
        ],
        out_specs=pl.BlockSpec((1, s, c), lambda b, j: (b, 0, j)),
        out_shape=jax.ShapeDtypeStruct((bsz, s, D_RNN), F32),
        scratch_shapes=[
            pltpu.VMEM((s + SUBLANES, c), F32),
            pltpu.VMEM((s + SUBLANES, c), F32),
            pltpu.VMEM((s + SUBLANES, c), F32),
        ],
        compiler_params=_params(("parallel", "arbitrary")),
        name="rg_lru",
    )(r_in, r_in, conv_w, conv_b, w_cat, b_cat, lam)


def _outproj_kernel(oa_ref, or_ref, ga_ref, gr_ref, w_ref, x_ref, gt_ref, gp_ref, o_ref):
    a = _rms(oa_ref[0], ga_ref[...]).astype(BF16)
    r = _rms(or_ref[0], gr_ref[...]).astype(BF16)
    mix = _dot(a, w_ref[0:D_ATT, :]) + _dot(r, w_ref[D_ATT:D_ATT + D_RNN, :])
    o_ref[0] = x_ref[0] + gt_ref[0, 0] * _rms(mix, gp_ref[...])


def _outproj(o_att, o_rnn, g_att, g_rnn, w_out, x, mod6, g_post, tm=512):
    bsz, s, d = x.shape
    row = lambda width: pl.BlockSpec((1, tm, width), lambda b, i: (b, i, 0))
    vec = lambda width: pl.BlockSpec((1, width), lambda b, i: (0, 0))
    return pl.pallas_call(
        _outproj_kernel,
        grid=(bsz, s // tm),
        in_specs=[
            row(D_ATT), row(D_RNN), vec(D_ATT), vec(D_RNN),
            pl.BlockSpec((D_ATT + D_RNN, d), lambda b, i: (0, 0), pipeline_mode=pl.Buffered(1)),
            row(d),
            pl.BlockSpec((1, 1, 1, d), lambda b, i: (2, b, 0, 0)),
            vec(d),
        ],
        out_specs=row(d),
        out_shape=jax.ShapeDtypeStruct((bsz, s, d), F32),
        compiler_params=_params(("parallel", "arbitrary")),
        name="out_proj",
    )(o_att, o_rnn, g_att.reshape(1, -1), g_rnn.reshape(1, -1), w_out, x, mod6, g_post.reshape(1, d))


def _mlp_kernel(x_ref, g_ref, sc_ref, sh_ref, w1_ref, w2_ref, gt_ref, gp_ref, o_ref, h_ref, a_ref):
    j = pl.program_id(2)
    n_chunks = pl.num_programs(2) - 1
    slot = j % 2

    def up():
        u = jnp.maximum(_dot(h_ref[...], w1_ref[...]), 0.0)
        a_ref[slot] = (u * u).astype(BF16)

    def down():
        o_ref[0] += _dot(a_ref[1 - slot], w2_ref[...])

    @pl.when(j == 0)
    def _():
        h = _rms(x_ref[0], g_ref[...]) * (1.0 + sc_ref[0, 0]) + sh_ref[0, 0]
        h_ref[...] = h.astype(BF16)
        o_ref[0] = jnp.zeros(o_ref.shape[1:], F32)
        up()

    @pl.when((j > 0) & (j < n_chunks))
    def _():
        down()
        up()

    @pl.when(j == n_chunks)
    def _():
        down()
        o_ref[0] = x_ref[0] + gt_ref[0, 0] * _rms(o_ref[0], gp_ref[...])


def _mlp(x, g_pre, mod6, w1, w2, g_post, tm=1024, tf=1024):
    bsz, s, d = x.shape
    n_chunks = w1.shape[1] // tf
    row = pl.BlockSpec((1, tm, d), lambda b, i, j: (b, i, 0))
    vec = pl.BlockSpec((1, d), lambda b, i, j: (0, 0))
    modk = lambda k: pl.BlockSpec((1, 1, 1, d), lambda b, i, j: (k, b, 0, 0))
    return pl.pallas_call(
        _mlp_kernel,
        grid=(bsz, s // tm, n_chunks + 1),
        in_specs=[
            row, vec, modk(4), modk(3),
            pl.BlockSpec((d, tf), lambda b, i, j: (0, jnp.minimum(j, n_chunks - 1))),
            pl.BlockSpec((tf, d), lambda b, i, j: (jnp.maximum(j - 1, 0), 0)),
            modk(5), vec,
        ],
        out_specs=row,
        out_shape=jax.ShapeDtypeStruct((bsz, s, d), F32),
        scratch_shapes=[pltpu.VMEM((tm, d), BF16), pltpu.VMEM((2, tm, tf), BF16)],
        compiler_params=_params(("parallel", "parallel", "arbitrary")),
        name="mlp",
    )(x, g_pre.reshape(1, d), mod6, mod6, w1, w2, mod6, g_post.reshape(1, d))


def _layer(x, c, w_ada, b_ada, g_pre_mix, g_post_mix, g_pre_mlp, g_post_mlp, w_in_perm,
           cmp_w1_k, cmp_w2_k, cmp_pe_k, cmp_w1_v, cmp_w2_v, cmp_pe_v,
           conv_w, conv_b, w_rg_a, b_rg_a, w_rg_x, b_rg_x, lru_lambda,
           g_grp_att, g_grp_rnn, w_out, w_ff1, w_ff2):
    bsz, s, d = x.shape
    dh = HEAD_DIM

    mod = _ada(c, w_ada, b_ada)
    mod6 = mod.reshape(bsz, 6, 1, d).transpose(1, 0, 2, 3)

    qt, kvc, ks, kw, vst, vwt, r_in, glt = _inproj(x, g_pre_mix, mod6, w_in_perm)

    def pair_diag(w, axis):
        z = jnp.zeros_like(w)
        return jnp.concatenate([jnp.concatenate([w, z], axis=-1), jnp.concatenate([z, w], axis=-1)], axis=axis)

    n_pair = CMP_STRIDE // 2
    pe = jnp.tile(jnp.stack([cmp_pe_k, cmp_pe_v]), (1, 1, 2)).reshape(2, 2, n_pair, 4 * dh)
    w1c = pair_diag(jnp.stack([cmp_w1_k, cmp_w1_v]).astype(BF16).reshape(2, CMP_BLOCK, dh, -1), 2)
    w1c = w1c.reshape(2, 2, n_pair, 4 * dh, w1c.shape[-1])
    w2c = pair_diag(jnp.stack([cmp_w2_k, cmp_w2_v]).astype(BF16), 1)
    kv_cmp = _compress(kvc, pe, w1c, w2c)

    o_cmp, sel_bias = _cmp_select(qt, kv_cmp)
    o_att = _attention(qt, o_cmp, sel_bias, ks, kw, vst, vwt, glt)

    w_cat = jnp.concatenate([w_rg_a, w_rg_x], axis=-1).astype(BF16)
    b_cat = jnp.concatenate([b_rg_a.reshape(RNN_BLOCKS, 1, RNN_BLOCK_DIM),
                             b_rg_x.reshape(RNN_BLOCKS, 1, RNN_BLOCK_DIM)], axis=-1)
    o_rnn = _rglru(r_in, conv_w, conv_b.reshape(1, -1), w_cat, b_cat, lru_lambda.reshape(1, -1))

    x1 = _outproj(o_att, o_rnn, g_grp_att, g_grp_rnn, w_out.astype(BF16), x, mod6, g_post_mix)
    return _mlp(x1, g_pre_mlp, mod6, w_ff1.astype(BF16), w_ff2.astype(BF16), g_post_mlp)


def kernel(x, c, w_ada, b_ada, g_pre_mix, g_post_mix, g_pre_mlp, g_post_mlp, w_in, cmp_w1_k, cmp_w2_k, cmp_pe_k, cmp_w1_v, cmp_w2_v, cmp_pe_v, conv_w, conv_b, w_rg_a, b_rg_a, w_rg_x, b_rg_x, lru_lambda, g_grp_att, g_grp_rnn, w_out, w_ff1, w_ff2):
    depth = w_ada.shape[0]
    for l in range(depth):
        x = _layer(x, c, w_ada[l], b_ada[l], g_pre_mix[l], g_post_mix[l], g_pre_mlp[l], g_post_mlp[l],
                   _w_in_prep(w_in, l), cmp_w1_k[l], cmp_w2_k[l], cmp_pe_k[l], cmp_w1_v[l], cmp_w2_v[l], cmp_pe_v[l],
                   conv_w[l], conv_b[l], w_rg_a[l], b_rg_a[l], w_rg_x[l], b_rg_x[l], lru_lambda[l],
                   g_grp_att[l], g_grp_rnn[l], w_out[l], w_ff1[l], w_ff2[l])
    return x
```

```python
import functools

import jax
import jax.numpy as jnp
from jax import lax
from jax.experimental import pallas as pl
from jax.experimental.pallas import tpu as pltpu

F32 = jnp.float32
BF16 = jnp.bfloat16

D_MODEL = 2048
D_ATT = 1024
D_RNN = 1024
N_Q_HEADS = 16
N_KV_GROUPS = 4
HEADS_PER_GROUP = 4
HEAD_DIM = 64
D_KV = 256
CMP_BLOCK = 32
CMP_STRIDE = 16
SEL_BLOCK = 64
SEL_SHIFT = 6
SEL_TOP_N = 8
WINDOW = 512
N_BRANCH = 3
RNN_BLOCKS = 8
RNN_BLOCK_DIM = 128
CONV_WIDTH = 4
LRU_C = 8.0
D_FF = 4 * D_MODEL
EPS = 1e-6
NEG = -1e30
FORCE_SCORE = 1e9
LOG2_E = 1.4426950408889634

LANES = 128
SUBLANES = 8
VMEM_LIMIT = 60 * 1024 * 1024

TQ = 256
N_CMP_PAD = 128
N_SEL = 32
GL_PAD = 128
TT = 256


def _params(sem):
    return pltpu.CompilerParams(dimension_semantics=sem, vmem_limit_bytes=VMEM_LIMIT)


def _dot(a, b):
    return jnp.dot(a, b, preferred_element_type=F32)


def _rms(x, g):
    return x * lax.rsqrt(jnp.mean(x * x, axis=-1, keepdims=True) + EPS) * g


def _ada_kernel(c_ref, w_ref, b_ref, o_ref):
    c = c_ref[...]
    ca = (c * jax.nn.sigmoid(c)).astype(BF16)
    o_ref[...] = _dot(ca, w_ref[...].astype(BF16)) + b_ref[...]


def _ada(c, w, b):
    bsz, d = c.shape
    n = w.shape[1]
    tn = 1024
    return pl.pallas_call(
        _ada_kernel,
        grid=(n // tn,),
        in_specs=[
            pl.BlockSpec((bsz, d), lambda j: (0, 0)),
            pl.BlockSpec((d, tn), lambda j: (0, j)),
            pl.BlockSpec((1, tn), lambda j: (0, j)),
        ],
        out_specs=pl.BlockSpec((bsz, tn), lambda j: (0, j)),
        out_shape=jax.ShapeDtypeStruct((bsz, n), F32),
        compiler_params=_params(("arbitrary",)),
        name="ada_mod",
    )(c, w, b.reshape(1, n))


C_Q = (0, D_ATT)
C_CMP = (C_Q[1], C_Q[1] + 2 * D_KV)
C_K = (C_CMP[1], C_CMP[1] + 2 * D_KV)
C_V = (C_K[1], C_K[1] + 2 * D_KV)
C_R = (C_V[1], C_V[1] + 2 * D_RNN)
C_GL = (C_R[1], C_R[1] + GL_PAD)
D_IN_PAD = C_GL[1]
K_ROW = LANES
V_ROWS = HEAD_DIM + 16
N_GATE = HEADS_PER_GROUP * N_BRANCH


def _w_in_prep_kernel(wt_ref, o_ref):
    seg = lambda lo, hi: wt_ref[0, lo:hi, :].T.astype(BF16)
    kv0 = D_ATT
    gl0 = D_ATT + 6 * D_KV
    n_gl = N_BRANCH * N_Q_HEADS
    o_ref[:, 0:C_K[0] + D_KV] = seg(0, kv0 + 3 * D_KV)
    o_ref[:, C_K[0] + D_KV:C_K[1]] = seg(kv0 + 4 * D_KV, kv0 + 5 * D_KV)
    o_ref[:, C_V[0]:C_V[0] + D_KV] = seg(kv0 + 3 * D_KV, kv0 + 4 * D_KV)
    o_ref[:, C_V[0] + D_KV:C_V[1]] = seg(kv0 + 5 * D_KV, kv0 + 6 * D_KV)
    o_ref[:, C_R[0]:C_R[1]] = seg(gl0 + n_gl, gl0 + n_gl + 2 * D_RNN)
    gl_tile = wt_ref[0, gl0:gl0 + GL_PAD, :].T
    lane = lax.broadcasted_iota(jnp.int32, gl_tile.shape, 1)
    o_ref[:, C_GL[0]:C_GL[1]] = jnp.where(lane < n_gl, gl_tile, 0.0).astype(BF16)


def _w_in_prep(w_in_all, layer, tk=256):
    _, d, n = w_in_all.shape
    w_t = jnp.swapaxes(w_in_all, 1, 2)
    return pl.pallas_call(
        _w_in_prep_kernel,
        grid=(d // tk,),
        in_specs=[pl.BlockSpec((1, n, tk), lambda i: (layer, 0, i))],
        out_specs=pl.BlockSpec((tk, D_IN_PAD), lambda i: (i, 0)),
        out_shape=jax.ShapeDtypeStruct((d, D_IN_PAD), BF16),
        compiler_params=_params(("arbitrary",)),
        name="w_in_prep",
    )(w_t)


def _inproj_kernel(x_ref, g_ref, sc_ref, sh_ref, w_ref,
                   qt_ref, cmp_ref, ks_ref, kw_ref, vst_ref, vwt_ref, r_ref, glt_ref):
    tm = x_ref.shape[1]
    ng, hg, dh = N_KV_GROUPS, HEADS_PER_GROUP, HEAD_DIM
    x = x_ref[0]
    h = _rms(x, g_ref[...]) * (1.0 + sc_ref[0, 0]) + sh_ref[0, 0]
    hb = h.astype(BF16)
    tiles = [slice(u * TQ, (u + 1) * TQ) for u in range(tm // TQ)]

    q_t = (_dot(hb, w_ref[:, C_Q[0]:C_Q[1]]) * (dh ** -0.5 * LOG2_E)).T.astype(BF16)
    for j in range(ng):
        for u, cols in enumerate(tiles):
            qt_ref[0, j, u] = jnp.concatenate(
                [q_t[(hg * j + hh) * dh:(hg * j + hh + 1) * dh, cols] for hh in range(hg)], axis=1)

    kv_cmp = _dot(hb, w_ref[:, C_CMP[0]:C_CMP[1]])
    for c in range(2 * D_KV // LANES):
        cmp_ref[0, c] = kv_cmp[:, c * LANES:(c + 1) * LANES]

    keys = _dot(hb, w_ref[:, C_K[0]:C_K[1]])
    row_pos = pl.program_id(1) * tm + lax.broadcasted_iota(jnp.int32, (tm, K_ROW - dh), 0)
    lane = lax.broadcasted_iota(jnp.int32, (tm, K_ROW - dh), 1)
    onehot = jnp.where(lax.shift_right_logical(row_pos, SEL_SHIFT) == lane, 1.0, 0.0)
    for j in range(ng):
        ks_ref[0, j] = jnp.concatenate([keys[:, j * dh:(j + 1) * dh], onehot], axis=1).astype(BF16)
        kw_ref[0, j] = jnp.concatenate(
            [keys[:, D_KV + j * dh:D_KV + (j + 1) * dh], onehot], axis=1).astype(BF16)

    v_t = _dot(hb, w_ref[:, C_V[0]:C_V[1]]).T.astype(BF16)
    ones = jnp.ones((V_ROWS - dh, TQ), BF16)
    for j in range(ng):
        for u, cols in enumerate(tiles):
            vst_ref[0, j, u] = jnp.concatenate([v_t[j * dh:(j + 1) * dh, cols], ones], axis=0)
            vwt_ref[0, j, u] = jnp.concatenate([v_t[D_KV + j * dh:D_KV + (j + 1) * dh, cols], ones], axis=0)

    half = (C_R[0] + C_R[1]) // 2
    r_ref[0, :, 0:D_RNN] = _dot(hb, w_ref[:, C_R[0]:half])
    r_ref[0, :, D_RNN:2 * D_RNN] = _dot(hb, w_ref[:, half:C_R[1]])

    gl_t = _dot(hb, w_ref[:, C_GL[0]:C_GL[1]]).T
    for j in range(ng):
        glt_ref[0, j] = gl_t[N_GATE * j:N_GATE * (j + 1), :]


def _inproj(x, g, mod6, w_perm, tm=512):
    bsz, s, d = x.shape
    ng = N_KV_GROUPS
    nt = tm // TQ
    row = lambda width: pl.BlockSpec((1, tm, width), lambda b, i: (b, i, 0))
    grp_rows = pl.BlockSpec((1, ng, tm, K_ROW), lambda b, i: (b, 0, i, 0))
    grp_tiles = lambda r, c: pl.BlockSpec((1, ng, nt, r, c), lambda b, i: (b, 0, i, 0, 0))
    sds = jax.ShapeDtypeStruct
    return pl.pallas_call(
        _inproj_kernel,
        grid=(bsz, s // tm),
        in_specs=[
            row(d),
            pl.BlockSpec((1, d), lambda b, i: (0, 0)),
            pl.BlockSpec((1, 1, 1, d), lambda b, i: (1, b, 0, 0)),
            pl.BlockSpec((1, 1, 1, d), lambda b, i: (0, b, 0, 0)),
            pl.BlockSpec((d, D_IN_PAD), lambda b, i: (0, 0), pipeline_mode=pl.Buffered(1)),
        ],
        out_specs=[
            grp_tiles(HEAD_DIM, HEADS_PER_GROUP * TQ),
            pl.BlockSpec((1, 2 * D_KV // LANES, tm, LANES), lambda b, i: (b, 0, i, 0)),
            grp_rows, grp_rows,
            grp_tiles(V_ROWS, TQ), grp_tiles(V_ROWS, TQ), row(2 * D_RNN),
            pl.BlockSpec((1, ng, N_GATE, tm), lambda b, i: (b, 0, 0, i)),
        ],
        out_shape=[
            sds((bsz, ng, s // TQ, HEAD_DIM, HEADS_PER_GROUP * TQ), BF16),
            sds((bsz, 2 * D_KV // LANES, s, LANES), F32),
            sds((bsz, ng, s, K_ROW), BF16), sds((bsz, ng, s, K_ROW), BF16),
            sds((bsz, ng, s // TQ, V_ROWS, TQ), BF16), sds((bsz, ng, s // TQ, V_ROWS, TQ), BF16),
            sds((bsz, s, 2 * D_RNN), F32),
            sds((bsz, ng, N_GATE, s), F32),
        ],
        compiler_params=_params(("parallel", "arbitrary")),
        name="in_proj",
    )(x, g.reshape(1, d), mod6, mod6, w_perm)


def _compress_kernel(x_ref, pe_ref, w1_ref, w2_ref, o_ref):
    bsz = x_ref.shape[0]
    dh = HEAD_DIM
    n_rows = x_ref.shape[2] // CMP_STRIDE
    rows = bsz * n_rows
    first = jnp.zeros((rows, w1_ref.shape[4]), F32)
    second = jnp.zeros((rows, w1_ref.shape[4]), F32)
    for lp in range(CMP_STRIDE // 2):
        x_2 = jnp.concatenate(
            [x_ref[:, 0, pl.ds(2 * lp + k, n_rows, stride=CMP_STRIDE), :].reshape(rows, LANES) for k in range(2)],
            axis=1)
        first = first + _dot((x_2 + pe_ref[0, 0, lp:lp + 1, :]).astype(BF16), w1_ref[0, 0, lp])
        second = second + _dot((x_2 + pe_ref[0, 1, lp:lp + 1, :]).astype(BF16), w1_ref[0, 1, lp])
    hid = first + pltpu.roll(second, rows - 1, axis=0)
    out = _dot(jax.nn.gelu(hid).astype(BF16), w2_ref[0]).astype(BF16)
    for b in range(bsz):
        for j in range(2):
            o_ref[0, b, j] = out[b * n_rows:(b + 1) * n_rows, j * dh:(j + 1) * dh]


def _compress(kvc4, pe, w1, w2):
    bsz, n_tiles, s, _ = kvc4.shape
    n_rows = s // CMP_STRIDE
    per_kind = lambda a: pl.BlockSpec((1,) + a.shape[1:], lambda t, p: (t,) + (0,) * (a.ndim - 1))
    return pl.pallas_call(
        _compress_kernel,
        grid=(2, n_tiles // 2),
        in_specs=[pl.BlockSpec((bsz, 1, s, LANES), lambda t, p: (0, 2 * t + p, 0, 0)),
                  per_kind(pe), per_kind(w1), per_kind(w2)],
        out_specs=pl.BlockSpec((1, bsz, 2, n_rows, HEAD_DIM), lambda t, p: (t, 0, p, 0, 0)),
        out_shape=jax.ShapeDtypeStruct((2, bsz, N_KV_GROUPS, n_rows, HEAD_DIM), BF16),
        compiler_params=_params(("arbitrary", "arbitrary")),
        name="compress_kv",
    )(kvc4, pe, w1, w2)


def _cmp_select_kernel(qt_ref, kc_ref, vc_ref, oc_ref, bias_ref):
    hg = HEADS_PER_GROUP
    kc = kc_ref[0, 0, 0]
    vct = vc_ref[0, 0, 0].astype(F32).T.astype(BF16)
    jj = lax.broadcasted_iota(jnp.int32, (N_SEL, N_CMP_PAD), 0) * SEL_BLOCK
    nn = lax.broadcasted_iota(jnp.int32, (N_SEL, N_CMP_PAD), 1) * CMP_STRIDE
    ov = jnp.minimum(nn + CMP_BLOCK, jj + SEL_BLOCK) - jnp.maximum(nn, jj)
    w_sel = (jnp.maximum(ov, 0).astype(F32) * (1.0 / CMP_BLOCK)).astype(BF16)
    nrow = lax.broadcasted_iota(jnp.int32, (N_CMP_PAD, TQ), 0)
    blk = lax.broadcasted_iota(jnp.int32, (N_SEL, TQ), 0)

    for qi in range(qt_ref.shape[2]):
        qt = qt_ref[0, 0, qi]
        pos = qi * TQ + lax.broadcasted_iota(jnp.int32, (1, TQ), 1)
        mask_c = nrow * CMP_STRIDE + (CMP_BLOCK - 1) <= pos

        def per_head_where(a, fill):
            return jnp.concatenate(
                [jnp.where(mask_c, a[:, h * TQ:(h + 1) * TQ], fill) for h in range(hg)], axis=1)

        s = per_head_where(_dot(kc, qt), NEG)
        e = jnp.exp2(s - jnp.max(s, axis=0, keepdims=True))
        p = per_head_where(e * (1.0 / jnp.sum(e, axis=0, keepdims=True)), 0.0)
        oc_ref[0, 0, qi] = _dot(vct, p.astype(BF16))
        psum = p[:, 0:TQ]
        for h in range(1, hg):
            psum = psum + p[:, h * TQ:(h + 1) * TQ]

        p_hi = psum.astype(BF16)
        p_lo = (psum - p_hi.astype(F32)).astype(BF16)
        imp = _dot(w_sel, p_hi) + _dot(w_sel, p_lo)
        cur = lax.shift_right_logical(pos, SEL_SHIFT)
        free = jnp.where(blk * SEL_BLOCK <= pos, imp, -FORCE_SCORE)
        score = jnp.where(blk == 0, FORCE_SCORE,
                          jnp.where(blk == cur, FORCE_SCORE, jnp.where(blk == cur - 1, FORCE_SCORE, free)))
        rank = jnp.zeros((N_SEL, TQ), F32)
        for k in range(N_SEL):
            sk = score[k:k + 1, :]
            tie = jnp.where(blk > k, 1.0, 0.0)
            rank = rank + jnp.where(sk > score, 1.0, jnp.where(sk == score, tie, 0.0))
        bias_ref[0, 0, qi] = jnp.where(rank < SEL_TOP_N, 0.0, NEG).astype(BF16)


def _cmp_select(qt, kv_cmp):
    bsz, g, nq, dh, wide = qt.shape
    per_bg = lambda *shape: pl.BlockSpec((1, 1) + shape, lambda b, j: (b, j) + (0,) * len(shape))
    kind = lambda t: pl.BlockSpec((1, 1, 1, N_CMP_PAD, dh), lambda b, j: (t, b, j, 0, 0))
    return pl.pallas_call(
        _cmp_select_kernel,
        grid=(bsz, g),
        in_specs=[per_bg(nq, dh, wide), kind(0), kind(1)],
        out_specs=[per_bg(nq, dh, wide), per_bg(nq, N_SEL, TQ)],
        out_shape=[jax.ShapeDtypeStruct((bsz, g, nq, dh, wide), F32),
                   jax.ShapeDtypeStruct((bsz, g, nq, N_SEL, TQ), BF16)],
        compiler_params=_params(("parallel", "arbitrary")),
        name="cmp_select",
    )(qt, kv_cmp, kv_cmp)


def _attn_kernel(qt_ref, oc_ref, bias_ref, ks_ref, kw_ref, vst_ref, vwt_ref, gl_ref, o_ref,
                 m_ref, acc_ref, sc_ref):
    qi = pl.program_id(2)
    last = pl.num_programs(2) - 1
    hg = HEADS_PER_GROUP
    dh = HEAD_DIM
    wide = hg * TQ
    qt = qt_ref[0, 0, 0]

    def per_head_where(mask, a, fill):
        return jnp.concatenate(
            [jnp.where(mask, a[:, h * TQ:(h + 1) * TQ], fill) for h in range(hg)], axis=1)

    def block_bias(value):
        return jnp.full((N_SEL, wide), value, F32).astype(BF16)

    zero_rows = jnp.zeros((K_ROW - dh - N_SEL, wide), BF16)

    def scores(k_ref, kt, bias):
        k_t = k_ref[0, 0, pl.ds(pl.multiple_of(kt * TQ, TQ), TQ), :]
        return _dot(k_t, jnp.concatenate([qt, bias, zero_rows], axis=0))

    rel = (lax.broadcasted_iota(jnp.int32, (TQ, TQ), 0)
           - lax.broadcasted_iota(jnp.int32, (TQ, TQ), 1))

    def online(state, sc, v_t):
        m_tile = jnp.max(sc, axis=0, keepdims=True)
        if state is None:
            return m_tile, _dot(v_t, jnp.exp2(sc - m_tile).astype(BF16))
        m_old, acc_old = state
        m_new = jnp.maximum(m_old, m_tile)
        return m_new, jnp.exp2(m_old - m_new) * acc_old + _dot(v_t, jnp.exp2(sc - m_new).astype(BF16))

    k1 = jnp.maximum(qi - 1, 0)
    k2 = jnp.maximum(qi - 2, 0)
    s0 = per_head_where(rel <= 0, scores(kw_ref, qi, block_bias(0.0)), NEG)
    s1 = scores(kw_ref, k1, block_bias(jnp.where(qi >= 1, 0.0, NEG)))
    s2 = per_head_where(rel > 0, scores(kw_ref, k2, block_bias(jnp.where(qi >= 2, 0.0, NEG))), NEG)
    win = online(None, s0, vwt_ref[0, 0, qi])
    win = online(win, s1, vwt_ref[0, 0, k1])
    win = online(win, s2, vwt_ref[0, 0, k2])
    acc_ref[1] = win[1]

    bias_sel = jnp.concatenate([bias_ref[0, 0, 0]] * hg, axis=1)

    sd = per_head_where(rel <= 0, scores(ks_ref, qi, bias_sel), NEG)
    m_ref[...], acc_ref[0] = online(None, sd, vst_ref[0, 0, qi])

    def sel_scores(kt):
        pad_bias = block_bias(jnp.where(kt < qi, 0.0, NEG))
        return scores(ks_ref, jnp.minimum(kt, last), jnp.minimum(bias_sel, pad_bias))

    def sel_accumulate(sc, kt):
        m_ref[...], acc_ref[0] = online((m_ref[...], acc_ref[0]), sc, vst_ref[0, 0, jnp.minimum(kt, last)])

    sc_ref[0] = sel_scores(0)

    def sel_pair(pi, carry):
        kt = 2 * pi
        sc_ref[1] = sel_scores(kt + 1)
        sel_accumulate(sc_ref[0], kt)
        sc_ref[0] = sel_scores(kt + 2)
        sel_accumulate(sc_ref[1], kt + 1)
        return carry

    lax.fori_loop(0, (qi + 1) // 2, sel_pair, 0)

    gate = jax.nn.sigmoid(gl_ref[0, 0])
    o_cmp = oc_ref[0, 0, 0]
    acc_s = acc_ref[0]
    acc_w = acc_ref[1]
    o_sel = acc_s[0:dh, :] * (1.0 / acc_s[dh:dh + 1, :])
    o_win = acc_w[0:dh, :] * (1.0 / acc_w[dh:dh + 1, :])
    outs = []
    for h in range(hg):
        lanes = slice(h * TQ, (h + 1) * TQ)
        g0 = gate[3 * h:3 * h + 1, :]
        g1 = gate[3 * h + 1:3 * h + 2, :]
        g2 = gate[3 * h + 2:3 * h + 3, :]
        outs.append(g0 * o_cmp[:, lanes] + g1 * o_sel[:, lanes] + g2 * o_win[:, lanes])
    o_ref[0] = jnp.concatenate(outs, axis=0).T


def _attention(qt, o_cmp, bias, ks, kw, vst, vwt, glt):
    bsz, g, nq = qt.shape[:3]
    s = nq * TQ
    width = HEADS_PER_GROUP * HEAD_DIM
    wide = HEADS_PER_GROUP * TQ
    per_bg = lambda *shape: pl.BlockSpec((1, 1) + shape, lambda b, j, i: (b, j) + (0,) * len(shape))
    per_tile = lambda *shape: pl.BlockSpec((1, 1, 1) + shape, lambda b, j, i: (b, j, i) + (0,) * len(shape))
    return pl.pallas_call(
        _attn_kernel,
        grid=(bsz, g, nq),
        in_specs=[
            per_tile(HEAD_DIM, wide),
            per_tile(HEAD_DIM, wide),
            per_tile(N_SEL, TQ),
            per_bg(s, K_ROW),
            per_bg(s, K_ROW),
            per_bg(nq, V_ROWS, TQ),
            per_bg(nq, V_ROWS, TQ),
            pl.BlockSpec((1, 1, HEADS_PER_GROUP * N_BRANCH, TQ), lambda b, j, i: (b, j, 0, i)),
        ],
        out_specs=pl.BlockSpec((1, TQ, width), lambda b, j, i: (b, i, j)),
        out_shape=jax.ShapeDtypeStruct((bsz, s, D_ATT), F32),
        scratch_shapes=[
            pltpu.VMEM((1, wide), F32),
            pltpu.VMEM((2, V_ROWS, wide), F32),
            pltpu.VMEM((2, TQ, wide), F32),
        ],
        compiler_params=_params(("parallel", "parallel", "arbitrary")),
        name="nsa_attention",
    )(qt, o_cmp, bias, ks, kw, vst, vwt, glt)


def _rglru_kernel(x_ref, y_ref, cw_ref, cb_ref, w_ref, b_ref, lam_ref, o_ref,
                  xpad_ref, a_ref, bt_ref):
    s = x_ref.shape[1]
    c = RNN_BLOCK_DIM
    pad = SUBLANES
    xpad_ref[0:pad, :] = jnp.zeros((pad, c), F32)
    xpad_ref[pad:pad + s, :] = x_ref[0]
    cw = cw_ref[...]
    w = w_ref[0]
    bias = b_ref[0]
    lam = lam_ref[...]
    neg_softplus = -(jnp.maximum(-lam, 0.0) + jnp.log1p(jnp.exp(-jnp.abs(lam))))
    sub = lax.broadcasted_iota(jnp.int32, (TT, c), 0) & (SUBLANES - 1)

    for ci in range(s // TT):
        t0 = ci * TT
        xc = cb_ref[...] + sum(
            xpad_ref[t0 + pad - (CONV_WIDTH - 1) + k:t0 + pad - (CONV_WIDTH - 1) + k + TT, :] * cw[k:k + 1, :]
            for k in range(CONV_WIDTH))
        gates = jax.nn.sigmoid(_dot(xc.astype(BF16), w) + bias)
        r = gates[:, 0:c]
        i = gates[:, c:2 * c]
        log_a = LRU_C * r * neg_softplus
        a = jnp.exp(log_a)
        var = -jnp.tanh(log_a) * (a * a + 1.0)
        bt = jnp.where(var > 0.0, var * lax.rsqrt(var), 0.0) * (i * xc)
        for d in (1, 2, 4):
            keep = sub >= d
            a_prev = jnp.where(keep, pltpu.roll(a, d, axis=0), 1.0)
            b_prev = jnp.where(keep, pltpu.roll(bt, d, axis=0), 0.0)
            bt = bt + a * b_prev
            a = a * a_prev
        a_ref[t0:t0 + TT, :] = a
        bt_ref[t0:t0 + TT, :] = bt

    def group(gi, h):
        r0 = pl.multiple_of(gi * SUBLANES, SUBLANES)
        hg = bt_ref[pl.ds(r0, SUBLANES), :] + a_ref[pl.ds(r0, SUBLANES), :] * h
        bt_ref[pl.ds(r0, SUBLANES), :] = hg
        return hg[SUBLANES - 1:SUBLANES, :]

    lax.fori_loop(0, s // SUBLANES, group, jnp.zeros((1, c), F32), unroll=8)

    for ci in range(s // TT):
        t0 = ci * TT
        o_ref[0, t0:t0 + TT, :] = jax.nn.gelu(y_ref[0, t0:t0 + TT, :]) * bt_ref[t0:t0 + TT, :]


def _rglru(r_in, conv_w, conv_b, w_cat, b_cat, lam):
    bsz, s, _ = r_in.shape
    c = RNN_BLOCK_DIM
    nb = RNN_BLOCKS
    return pl.pallas_call(
        _rglru_kernel,
        grid=(bsz, nb),
        in_specs=[
            pl.BlockSpec((1, s, c), lambda b, j: (b, 0, j)),
            pl.BlockSpec((1, s, c), lambda b, j: (b, 0, nb + j)),
            pl.BlockSpec((CONV_WIDTH, c), lambda b, j: (0, j)),
            pl.BlockSpec((1, c), lambda b, j: (0, j)),
            pl.BlockSpec((1, c, 2 * c), lambda b, j: (j, 0, 0)),
            pl.BlockSpec((1, 1, 2 * c), lambda b, j: (j, 0, 0)),
            pl.BlockSpec((1, c), lambda b, j: (0, j)),
        ],
        out_specs=pl.BlockSpec((1, s, c), lambda b, j: (b, 0, j)),
        out_shape=jax.ShapeDtypeStruct((bsz, s, D_RNN), F32),
        scratch_shapes=[
            pltpu.VMEM((s + SUBLANES, c), F32),
            pltpu.VMEM((s, c), F32),
            pltpu.VMEM((s, c), F32),
        ],
        compiler_params=_params(("parallel", "arbitrary")),
        name="rg_lru",
    )(r_in, r_in, conv_w, conv_b, w_cat, b_cat, lam)


def _outproj_kernel(oa_ref, or_ref, ga_ref, gr_ref, w_ref, x_ref, gt_ref, gp_ref, o_ref):
    a = _rms(oa_ref[0], ga_ref[...]).astype(BF16)
    r = _rms(or_ref[0], gr_ref[...]).astype(BF16)
    mix = _dot(a, w_ref[0:D_ATT, :]) + _dot(r, w_ref[D_ATT:D_ATT + D_RNN, :])
    o_ref[0] = x_ref[0] + gt_ref[0, 0] * _rms(mix, gp_ref[...])


def _outproj(o_att, o_rnn, g_att, g_rnn, w_out, x, mod6, g_post, tm=512):
    bsz, s, d = x.shape
    row = lambda width: pl.BlockSpec((1, tm, width), lambda b, i: (b, i, 0))
    vec = lambda width: pl.BlockSpec((1, width), lambda b, i: (0, 0))
    return pl.pallas_call(
        _outproj_kernel,
        grid=(bsz, s // tm),
        in_specs=[
            row(D_ATT), row(D_RNN), vec(D_ATT), vec(D_RNN),
            pl.BlockSpec((D_ATT + D_RNN, d), lambda b, i: (0, 0), pipeline_mode=pl.Buffered(1)),
            row(d),
            pl.BlockSpec((1, 1, 1, d), lambda b, i: (2, b, 0, 0)),
            vec(d),
        ],
        out_specs=row(d),
        out_shape=jax.ShapeDtypeStruct((bsz, s, d), F32),
        compiler_params=_params(("parallel", "arbitrary")),
        name="out_proj",
    )(o_att, o_rnn, g_att.reshape(1, -1), g_rnn.reshape(1, -1), w_out, x, mod6, g_post.reshape(1, d))


def _mlp_kernel(x_ref, g_ref, sc_ref, sh_ref, w1_ref, w2_ref, gt_ref, gp_ref, o_ref, h_ref, a_ref):
    j = pl.program_id(2)
    n_chunks = pl.num_programs(2) - 1
    slot = j % 2

    def up():
        u = jnp.maximum(_dot(h_ref[...], w1_ref[...]), 0.0)
        a_ref[slot] = (u * u).astype(BF16)

    def down():
        o_ref[0] += _dot(a_ref[1 - slot], w2_ref[...])

    @pl.when(j == 0)
    def _():
        h = _rms(x_ref[0], g_ref[...]) * (1.0 + sc_ref[0, 0]) + sh_ref[0, 0]
        h_ref[...] = h.astype(BF16)
        o_ref[0] = jnp.zeros(o_ref.shape[1:], F32)
        up()

    @pl.when((j > 0) & (j < n_chunks))
    def _():
        down()
        up()

    @pl.when(j == n_chunks)
    def _():
        down()
        o_ref[0] = x_ref[0] + gt_ref[0, 0] * _rms(o_ref[0], gp_ref[...])


def _mlp(x, g_pre, mod6, w1, w2, g_post, tm=1024, tf=1024):
    bsz, s, d = x.shape
    n_chunks = w1.shape[1] // tf
    row = pl.BlockSpec((1, tm, d), lambda b, i, j: (b, i, 0))
    vec = pl.BlockSpec((1, d), lambda b, i, j: (0, 0))
    modk = lambda k: pl.BlockSpec((1, 1, 1, d), lambda b, i, j: (k, b, 0, 0))
    return pl.pallas_call(
        _mlp_kernel,
        grid=(bsz, s // tm, n_chunks + 1),
        in_specs=[
            row, vec, modk(4), modk(3),
            pl.BlockSpec((d, tf), lambda b, i, j: (0, jnp.minimum(j, n_chunks - 1))),
            pl.BlockSpec((tf, d), lambda b, i, j: (jnp.maximum(j - 1, 0), 0)),
            modk(5), vec,
        ],
        out_specs=row,
        out_shape=jax.ShapeDtypeStruct((bsz, s, d), F32),
        scratch_shapes=[pltpu.VMEM((tm, d), BF16), pltpu.VMEM((2, tm, tf), BF16)],
        compiler_params=_params(("parallel", "parallel", "arbitrary")),
        name="mlp",
    )(x, g_pre.reshape(1, d), mod6, mod6, w1, w2, mod6, g_post.reshape(1, d))


def _layer(x, c, w_ada, b_ada, g_pre_mix, g_post_mix, g_pre_mlp, g_post_mlp, w_in_perm,
           cmp_w1_k, cmp_w2_k, cmp_pe_k, cmp_w1_v, cmp_w2_v, cmp_pe_v,
           conv_w, conv_b, w_rg_a, b_rg_a, w_rg_x, b_rg_x, lru_lambda,
           g_grp_att, g_grp_rnn, w_out, w_ff1, w_ff2):
    bsz, s, d = x.shape
    dh = HEAD_DIM

    mod = _ada(c, w_ada, b_ada)
    mod6 = mod.reshape(bsz, 6, 1, d).transpose(1, 0, 2, 3)

    qt, kvc, ks, kw, vst, vwt, r_in, glt = _inproj(x, g_pre_mix, mod6, w_in_perm)

    def pair_diag(w, axis):
        z = jnp.zeros_like(w)
        return jnp.concatenate([jnp.concatenate([w, z], axis=-1), jnp.concatenate([z, w], axis=-1)], axis=axis)

    n_pair = CMP_STRIDE // 2
    pe = jnp.tile(jnp.stack([cmp_pe_k, cmp_pe_v]), (1, 1, 2)).reshape(2, 2, n_pair, 4 * dh)
    w1c = pair_diag(jnp.stack([cmp_w1_k, cmp_w1_v]).astype(BF16).reshape(2, CMP_BLOCK, dh, -1), 2)
    w1c = w1c.reshape(2, 2, n_pair, 4 * dh, w1c.shape[-1])
    w2c = pair_diag(jnp.stack([cmp_w2_k, cmp_w2_v]).astype(BF16), 1)
    kv_cmp = _compress(kvc, pe, w1c, w2c)

    o_cmp, sel_bias = _cmp_select(qt, kv_cmp)
    o_att = _attention(qt, o_cmp, sel_bias, ks, kw, vst, vwt, glt)

    w_cat = jnp.concatenate([w_rg_a, w_rg_x], axis=-1).astype(BF16)
    b_cat = jnp.concatenate([b_rg_a.reshape(RNN_BLOCKS, 1, RNN_BLOCK_DIM),
                             b_rg_x.reshape(RNN_BLOCKS, 1, RNN_BLOCK_DIM)], axis=-1)
    o_rnn = _rglru(r_in, conv_w, conv_b.reshape(1, -1), w_cat, b_cat, lru_lambda.reshape(1, -1))

    x1 = _outproj(o_att, o_rnn, g_grp_att, g_grp_rnn, w_out.astype(BF16), x, mod6, g_post_mix)
    return _mlp(x1, g_pre_mlp, mod6, w_ff1.astype(BF16), w_ff2.astype(BF16), g_post_mlp)


def kernel(x, c, w_ada, b_ada, g_pre_mix, g_post_mix, g_pre_mlp, g_post_mlp, w_in, cmp_w1_k, cmp_w2_k, cmp_pe_k, cmp_w1_v, cmp_w2_v, cmp_pe_v, conv_w, conv_b, w_rg_a, b_rg_a, w_rg_x, b_rg_x, lru_lambda, g_grp_att, g_grp_rnn, w_out, w_ff1, w_ff2):
    depth = w_ada.shape[0]
    for l in range(depth):
        x = _layer(x, c, w_ada[l], b_ada[l], g_pre_mix[l], g_post_mix[l], g_pre_mlp[l], g_post_mlp[l],
                   _w_in_prep(w_in, l), cmp_w1_k[l], cmp_w2_k[l], cmp_pe_k[l], cmp_w1_v[l], cmp_w2_v[l], cmp_pe_v[l],
                   conv_w[l], conv_b[l], w_rg_a[l], b_rg_a[l], w_rg_x[l], b_rg_x[l], lru_lambda[l],
                   g_grp_att[l], g_grp_rnn[l], w_out[l], w_ff1[l], w_ff2[l])
    return x
```

```python
import functools

import jax
import jax.numpy as jnp
from jax import lax
from jax.experimental import pallas as pl
from jax.experimental.pallas import tpu as pltpu

F32 = jnp.float32
BF16 = jnp.bfloat16

D_MODEL = 2048
D_ATT = 1024
D_RNN = 1024
N_Q_HEADS = 16
N_KV_GROUPS = 4
HEADS_PER_GROUP = 4
HEAD_DIM = 64
D_KV = 256
CMP_BLOCK = 32
CMP_STRIDE = 16
SEL_BLOCK = 64
SEL_SHIFT = 6
SEL_TOP_N = 8
WINDOW = 512
N_BRANCH = 3
RNN_BLOCKS = 8
RNN_BLOCK_DIM = 128
CONV_WIDTH = 4
LRU_C = 8.0
D_FF = 4 * D_MODEL
EPS = 1e-6
NEG = -1e30
FORCE_SCORE = 1e9
LOG2_E = 1.4426950408889634

LANES = 128
SUBLANES = 8
VMEM_LIMIT = 60 * 1024 * 1024

TQ = 256
N_CMP_PAD = 128
N_SEL = 32
GL_PAD = 128
TT = 256


def _params(sem):
    return pltpu.CompilerParams(dimension_semantics=sem, vmem_limit_bytes=VMEM_LIMIT)


def _dot(a, b):
    return jnp.dot(a, b, preferred_element_type=F32)


def _rms(x, g):
    return x * lax.rsqrt(jnp.mean(x * x, axis=-1, keepdims=True) + EPS) * g


def _ada_kernel(c_ref, w_ref, b_ref, o_ref):
    c = c_ref[...]
    ca = (c * jax.nn.sigmoid(c)).astype(BF16)
    o_ref[...] = _dot(ca, w_ref[...].astype(BF16)) + b_ref[...]


def _ada(c, w, b):
    bsz, d = c.shape
    n = w.shape[1]
    tn = 1024
    return pl.pallas_call(
        _ada_kernel,
        grid=(n // tn,),
        in_specs=[
            pl.BlockSpec((bsz, d), lambda j: (0, 0)),
            pl.BlockSpec((d, tn), lambda j: (0, j)),
            pl.BlockSpec((1, tn), lambda j: (0, j)),
        ],
        out_specs=pl.BlockSpec((bsz, tn), lambda j: (0, j)),
        out_shape=jax.ShapeDtypeStruct((bsz, n), F32),
        compiler_params=_params(("arbitrary",)),
        name="ada_mod",
    )(c, w, b.reshape(1, n))


C_Q = (0, D_ATT)
C_CMP = (C_Q[1], C_Q[1] + 2 * D_KV)
C_K = (C_CMP[1], C_CMP[1] + 2 * D_KV)
C_V = (C_K[1], C_K[1] + 2 * D_KV)
C_R = (C_V[1], C_V[1] + 2 * D_RNN)
C_GL = (C_R[1], C_R[1] + GL_PAD)
D_IN_PAD = C_GL[1]
K_ROW = LANES
V_ROWS = HEAD_DIM + 16
N_GATE = HEADS_PER_GROUP * N_BRANCH


def _w_in_prep_kernel(wt_ref, o_ref):
    seg = lambda lo, hi: wt_ref[0, lo:hi, :].T.astype(BF16)
    kv0 = D_ATT
    gl0 = D_ATT + 6 * D_KV
    n_gl = N_BRANCH * N_Q_HEADS
    o_ref[:, 0:C_K[0] + D_KV] = seg(0, kv0 + 3 * D_KV)
    o_ref[:, C_K[0] + D_KV:C_K[1]] = seg(kv0 + 4 * D_KV, kv0 + 5 * D_KV)
    o_ref[:, C_V[0]:C_V[0] + D_KV] = seg(kv0 + 3 * D_KV, kv0 + 4 * D_KV)
    o_ref[:, C_V[0] + D_KV:C_V[1]] = seg(kv0 + 5 * D_KV, kv0 + 6 * D_KV)
    o_ref[:, C_R[0]:C_R[1]] = seg(gl0 + n_gl, gl0 + n_gl + 2 * D_RNN)
    gl_tile = wt_ref[0, gl0:gl0 + GL_PAD, :].T
    lane = lax.broadcasted_iota(jnp.int32, gl_tile.shape, 1)
    o_ref[:, C_GL[0]:C_GL[1]] = jnp.where(lane < n_gl, gl_tile, 0.0).astype(BF16)


def _w_in_prep(w_in_all, layer, tk=256):
    _, d, n = w_in_all.shape
    w_t = jnp.swapaxes(w_in_all, 1, 2)
    return pl.pallas_call(
        _w_in_prep_kernel,
        grid=(d // tk,),
        in_specs=[pl.BlockSpec((1, n, tk), lambda i: (layer, 0, i))],
        out_specs=pl.BlockSpec((tk, D_IN_PAD), lambda i: (i, 0)),
        out_shape=jax.ShapeDtypeStruct((d, D_IN_PAD), BF16),
        compiler_params=_params(("arbitrary",)),
        name="w_in_prep",
    )(w_t)


def _inproj_kernel(x_ref, g_ref, sc_ref, sh_ref, w_ref,
                   qt_ref, cmp_ref, ks_ref, kw_ref, vst_ref, vwt_ref, r_ref, glt_ref):
    tm = x_ref.shape[1]
    ng, hg, dh = N_KV_GROUPS, HEADS_PER_GROUP, HEAD_DIM
    x = x_ref[0]
    h = _rms(x, g_ref[...]) * (1.0 + sc_ref[0, 0]) + sh_ref[0, 0]
    hb = h.astype(BF16)
    tiles = [slice(u * TQ, (u + 1) * TQ) for u in range(tm // TQ)]

    q_t = (_dot(hb, w_ref[:, C_Q[0]:C_Q[1]]) * (dh ** -0.5 * LOG2_E)).T.astype(BF16)
    for j in range(ng):
        for u, cols in enumerate(tiles):
            qt_ref[0, j, u] = jnp.concatenate(
                [q_t[(hg * j + hh) * dh:(hg * j + hh + 1) * dh, cols] for hh in range(hg)], axis=1)

    kv_cmp = _dot(hb, w_ref[:, C_CMP[0]:C_CMP[1]])
    for c in range(2 * D_KV // LANES):
        cmp_ref[0, c] = kv_cmp[:, c * LANES:(c + 1) * LANES]

    keys = _dot(hb, w_ref[:, C_K[0]:C_K[1]])
    row_pos = pl.program_id(1) * tm + lax.broadcasted_iota(jnp.int32, (tm, K_ROW - dh), 0)
    lane = lax.broadcasted_iota(jnp.int32, (tm, K_ROW - dh), 1)
    onehot = jnp.where(lax.shift_right_logical(row_pos, SEL_SHIFT) == lane, 1.0, 0.0)
    for j in range(ng):
        ks_ref[0, j] = jnp.concatenate([keys[:, j * dh:(j + 1) * dh], onehot], axis=1).astype(BF16)
        kw_ref[0, j] = jnp.concatenate(
            [keys[:, D_KV + j * dh:D_KV + (j + 1) * dh], onehot], axis=1).astype(BF16)

    v_t = _dot(hb, w_ref[:, C_V[0]:C_V[1]]).T.astype(BF16)
    ones = jnp.ones((V_ROWS - dh, TQ), BF16)
    for j in range(ng):
        for u, cols in enumerate(tiles):
            vst_ref[0, j, u] = jnp.concatenate([v_t[j * dh:(j + 1) * dh, cols], ones], axis=0)
            vwt_ref[0, j, u] = jnp.concatenate([v_t[D_KV + j * dh:D_KV + (j + 1) * dh, cols], ones], axis=0)

    half = (C_R[0] + C_R[1]) // 2
    r_ref[0, :, 0:D_RNN] = _dot(hb, w_ref[:, C_R[0]:half])
    r_ref[0, :, D_RNN:2 * D_RNN] = _dot(hb, w_ref[:, half:C_R[1]])

    gl_t = _dot(hb, w_ref[:, C_GL[0]:C_GL[1]]).T
    for j in range(ng):
        glt_ref[0, j] = gl_t[N_GATE * j:N_GATE * (j + 1), :]


def _inproj(x, g, mod6, w_perm, tm=512):
    bsz, s, d = x.shape
    ng = N_KV_GROUPS
    nt = tm // TQ
    row = lambda width: pl.BlockSpec((1, tm, width), lambda b, i: (b, i, 0))
    grp_rows = pl.BlockSpec((1, ng, tm, K_ROW), lambda b, i: (b, 0, i, 0))
    grp_tiles = lambda r, c: pl.BlockSpec((1, ng, nt, r, c), lambda b, i: (b, 0, i, 0, 0))
    sds = jax.ShapeDtypeStruct
    return pl.pallas_call(
        _inproj_kernel,
        grid=(bsz, s // tm),
        in_specs=[
            row(d),
            pl.BlockSpec((1, d), lambda b, i: (0, 0)),
            pl.BlockSpec((1, 1, 1, d), lambda b, i: (1, b, 0, 0)),
            pl.BlockSpec((1, 1, 1, d), lambda b, i: (0, b, 0, 0)),
            pl.BlockSpec((d, D_IN_PAD), lambda b, i: (0, 0), pipeline_mode=pl.Buffered(1)),
        ],
        out_specs=[
            grp_tiles(HEAD_DIM, HEADS_PER_GROUP * TQ),
            pl.BlockSpec((1, 2 * D_KV // LANES, tm, LANES), lambda b, i: (b, 0, i, 0)),
            grp_rows, grp_rows,
            grp_tiles(V_ROWS, TQ), grp_tiles(V_ROWS, TQ), row(2 * D_RNN),
            pl.BlockSpec((1, ng, N_GATE, tm), lambda b, i: (b, 0, 0, i)),
        ],
        out_shape=[
            sds((bsz, ng, s // TQ, HEAD_DIM, HEADS_PER_GROUP * TQ), BF16),
            sds((bsz, 2 * D_KV // LANES, s, LANES), F32),
            sds((bsz, ng, s, K_ROW), BF16), sds((bsz, ng, s, K_ROW), BF16),
            sds((bsz, ng, s // TQ, V_ROWS, TQ), BF16), sds((bsz, ng, s // TQ, V_ROWS, TQ), BF16),
            sds((bsz, s, 2 * D_RNN), F32),
            sds((bsz, ng, N_GATE, s), F32),
        ],
        compiler_params=_params(("parallel", "arbitrary")),
        name="in_proj",
    )(x, g.reshape(1, d), mod6, mod6, w_perm)


def _compress_kernel(x_ref, pe_ref, w1_ref, w2_ref, o_ref):
    bsz = x_ref.shape[0]
    dh = HEAD_DIM
    n_rows = x_ref.shape[2] // CMP_STRIDE
    rows = bsz * n_rows
    first = jnp.zeros((rows, w1_ref.shape[4]), F32)
    second = jnp.zeros((rows, w1_ref.shape[4]), F32)
    for lp in range(CMP_STRIDE // 2):
        x_2 = jnp.concatenate(
            [x_ref[:, 0, pl.ds(2 * lp + k, n_rows, stride=CMP_STRIDE), :].reshape(rows, LANES) for k in range(2)],
            axis=1)
        first = first + _dot((x_2 + pe_ref[0, 0, lp:lp + 1, :]).astype(BF16), w1_ref[0, 0, lp])
        second = second + _dot((x_2 + pe_ref[0, 1, lp:lp + 1, :]).astype(BF16), w1_ref[0, 1, lp])
    hid = first + pltpu.roll(second, rows - 1, axis=0)
    out = _dot(jax.nn.gelu(hid).astype(BF16), w2_ref[0]).astype(BF16)
    for b in range(bsz):
        for j in range(2):
            o_ref[0, b, j] = out[b * n_rows:(b + 1) * n_rows, j * dh:(j + 1) * dh]


def _compress(kvc4, pe, w1, w2):
    bsz, n_tiles, s, _ = kvc4.shape
    n_rows = s // CMP_STRIDE
    per_kind = lambda a: pl.BlockSpec((1,) + a.shape[1:], lambda t, p: (t,) + (0,) * (a.ndim - 1))
    return pl.pallas_call(
        _compress_kernel,
        grid=(2, n_tiles // 2),
        in_specs=[pl.BlockSpec((bsz, 1, s, LANES), lambda t, p: (0, 2 * t + p, 0, 0)),
                  per_kind(pe), per_kind(w1), per_kind(w2)],
        out_specs=pl.BlockSpec((1, bsz, 2, n_rows, HEAD_DIM), lambda t, p: (t, 0, p, 0, 0)),
        out_shape=jax.ShapeDtypeStruct((2, bsz, N_KV_GROUPS, n_rows, HEAD_DIM), BF16),
        compiler_params=_params(("arbitrary", "arbitrary")),
        name="compress_kv",
    )(kvc4, pe, w1, w2)


def _cmp_select_kernel(qt_ref, kc_ref, vc_ref, oc_ref, bias_ref):
    hg = HEADS_PER_GROUP
    kc = kc_ref[0, 0, 0]
    vct = vc_ref[0, 0, 0].astype(F32).T.astype(BF16)
    jj = lax.broadcasted_iota(jnp.int32, (N_SEL, N_CMP_PAD), 0) * SEL_BLOCK
    nn = lax.broadcasted_iota(jnp.int32, (N_SEL, N_CMP_PAD), 1) * CMP_STRIDE
    ov = jnp.minimum(nn + CMP_BLOCK, jj + SEL_BLOCK) - jnp.maximum(nn, jj)
    w_sel = (jnp.maximum(ov, 0).astype(F32) * (1.0 / CMP_BLOCK)).astype(BF16)
    nrow = lax.broadcasted_iota(jnp.int32, (N_CMP_PAD, TQ), 0)
    blk = lax.broadcasted_iota(jnp.int32, (N_SEL, TQ), 0)

    for qi in range(qt_ref.shape[2]):
        qt = qt_ref[0, 0, qi]
        pos = qi * TQ + lax.broadcasted_iota(jnp.int32, (1, TQ), 1)
        mask_c = nrow * CMP_STRIDE + (CMP_BLOCK - 1) <= pos

        def per_head_where(a, fill):
            return jnp.concatenate(
                [jnp.where(mask_c, a[:, h * TQ:(h + 1) * TQ], fill) for h in range(hg)], axis=1)

        s = per_head_where(_dot(kc, qt), NEG)
        e = jnp.exp2(s - jnp.max(s, axis=0, keepdims=True))
        p = per_head_where(e * (1.0 / jnp.sum(e, axis=0, keepdims=True)), 0.0)
        oc_ref[0, 0, qi] = _dot(vct, p.astype(BF16))
        psum = p[:, 0:TQ]
        for h in range(1, hg):
            psum = psum + p[:, h * TQ:(h + 1) * TQ]

        p_hi = psum.astype(BF16)
        p_lo = (psum - p_hi.astype(F32)).astype(BF16)
        imp = _dot(w_sel, p_hi) + _dot(w_sel, p_lo)
        cur = lax.shift_right_logical(pos, SEL_SHIFT)
        free = jnp.where(blk * SEL_BLOCK <= pos, imp, -FORCE_SCORE)
        score = jnp.where(blk == 0, FORCE_SCORE,
                          jnp.where(blk == cur, FORCE_SCORE, jnp.where(blk == cur - 1, FORCE_SCORE, free)))
        rank = jnp.zeros((N_SEL, TQ), F32)
        for k in range(N_SEL):
            sk = score[k:k + 1, :]
            tie = jnp.where(blk > k, 1.0, 0.0)
            rank = rank + jnp.where(sk > score, 1.0, jnp.where(sk == score, tie, 0.0))
        bias_ref[0, 0, qi] = jnp.where(rank < SEL_TOP_N, 0.0, NEG).astype(BF16)


def _cmp_select(qt, kv_cmp):
    bsz, g, nq, dh, wide = qt.shape
    per_bg = lambda *shape: pl.BlockSpec((1, 1) + shape, lambda b, j: (b, j) + (0,) * len(shape))
    kind = lambda t: pl.BlockSpec((1, 1, 1, N_CMP_PAD, dh), lambda b, j: (t, b, j, 0, 0))
    return pl.pallas_call(
        _cmp_select_kernel,
        grid=(bsz, g),
        in_specs=[per_bg(nq, dh, wide), kind(0), kind(1)],
        out_specs=[per_bg(nq, dh, wide), per_bg(nq, N_SEL, TQ)],
        out_shape=[jax.ShapeDtypeStruct((bsz, g, nq, dh, wide), F32),
                   jax.ShapeDtypeStruct((bsz, g, nq, N_SEL, TQ), BF16)],
        compiler_params=_params(("parallel", "arbitrary")),
        name="cmp_select",
    )(qt, kv_cmp, kv_cmp)


def _attn_kernel(qt_ref, oc_ref, bias_ref, ks_ref, kw_ref, vst_ref, vwt_ref, gl_ref, o_ref,
                 m_ref, acc_ref, sc_ref):
    qi = pl.program_id(2)
    last = pl.num_programs(2) - 1
    hg = HEADS_PER_GROUP
    dh = HEAD_DIM
    wide = hg * TQ
    qt = qt_ref[0, 0, 0]

    def per_head_where(mask, a, fill):
        return jnp.concatenate(
            [jnp.where(mask, a[:, h * TQ:(h + 1) * TQ], fill) for h in range(hg)], axis=1)

    def block_bias(value):
        return jnp.full((N_SEL, wide), value, F32).astype(BF16)

    zero_rows = jnp.zeros((K_ROW - dh - N_SEL, wide), BF16)

    def scores(k_ref, kt, bias):
        k_t = k_ref[0, 0, pl.ds(pl.multiple_of(kt * TQ, TQ), TQ), :]
        return _dot(k_t, jnp.concatenate([qt, bias, zero_rows], axis=0))

    rel = (lax.broadcasted_iota(jnp.int32, (TQ, TQ), 0)
           - lax.broadcasted_iota(jnp.int32, (TQ, TQ), 1))

    def online(state, sc, v_t):
        m_tile = jnp.max(sc, axis=0, keepdims=True)
        if state is None:
            return m_tile, _dot(v_t, jnp.exp2(sc - m_tile).astype(BF16))
        m_old, acc_old = state
        m_new = jnp.maximum(m_old, m_tile)
        return m_new, jnp.exp2(m_old - m_new) * acc_old + _dot(v_t, jnp.exp2(sc - m_new).astype(BF16))

    k1 = jnp.maximum(qi - 1, 0)
    k2 = jnp.maximum(qi - 2, 0)
    s0 = per_head_where(rel <= 0, scores(kw_ref, qi, block_bias(0.0)), NEG)
    s1 = scores(kw_ref, k1, block_bias(jnp.where(qi >= 1, 0.0, NEG)))
    s2 = per_head_where(rel > 0, scores(kw_ref, k2, block_bias(jnp.where(qi >= 2, 0.0, NEG))), NEG)
    win = online(None, s0, vwt_ref[0, 0, qi])
    win = online(win, s1, vwt_ref[0, 0, k1])
    win = online(win, s2, vwt_ref[0, 0, k2])
    acc_ref[1] = win[1]

    bias_sel = jnp.concatenate([bias_ref[0, 0, 0]] * hg, axis=1)

    sd = per_head_where(rel <= 0, scores(ks_ref, qi, bias_sel), NEG)
    m_ref[...], acc_ref[0] = online(None, sd, vst_ref[0, 0, qi])

    def sel_scores(kt):
        pad_bias = block_bias(jnp.where(kt < qi, 0.0, NEG))
        return scores(ks_ref, jnp.minimum(kt, last), jnp.minimum(bias_sel, pad_bias))

    def sel_accumulate(sc, kt):
        m_ref[...], acc_ref[0] = online((m_ref[...], acc_ref[0]), sc, vst_ref[0, 0, jnp.minimum(kt, last)])

    sc_ref[0] = sel_scores(0)

    def sel_pair(pi, carry):
        kt = 2 * pi
        sc_ref[1] = sel_scores(kt + 1)
        sel_accumulate(sc_ref[0], kt)
        sc_ref[0] = sel_scores(kt + 2)
        sel_accumulate(sc_ref[1], kt + 1)
        return carry

    lax.fori_loop(0, (qi + 1) // 2, sel_pair, 0)

    gate = jax.nn.sigmoid(gl_ref[0, 0])
    o_cmp = oc_ref[0, 0, 0]
    acc_s = acc_ref[0]
    acc_w = acc_ref[1]
    o_sel = acc_s[0:dh, :] * (1.0 / acc_s[dh:dh + 1, :])
    o_win = acc_w[0:dh, :] * (1.0 / acc_w[dh:dh + 1, :])
    outs = []
    for h in range(hg):
        lanes = slice(h * TQ, (h + 1) * TQ)
        g0 = gate[3 * h:3 * h + 1, :]
        g1 = gate[3 * h + 1:3 * h + 2, :]
        g2 = gate[3 * h + 2:3 * h + 3, :]
        outs.append(g0 * o_cmp[:, lanes] + g1 * o_sel[:, lanes] + g2 * o_win[:, lanes])
    o_ref[0] = jnp.concatenate(outs, axis=0).T


def _attention(qt, o_cmp, bias, ks, kw, vst, vwt, glt):
    bsz, g, nq = qt.shape[:3]
    s = nq * TQ
    width = HEADS_PER_GROUP * HEAD_DIM
    wide = HEADS_PER_GROUP * TQ
    per_bg = lambda *shape: pl.BlockSpec((1, 1) + shape, lambda b, j, i: (b, j) + (0,) * len(shape))
    per_tile = lambda *shape: pl.BlockSpec((1, 1, 1) + shape, lambda b, j, i: (b, j, i) + (0,) * len(shape))
    return pl.pallas_call(
        _attn_kernel,
        grid=(bsz, g, nq),
        in_specs=[
            per_tile(HEAD_DIM, wide),
            per_tile(HEAD_DIM, wide),
            per_tile(N_SEL, TQ),
            per_bg(s, K_ROW),
            per_bg(s, K_ROW),
            per_bg(nq, V_ROWS, TQ),
            per_bg(nq, V_ROWS, TQ),
            pl.BlockSpec((1, 1, HEADS_PER_GROUP * N_BRANCH, TQ), lambda b, j, i: (b, j, 0, i)),
        ],
        out_specs=pl.BlockSpec((1, TQ, width), lambda b, j, i: (b, i, j)),
        out_shape=jax.ShapeDtypeStruct((bsz, s, D_ATT), F32),
        scratch_shapes=[
            pltpu.VMEM((1, wide), F32),
            pltpu.VMEM((2, V_ROWS, wide), F32),
            pltpu.VMEM((2, TQ, wide), F32),
        ],
        compiler_params=_params(("parallel", "parallel", "arbitrary")),
        name="nsa_attention",
    )(qt, o_cmp, bias, ks, kw, vst, vwt, glt)


def _rglru_kernel(x_ref, y_ref, cw_ref, cb_ref, w_ref, b_ref, lam_ref, o_ref, xpad_ref):
    s = x_ref.shape[1]
    c = RNN_BLOCK_DIM
    pad = SUBLANES
    xpad_ref[0:pad, :] = jnp.zeros((pad, c), F32)
    xpad_ref[pad:pad + s, :] = x_ref[0]
    cw = cw_ref[...]
    w = w_ref[0]
    bias = b_ref[0]
    lam = lam_ref[...]
    neg_softplus = -(jnp.maximum(-lam, 0.0) + jnp.log1p(jnp.exp(-jnp.abs(lam))))
    sub = lax.broadcasted_iota(jnp.int32, (TT, c), 0) & (SUBLANES - 1)
    h = jnp.zeros((1, c), F32)

    for ci in range(s // TT):
        t0 = ci * TT
        xc = cb_ref[...] + sum(
            xpad_ref[t0 + pad - (CONV_WIDTH - 1) + k:t0 + pad - (CONV_WIDTH - 1) + k + TT, :] * cw[k:k + 1, :]
            for k in range(CONV_WIDTH))
        gates = jax.nn.sigmoid(_dot(xc.astype(BF16), w) + bias)
        r = gates[:, 0:c]
        i = gates[:, c:2 * c]
        log_a = LRU_C * r * neg_softplus
        a = jnp.exp(log_a)
        var = -jnp.tanh(log_a) * (a * a + 1.0)
        bt = jnp.where(var > 0.0, var * lax.rsqrt(var), 0.0) * (i * xc)
        for d in (1, 2, 4):
            keep = sub >= d
            a_prev = jnp.where(keep, pltpu.roll(a, d, axis=0), 1.0)
            b_prev = jnp.where(keep, pltpu.roll(bt, d, axis=0), 0.0)
            bt = bt + a * b_prev
            a = a * a_prev
        states = []
        for gi in range(TT // SUBLANES):
            grp = slice(gi * SUBLANES, (gi + 1) * SUBLANES)
            hg = bt[grp, :] + a[grp, :] * h
            states.append(hg)
            h = hg[SUBLANES - 1:SUBLANES, :]
        o_ref[0, t0:t0 + TT, :] = jax.nn.gelu(y_ref[0, t0:t0 + TT, :]) * jnp.concatenate(states, axis=0)


def _rglru(r_in, conv_w, conv_b, w_cat, b_cat, lam):
    bsz, s, _ = r_in.shape
    c = RNN_BLOCK_DIM
    nb = RNN_BLOCKS
    return pl.pallas_call(
        _rglru_kernel,
        grid=(bsz, nb),
        in_specs=[
            pl.BlockSpec((1, s, c), lambda b, j: (b, 0, j)),
            pl.BlockSpec((1, s, c), lambda b, j: (b, 0, nb + j)),
            pl.BlockSpec((CONV_WIDTH, c), lambda b, j: (0, j)),
            pl.BlockSpec((1, c), lambda b, j: (0, j)),
            pl.BlockSpec((1, c, 2 * c), lambda b, j: (j, 0, 0)),
            pl.BlockSpec((1, 1, 2 * c), lambda b, j: (j, 0, 0)),
            pl.BlockSpec((1, c), lambda b, j: (0, j)),
        ],
        out_specs=pl.BlockSpec((1, s, c), lambda b, j: (b, 0, j)),
        out_shape=jax.ShapeDtypeStruct((bsz, s, D_RNN), F32),
        scratch_shapes=[pltpu.VMEM((s + SUBLANES, c), F32)],
        compiler_params=_params(("parallel", "arbitrary")),
        name="rg_lru",
    )(r_in, r_in, conv_w, conv_b, w_cat, b_cat, lam)


def _outproj_kernel(oa_ref, or_ref, ga_ref, gr_ref, w_ref, x_ref, gt_ref, gp_ref, o_ref):
    a = _rms(oa_ref[0], ga_ref[...]).astype(BF16)
    r = _rms(or_ref[0], gr_ref[...]).astype(BF16)
    mix = _dot(a, w_ref[0:D_ATT, :]) + _dot(r, w_ref[D_ATT:D_ATT + D_RNN, :])
    o_ref[0] = x_ref[0] + gt_ref[0, 0] * _rms(mix, gp_ref[...])


def _outproj(o_att, o_rnn, g_att, g_rnn, w_out, x, mod6, g_post, tm=512):
    bsz, s, d = x.shape
    row = lambda width: pl.BlockSpec((1, tm, width), lambda b, i: (b, i, 0))
    vec = lambda width: pl.BlockSpec((1, width), lambda b, i: (0, 0))
    return pl.pallas_call(
        _outproj_kernel,
        grid=(bsz, s // tm),
        in_specs=[
            row(D_ATT), row(D_RNN), vec(D_ATT), vec(D_RNN),
            pl.BlockSpec((D_ATT + D_RNN, d), lambda b, i: (0, 0), pipeline_mode=pl.Buffered(1)),
            row(d),
            pl.BlockSpec((1, 1, 1, d), lambda b, i: (2, b, 0, 0)),
            vec(d),
        ],
        out_specs=row(d),
        out_shape=jax.ShapeDtypeStruct((bsz, s, d), F32),
        compiler_params=_params(("parallel", "arbitrary")),
        name="out_proj",
    )(o_att, o_rnn, g_att.reshape(1, -1), g_rnn.reshape(1, -1), w_out, x, mod6, g_post.reshape(1, d))


def _mlp_kernel(x_ref, g_ref, sc_ref, sh_ref, w1_ref, w2_ref, gt_ref, gp_ref, o_ref, h_ref, a_ref):
    j = pl.program_id(2)
    n_chunks = pl.num_programs(2) - 1
    slot = j % 2

    def up():
        u = jnp.maximum(_dot(h_ref[...], w1_ref[...]), 0.0)
        a_ref[slot] = (u * u).astype(BF16)

    def down():
        o_ref[0] += _dot(a_ref[1 - slot], w2_ref[...])

    @pl.when(j == 0)
    def _():
        h = _rms(x_ref[0], g_ref[...]) * (1.0 + sc_ref[0, 0]) + sh_ref[0, 0]
        h_ref[...] = h.astype(BF16)
        o_ref[0] = jnp.zeros(o_ref.shape[1:], F32)
        up()

    @pl.when((j > 0) & (j < n_chunks))
    def _():
        down()
        up()

    @pl.when(j == n_chunks)
    def _():
        down()
        o_ref[0] = x_ref[0] + gt_ref[0, 0] * _rms(o_ref[0], gp_ref[...])


def _mlp(x, g_pre, mod6, w1, w2, g_post, tm=1024, tf=1024):
    bsz, s, d = x.shape
    n_chunks = w1.shape[1] // tf
    row = pl.BlockSpec((1, tm, d), lambda b, i, j: (b, i, 0))
    vec = pl.BlockSpec((1, d), lambda b, i, j: (0, 0))
    modk = lambda k: pl.BlockSpec((1, 1, 1, d), lambda b, i, j: (k, b, 0, 0))
    return pl.pallas_call(
        _mlp_kernel,
        grid=(bsz, s // tm, n_chunks + 1),
        in_specs=[
            row, vec, modk(4), modk(3),
            pl.BlockSpec((d, tf), lambda b, i, j: (0, jnp.minimum(j, n_chunks - 1))),
            pl.BlockSpec((tf, d), lambda b, i, j: (jnp.maximum(j - 1, 0), 0)),
            modk(5), vec,
        ],
        out_specs=row,
        out_shape=jax.ShapeDtypeStruct((bsz, s, d), F32),
        scratch_shapes=[pltpu.VMEM((tm, d), BF16), pltpu.VMEM((2, tm, tf), BF16)],
        compiler_params=_params(("parallel", "parallel", "arbitrary")),
        name="mlp",
    )(x, g_pre.reshape(1, d), mod6, mod6, w1, w2, mod6, g_post.reshape(1, d))


def _layer(x, c, w_ada, b_ada, g_pre_mix, g_post_mix, g_pre_mlp, g_post_mlp, w_in_perm,
           cmp_w1_k, cmp_w2_k, cmp_pe_k, cmp_w1_v, cmp_w2_v, cmp_pe_v,
           conv_w, conv_b, w_rg_a, b_rg_a, w_rg_x, b_rg_x, lru_lambda,
           g_grp_att, g_grp_rnn, w_out, w_ff1, w_ff2):
    bsz, s, d = x.shape
    dh = HEAD_DIM

    mod = _ada(c, w_ada, b_ada)
    mod6 = mod.reshape(bsz, 6, 1, d).transpose(1, 0, 2, 3)

    qt, kvc, ks, kw, vst, vwt, r_in, glt = _inproj(x, g_pre_mix, mod6, w_in_perm)

    def pair_diag(w, axis):
        z = jnp.zeros_like(w)
        return jnp.concatenate([jnp.concatenate([w, z], axis=-1), jnp.concatenate([z, w], axis=-1)], axis=axis)

    n_pair = CMP_STRIDE // 2
    pe = jnp.tile(jnp.stack([cmp_pe_k, cmp_pe_v]), (1, 1, 2)).reshape(2, 2, n_pair, 4 * dh)
    w1c = pair_diag(jnp.stack([cmp_w1_k, cmp_w1_v]).astype(BF16).reshape(2, CMP_BLOCK, dh, -1), 2)
    w1c = w1c.reshape(2, 2, n_pair, 4 * dh, w1c.shape[-1])
    w2c = pair_diag(jnp.stack([cmp_w2_k, cmp_w2_v]).astype(BF16), 1)
    kv_cmp = _compress(kvc, pe, w1c, w2c)

    o_cmp, sel_bias = _cmp_select(qt, kv_cmp)
    o_att = _attention(qt, o_cmp, sel_bias, ks, kw, vst, vwt, glt)

    w_cat = jnp.concatenate([w_rg_a, w_rg_x], axis=-1).astype(BF16)
    b_cat = jnp.concatenate([b_rg_a.reshape(RNN_BLOCKS, 1, RNN_BLOCK_DIM),
                             b_rg_x.reshape(RNN_BLOCKS, 1, RNN_BLOCK_DIM)], axis=-1)
    o_rnn = _rglru(r_in, conv_w, conv_b.reshape(1, -1), w_cat, b_cat, lru_lambda.reshape(1, -1))

    x1 = _outproj(o_att, o_rnn, g_grp_att, g_grp_rnn, w_out.astype(BF16), x, mod6, g_post_mix)
    return _mlp(x1, g_pre_mlp, mod6, w_ff1.astype(BF16), w_ff2.astype(BF16), g_post_mlp)


def kernel(x, c, w_ada, b_ada, g_pre_mix, g_post_mix, g_pre_mlp, g_post_mlp, w_in, cmp_w1_k, cmp_w2_k, cmp_pe_k, cmp_w1_v, cmp_w2_v, cmp_pe_v, conv_w, conv_b, w_rg_a, b_rg_a, w_rg_x, b_rg_x, lru_lambda, g_grp_att, g_grp_rnn, w_out, w_ff1, w_ff2):
    depth = w_ada.shape[0]
    for l in range(depth):
        x = _layer(x, c, w_ada[l], b_ada[l], g_pre_mix[l], g_post_mix[l], g_pre_mlp[l], g_post_mlp[l],
                   _w_in_prep(w_in, l), cmp_w1_k[l], cmp_w2_k[l], cmp_pe_k[l], cmp_w1_v[l], cmp_w2_v[l], cmp_pe_v[l],
                   conv_w[l], conv_b[l], w_rg_a[l], b_rg_a[l], w_rg_x[l], b_rg_x[l], lru_lambda[l],
                   g_grp_att[l], g_grp_rnn[l], w_out[l], w_ff1[l], w_ff2[l])
    return x
```

```python
import jax
import jax.numpy as jnp
from jax import lax
from jax.experimental import pallas as pl
from jax.experimental.pallas import tpu as pltpu

F32 = jnp.float32
BF16 = jnp.bfloat16

D_MODEL = 2048
D_ATT = 1024
D_RNN = 1024
N_Q_HEADS = 16
N_KV_GROUPS = 4
HEADS_PER_GROUP = 4
HEAD_DIM = 64
D_KV = 256
CMP_BLOCK = 32
CMP_STRIDE = 16
SEL_BLOCK = 64
SEL_SHIFT = 6
SEL_TOP_N = 8
WINDOW = 512
N_BRANCH = 3
RNN_BLOCKS = 8
RNN_BLOCK_DIM = 128
CONV_WIDTH = 4
LRU_C = 8.0
D_FF = 4 * D_MODEL
EPS = 1e-6
NEG = -1e30
FORCE_SCORE = 1e9
LOG2_E = 1.4426950408889634

LANES = 128
SUBLANES = 8
VMEM_LIMIT = 60 * 1024 * 1024

TQ = 256
N_CMP_PAD = 128
N_SEL = 32
GL_PAD = 128
TT = 128
assert WINDOW == 2 * TQ


def _params(sem):
    return pltpu.CompilerParams(dimension_semantics=sem, vmem_limit_bytes=VMEM_LIMIT)


def _dot(a, b):
    return jnp.dot(a, b, preferred_element_type=F32)


def _rms(x, g):
    return x * lax.rsqrt(jnp.mean(x * x, axis=-1, keepdims=True) + EPS) * g


def _ada_kernel(c_ref, w_ref, b_ref, o_ref):
    c = c_ref[...]
    ca = (c * jax.nn.sigmoid(c)).astype(BF16)
    o_ref[...] = _dot(ca, w_ref[...].astype(BF16)) + b_ref[...]


def _ada(c, w, b):
    bsz, d = c.shape
    n = w.shape[1]
    tn = 1024
    return pl.pallas_call(
        _ada_kernel,
        grid=(n // tn,),
        in_specs=[
            pl.BlockSpec((bsz, d), lambda j: (0, 0)),
            pl.BlockSpec((d, tn), lambda j: (0, j)),
            pl.BlockSpec((1, tn), lambda j: (0, j)),
        ],
        out_specs=pl.BlockSpec((bsz, tn), lambda j: (0, j)),
        out_shape=jax.ShapeDtypeStruct((bsz, n), F32),
        compiler_params=_params(("arbitrary",)),
        name="ada_mod",
    )(c, w, b.reshape(1, n))


C_Q = (0, D_ATT)
C_CMP = (C_Q[1], C_Q[1] + 2 * D_KV)
C_K = (C_CMP[1], C_CMP[1] + 2 * D_KV)
C_V = (C_K[1], C_K[1] + 2 * D_KV)
C_R = (C_V[1], C_V[1] + 2 * D_RNN)
C_GL = (C_R[1], C_R[1] + GL_PAD)
D_IN_PAD = C_GL[1]
K_ROW = LANES
V_ROWS = HEAD_DIM + 16
N_GATE = HEADS_PER_GROUP * N_BRANCH


def _w_in_prep_kernel(wt_ref, o_ref):
    seg = lambda lo, hi: wt_ref[0, lo:hi, :].T.astype(BF16)
    kv0 = D_ATT
    gl0 = D_ATT + 6 * D_KV
    n_gl = N_BRANCH * N_Q_HEADS
    o_ref[:, 0:C_K[0] + D_KV] = seg(0, kv0 + 3 * D_KV)
    o_ref[:, C_K[0] + D_KV:C_K[1]] = seg(kv0 + 4 * D_KV, kv0 + 5 * D_KV)
    o_ref[:, C_V[0]:C_V[0] + D_KV] = seg(kv0 + 3 * D_KV, kv0 + 4 * D_KV)
    o_ref[:, C_V[0] + D_KV:C_V[1]] = seg(kv0 + 5 * D_KV, kv0 + 6 * D_KV)
    o_ref[:, C_R[0]:C_R[1]] = seg(gl0 + n_gl, gl0 + n_gl + 2 * D_RNN)
    gl_tile = wt_ref[0, gl0:gl0 + GL_PAD, :].T
    lane = lax.broadcasted_iota(jnp.int32, gl_tile.shape, 1)
    o_ref[:, C_GL[0]:C_GL[1]] = jnp.where(lane < n_gl, gl_tile, 0.0).astype(BF16)


def _w_in_prep(w_in_all, layer, tk=256):
    _, d, n = w_in_all.shape
    w_t = jnp.swapaxes(w_in_all, 1, 2)
    return pl.pallas_call(
        _w_in_prep_kernel,
        grid=(d // tk,),
        in_specs=[pl.BlockSpec((1, n, tk), lambda i: (layer, 0, i))],
        out_specs=pl.BlockSpec((tk, D_IN_PAD), lambda i: (i, 0)),
        out_shape=jax.ShapeDtypeStruct((d, D_IN_PAD), BF16),
        compiler_params=_params(("arbitrary",)),
        name="w_in_prep",
    )(w_t)


def _inproj_kernel(x_ref, g_ref, sc_ref, sh_ref, w_ref,
                   qt_ref, cmp_ref, ks_ref, kw_ref, vst_ref, vwt_ref, r_ref, glt_ref):
    tm = x_ref.shape[1]
    ng, hg, dh = N_KV_GROUPS, HEADS_PER_GROUP, HEAD_DIM
    x = x_ref[0]
    h = _rms(x, g_ref[...]) * (1.0 + sc_ref[0, 0]) + sh_ref[0, 0]
    hb = h.astype(BF16)
    tiles = [slice(u * TQ, (u + 1) * TQ) for u in range(tm // TQ)]

    q_t = (_dot(hb, w_ref[:, C_Q[0]:C_Q[1]]) * (dh ** -0.5 * LOG2_E)).T.astype(BF16)
    for j in range(ng):
        for u, cols in enumerate(tiles):
            qt_ref[0, j, u] = jnp.concatenate(
                [q_t[(hg * j + hh) * dh:(hg * j + hh + 1) * dh, cols] for hh in range(hg)], axis=1)

    kv_cmp = _dot(hb, w_ref[:, C_CMP[0]:C_CMP[1]])
    for c in range(2 * D_KV // LANES):
        cmp_ref[0, c] = kv_cmp[:, c * LANES:(c + 1) * LANES]

    keys = _dot(hb, w_ref[:, C_K[0]:C_K[1]])
    row_pos = pl.program_id(1) * tm + lax.broadcasted_iota(jnp.int32, (tm, K_ROW - dh), 0)
    lane = lax.broadcasted_iota(jnp.int32, (tm, K_ROW - dh), 1)
    onehot = jnp.where(lax.shift_right_logical(row_pos, SEL_SHIFT) == lane, 1.0, 0.0)
    for j in range(ng):
        ks_ref[0, j] = jnp.concatenate([keys[:, j * dh:(j + 1) * dh], onehot], axis=1).astype(BF16)
        kw_ref[0, j] = jnp.concatenate(
            [keys[:, D_KV + j * dh:D_KV + (j + 1) * dh], onehot], axis=1).astype(BF16)

    v_t = _dot(hb, w_ref[:, C_V[0]:C_V[1]]).T.astype(BF16)
    ones = jnp.ones((V_ROWS - dh, TQ), BF16)
    for j in range(ng):
        for u, cols in enumerate(tiles):
            vst_ref[0, j, u] = jnp.concatenate([v_t[j * dh:(j + 1) * dh, cols], ones], axis=0)
            vwt_ref[0, j, u] = jnp.concatenate([v_t[D_KV + j * dh:D_KV + (j + 1) * dh, cols], ones], axis=0)

    half = (C_R[0] + C_R[1]) // 2
    r_ref[0, :, 0:D_RNN] = _dot(hb, w_ref[:, C_R[0]:half])
    r_ref[0, :, D_RNN:2 * D_RNN] = _dot(hb, w_ref[:, half:C_R[1]])

    gl_t = _dot(hb, w_ref[:, C_GL[0]:C_GL[1]]).T
    for j in range(ng):
        glt_ref[0, j] = gl_t[N_GATE * j:N_GATE * (j + 1), :]


def _inproj(x, g, mod6, w_perm, tm=512):
    bsz, s, d = x.shape
    ng = N_KV_GROUPS
    nt = tm // TQ
    row = lambda width: pl.BlockSpec((1, tm, width), lambda b, i: (b, i, 0))
    grp_rows = pl.BlockSpec((1, ng, tm, K_ROW), lambda b, i: (b, 0, i, 0))
    grp_tiles = lambda r, c: pl.BlockSpec((1, ng, nt, r, c), lambda b, i: (b, 0, i, 0, 0))
    sds = jax.ShapeDtypeStruct
    return pl.pallas_call(
        _inproj_kernel,
        grid=(bsz, s // tm),
        in_specs=[
            row(d),
            pl.BlockSpec((1, d), lambda b, i: (0, 0)),
            pl.BlockSpec((1, 1, 1, d), lambda b, i: (1, b, 0, 0)),
            pl.BlockSpec((1, 1, 1, d), lambda b, i: (0, b, 0, 0)),
            pl.BlockSpec((d, D_IN_PAD), lambda b, i: (0, 0), pipeline_mode=pl.Buffered(1)),
        ],
        out_specs=[
            grp_tiles(HEAD_DIM, HEADS_PER_GROUP * TQ),
            pl.BlockSpec((1, 2 * D_KV // LANES, tm, LANES), lambda b, i: (b, 0, i, 0)),
            grp_rows, grp_rows,
            grp_tiles(V_ROWS, TQ), grp_tiles(V_ROWS, TQ), row(2 * D_RNN),
            pl.BlockSpec((1, ng, N_GATE, tm), lambda b, i: (b, 0, 0, i)),
        ],
        out_shape=[
            sds((bsz, ng, s // TQ, HEAD_DIM, HEADS_PER_GROUP * TQ), BF16),
            sds((bsz, 2 * D_KV // LANES, s, LANES), F32),
            sds((bsz, ng, s, K_ROW), BF16), sds((bsz, ng, s, K_ROW), BF16),
            sds((bsz, ng, s // TQ, V_ROWS, TQ), BF16), sds((bsz, ng, s // TQ, V_ROWS, TQ), BF16),
            sds((bsz, s, 2 * D_RNN), F32),
            sds((bsz, ng, N_GATE, s), F32),
        ],
        compiler_params=_params(("parallel", "arbitrary")),
        name="in_proj",
    )(x, g.reshape(1, d), mod6, mod6, w_perm)


def _compress_kernel(x_ref, pe_ref, w1_ref, w2_ref, o_ref):
    bsz = x_ref.shape[0]
    dh = HEAD_DIM
    n_rows = x_ref.shape[2] // CMP_STRIDE
    rows = bsz * n_rows
    first = jnp.zeros((rows, w1_ref.shape[4]), F32)
    second = jnp.zeros((rows, w1_ref.shape[4]), F32)
    for lp in range(CMP_STRIDE // 2):
        x_2 = jnp.concatenate(
            [x_ref[:, 0, pl.ds(2 * lp + k, n_rows, stride=CMP_STRIDE), :].reshape(rows, LANES) for k in range(2)],
            axis=1)
        first = first + _dot((x_2 + pe_ref[0, 0, lp:lp + 1, :]).astype(BF16), w1_ref[0, 0, lp])
        second = second + _dot((x_2 + pe_ref[0, 1, lp:lp + 1, :]).astype(BF16), w1_ref[0, 1, lp])
    hid = first + pltpu.roll(second, rows - 1, axis=0)
    out = _dot(jax.nn.gelu(hid).astype(BF16), w2_ref[0]).astype(BF16)
    for b in range(bsz):
        for j in range(2):
            o_ref[0, b, j] = out[b * n_rows:(b + 1) * n_rows, j * dh:(j + 1) * dh]


def _compress(kvc4, pe, w1, w2):
    bsz, n_tiles, s, _ = kvc4.shape
    n_rows = s // CMP_STRIDE
    per_kind = lambda a: pl.BlockSpec((1,) + a.shape[1:], lambda t, p: (t,) + (0,) * (a.ndim - 1))
    return pl.pallas_call(
        _compress_kernel,
        grid=(2, n_tiles // 2),
        in_specs=[pl.BlockSpec((bsz, 1, s, LANES), lambda t, p: (0, 2 * t + p, 0, 0)),
                  per_kind(pe), per_kind(w1), per_kind(w2)],
        out_specs=pl.BlockSpec((1, bsz, 2, n_rows, HEAD_DIM), lambda t, p: (t, 0, p, 0, 0)),
        out_shape=jax.ShapeDtypeStruct((2, bsz, N_KV_GROUPS, n_rows, HEAD_DIM), BF16),
        compiler_params=_params(("arbitrary", "arbitrary")),
        name="compress_kv",
    )(kvc4, pe, w1, w2)


def _cmp_select_kernel(qt_ref, kc_ref, vc_ref, oc_ref, bias_ref):
    hg = HEADS_PER_GROUP
    kc = kc_ref[0, 0, 0]
    vct = vc_ref[0, 0, 0].astype(F32).T.astype(BF16)
    jj = lax.broadcasted_iota(jnp.int32, (N_SEL, N_CMP_PAD), 0) * SEL_BLOCK
    nn = lax.broadcasted_iota(jnp.int32, (N_SEL, N_CMP_PAD), 1) * CMP_STRIDE
    ov = jnp.minimum(nn + CMP_BLOCK, jj + SEL_BLOCK) - jnp.maximum(nn, jj)
    w_sel = (jnp.maximum(ov, 0).astype(F32) * (1.0 / CMP_BLOCK)).astype(BF16)
    nrow = lax.broadcasted_iota(jnp.int32, (N_CMP_PAD, TQ), 0)
    blk = lax.broadcasted_iota(jnp.int32, (N_SEL, TQ), 0)

    for qi in range(qt_ref.shape[2]):
        qt = qt_ref[0, 0, qi]
        pos = qi * TQ + lax.broadcasted_iota(jnp.int32, (1, TQ), 1)
        mask_c = nrow * CMP_STRIDE + (CMP_BLOCK - 1) <= pos

        def per_head_where(a, fill):
            return jnp.concatenate(
                [jnp.where(mask_c, a[:, h * TQ:(h + 1) * TQ], fill) for h in range(hg)], axis=1)

        s = per_head_where(_dot(kc, qt), NEG)
        e = jnp.exp2(s - jnp.max(s, axis=0, keepdims=True))
        p = per_head_where(e * (1.0 / jnp.sum(e, axis=0, keepdims=True)), 0.0)
        oc_ref[0, 0, qi] = _dot(vct, p.astype(BF16))
        psum = p[:, 0:TQ]
        for h in range(1, hg):
            psum = psum + p[:, h * TQ:(h + 1) * TQ]

        p_hi = psum.astype(BF16)
        p_lo = (psum - p_hi.astype(F32)).astype(BF16)
        imp = _dot(w_sel, p_hi) + _dot(w_sel, p_lo)
        cur = lax.shift_right_logical(pos, SEL_SHIFT)
        free = jnp.where(blk * SEL_BLOCK <= pos, imp, -FORCE_SCORE)
        score = jnp.where(blk == 0, FORCE_SCORE,
                          jnp.where(blk == cur, FORCE_SCORE, jnp.where(blk == cur - 1, FORCE_SCORE, free)))
        rank = jnp.zeros((N_SEL, TQ), F32)
        for k in range(N_SEL):
            sk = score[k:k + 1, :]
            tie = jnp.where(blk > k, 1.0, 0.0)
            rank = rank + jnp.where(sk > score, 1.0, jnp.where(sk == score, tie, 0.0))
        bias_ref[0, 0, qi] = jnp.where(rank < SEL_TOP_N, 0.0, NEG).astype(BF16)


def _cmp_select(qt, kv_cmp):
    bsz, g, nq, dh, wide = qt.shape
    per_bg = lambda *shape: pl.BlockSpec((1, 1) + shape, lambda b, j: (b, j) + (0,) * len(shape))
    kind = lambda t: pl.BlockSpec((1, 1, 1, N_CMP_PAD, dh), lambda b, j: (t, b, j, 0, 0))
    return pl.pallas_call(
        _cmp_select_kernel,
        grid=(bsz, g),
        in_specs=[per_bg(nq, dh, wide), kind(0), kind(1)],
        out_specs=[per_bg(nq, dh, wide), per_bg(nq, N_SEL, TQ)],
        out_shape=[jax.ShapeDtypeStruct((bsz, g, nq, dh, wide), F32),
                   jax.ShapeDtypeStruct((bsz, g, nq, N_SEL, TQ), BF16)],
        compiler_params=_params(("parallel", "arbitrary")),
        name="cmp_select",
    )(qt, kv_cmp, kv_cmp)


def _attn_kernel(qt_ref, oc_ref, bias_ref, ks_ref, kw_ref, vst_ref, vwt_ref, gl_ref, o_ref,
                 m_ref, acc_ref, sc_ref):
    qi = pl.program_id(2)
    last = pl.num_programs(2) - 1
    hg = HEADS_PER_GROUP
    dh = HEAD_DIM
    wide = hg * TQ
    qt = qt_ref[0, 0, 0]

    def per_head_where(mask, a, fill):
        return jnp.concatenate(
            [jnp.where(mask, a[:, h * TQ:(h + 1) * TQ], fill) for h in range(hg)], axis=1)

    def block_bias(value):
        return jnp.full((N_SEL, wide), value, F32).astype(BF16)

    zero_rows = jnp.zeros((K_ROW - dh - N_SEL, wide), BF16)

    def scores(k_ref, kt, bias):
        k_t = k_ref[0, 0, pl.ds(pl.multiple_of(kt * TQ, TQ), TQ), :]
        return _dot(k_t, jnp.concatenate([qt, bias, zero_rows], axis=0))

    rel = (lax.broadcasted_iota(jnp.int32, (TQ, TQ), 0)
           - lax.broadcasted_iota(jnp.int32, (TQ, TQ), 1))

    def online(state, sc, v_t):
        m_tile = jnp.max(sc, axis=0, keepdims=True)
        if state is None:
            return m_tile, _dot(v_t, jnp.exp2(sc - m_tile).astype(BF16))
        m_old, acc_old = state
        m_new = jnp.maximum(m_old, m_tile)
        return m_new, jnp.exp2(m_old - m_new) * acc_old + _dot(v_t, jnp.exp2(sc - m_new).astype(BF16))

    k1 = jnp.maximum(qi - 1, 0)
    k2 = jnp.maximum(qi - 2, 0)
    s0 = per_head_where(rel <= 0, scores(kw_ref, qi, block_bias(0.0)), NEG)
    s1 = scores(kw_ref, k1, block_bias(jnp.where(qi >= 1, 0.0, NEG)))
    s2 = per_head_where(rel > 0, scores(kw_ref, k2, block_bias(jnp.where(qi >= 2, 0.0, NEG))), NEG)
    win = online(None, s0, vwt_ref[0, 0, qi])
    win = online(win, s1, vwt_ref[0, 0, k1])
    win = online(win, s2, vwt_ref[0, 0, k2])
    acc_ref[1] = win[1]

    bias_sel = jnp.concatenate([bias_ref[0, 0, 0]] * hg, axis=1)

    sd = per_head_where(rel <= 0, scores(ks_ref, qi, bias_sel), NEG)
    m_ref[...], acc_ref[0] = online(None, sd, vst_ref[0, 0, qi])

    def sel_scores(kt):
        pad_bias = block_bias(jnp.where(kt < qi, 0.0, NEG))
        return scores(ks_ref, jnp.minimum(kt, last), jnp.minimum(bias_sel, pad_bias))

    def sel_accumulate(sc, kt):
        m_ref[...], acc_ref[0] = online((m_ref[...], acc_ref[0]), sc, vst_ref[0, 0, jnp.minimum(kt, last)])

    sc_ref[0] = sel_scores(0)

    def sel_pair(pi, carry):
        kt = 2 * pi
        sc_ref[1] = sel_scores(kt + 1)
        sel_accumulate(sc_ref[0], kt)
        sc_ref[0] = sel_scores(kt + 2)
        sel_accumulate(sc_ref[1], kt + 1)
        return carry

    lax.fori_loop(0, (qi + 1) // 2, sel_pair, 0)

    gate = jax.nn.sigmoid(gl_ref[0, 0])
    o_cmp = oc_ref[0, 0, 0]
    acc_s = acc_ref[0]
    acc_w = acc_ref[1]
    o_sel = acc_s[0:dh, :] * (1.0 / acc_s[dh:dh + 1, :])
    o_win = acc_w[0:dh, :] * (1.0 / acc_w[dh:dh + 1, :])
    outs = []
    for h in range(hg):
        lanes = slice(h * TQ, (h + 1) * TQ)
        g0 = gate[3 * h:3 * h + 1, :]
        g1 = gate[3 * h + 1:3 * h + 2, :]
        g2 = gate[3 * h + 2:3 * h + 3, :]
        outs.append(g0 * o_cmp[:, lanes] + g1 * o_sel[:, lanes] + g2 * o_win[:, lanes])
    o_ref[0] = jnp.concatenate(outs, axis=0).T


def _attention(qt, o_cmp, bias, ks, kw, vst, vwt, glt):
    bsz, g, nq = qt.shape[:3]
    s = nq * TQ
    width = HEADS_PER_GROUP * HEAD_DIM
    wide = HEADS_PER_GROUP * TQ
    per_bg = lambda *shape: pl.BlockSpec((1, 1) + shape, lambda b, j, i: (b, j) + (0,) * len(shape))
    per_tile = lambda *shape: pl.BlockSpec((1, 1, 1) + shape, lambda b, j, i: (b, j, i) + (0,) * len(shape))
    return pl.pallas_call(
        _attn_kernel,
        grid=(bsz, g, nq),
        in_specs=[
            per_tile(HEAD_DIM, wide),
            per_tile(HEAD_DIM, wide),
            per_tile(N_SEL, TQ),
            per_bg(s, K_ROW),
            per_bg(s, K_ROW),
            per_bg(nq, V_ROWS, TQ),
            per_bg(nq, V_ROWS, TQ),
            pl.BlockSpec((1, 1, HEADS_PER_GROUP * N_BRANCH, TQ), lambda b, j, i: (b, j, 0, i)),
        ],
        out_specs=pl.BlockSpec((1, TQ, width), lambda b, j, i: (b, i, j)),
        out_shape=jax.ShapeDtypeStruct((bsz, s, D_ATT), F32),
        scratch_shapes=[
            pltpu.VMEM((1, wide), F32),
            pltpu.VMEM((2, V_ROWS, wide), F32),
            pltpu.VMEM((2, TQ, wide), F32),
        ],
        compiler_params=_params(("parallel", "parallel", "arbitrary")),
        name="nsa_attention",
    )(qt, o_cmp, bias, ks, kw, vst, vwt, glt)


def _rglru_kernel(x_ref, y_ref, cw_ref, cb_ref, w_ref, b_ref, lam_ref, o_ref, xpad_ref):
    s = x_ref.shape[1]
    c = RNN_BLOCK_DIM
    pad = SUBLANES
    xpad_ref[0:pad, :] = jnp.zeros((pad, c), F32)
    xpad_ref[pad:pad + s, :] = x_ref[0]
    cw = cw_ref[...]
    w = w_ref[0]
    bias = b_ref[0]
    lam = lam_ref[...]
    log_a_per_gate = -LRU_C * (jnp.maximum(-lam, 0.0) + jnp.log1p(jnp.exp(-jnp.abs(lam))))
    sub = lax.broadcasted_iota(jnp.int32, (TT, c), 0) & (SUBLANES - 1)
    h = jnp.zeros((1, c), F32)

    for ci in range(s // TT):
        t0 = ci * TT
        xc = cb_ref[...] + sum(
            xpad_ref[t0 + pad - (CONV_WIDTH - 1) + k:t0 + pad - (CONV_WIDTH - 1) + k + TT, :] * cw[k:k + 1, :]
            for k in range(CONV_WIDTH))
        gates = jax.nn.sigmoid(_dot(xc.astype(BF16), w) + bias)
        r = gates[:, 0:c]
        i = gates[:, c:2 * c]
        log_a = r * log_a_per_gate
        a = jnp.exp(log_a)
        var = -jnp.tanh(log_a) * (a * a + 1.0)
        bt = jnp.where(var > 0.0, var * lax.rsqrt(var), 0.0) * (i * xc)
        for d in (1, 2, 4):
            keep = sub >= d
            a_prev = jnp.where(keep, pltpu.roll(a, d, axis=0), 1.0)
            b_prev = jnp.where(keep, pltpu.roll(bt, d, axis=0), 0.0)
            bt = bt + a * b_prev
            a = a * a_prev
        states = []
        for gi in range(TT // SUBLANES):
            grp = slice(gi * SUBLANES, (gi + 1) * SUBLANES)
            hg = bt[grp, :] + a[grp, :] * h
            states.append(hg)
            h = hg[SUBLANES - 1:SUBLANES, :]
        o_ref[0, t0:t0 + TT, :] = jax.nn.gelu(y_ref[0, t0:t0 + TT, :]) * jnp.concatenate(states, axis=0)


def _rglru(r_in, conv_w, conv_b, w_cat, b_cat, lam):
    bsz, s, _ = r_in.shape
    c = RNN_BLOCK_DIM
    nb = RNN_BLOCKS
    return pl.pallas_call(
        _rglru_kernel,
        grid=(bsz, nb),
        in_specs=[
            pl.BlockSpec((1, s, c), lambda b, j: (b, 0, j)),
            pl.BlockSpec((1, s, c), lambda b, j: (b, 0, nb + j)),
            pl.BlockSpec((CONV_WIDTH, c), lambda b, j: (0, j)),
            pl.BlockSpec((1, c), lambda b, j: (0, j)),
            pl.BlockSpec((1, c, 2 * c), lambda b, j: (j, 0, 0)),
            pl.BlockSpec((1, 1, 2 * c), lambda b, j: (j, 0, 0)),
            pl.BlockSpec((1, c), lambda b, j: (0, j)),
        ],
        out_specs=pl.BlockSpec((1, s, c), lambda b, j: (b, 0, j)),
        out_shape=jax.ShapeDtypeStruct((bsz, s, D_RNN), F32),
        scratch_shapes=[pltpu.VMEM((s + SUBLANES, c), F32)],
        compiler_params=_params(("parallel", "arbitrary")),
        name="rg_lru",
    )(r_in, r_in, conv_w, conv_b, w_cat, b_cat, lam)


def _outproj_kernel(oa_ref, or_ref, ga_ref, gr_ref, w_ref, x_ref, gt_ref, gp_ref, o_ref):
    a = _rms(oa_ref[0], ga_ref[...]).astype(BF16)
    r = _rms(or_ref[0], gr_ref[...]).astype(BF16)
    mix = _dot(a, w_ref[0:D_ATT, :]) + _dot(r, w_ref[D_ATT:D_ATT + D_RNN, :])
    o_ref[0] = x_ref[0] + gt_ref[0, 0] * _rms(mix, gp_ref[...])


def _outproj(o_att, o_rnn, g_att, g_rnn, w_out, x, mod6, g_post, tm=512):
    bsz, s, d = x.shape
    row = lambda width: pl.BlockSpec((1, tm, width), lambda b, i: (b, i, 0))
    vec = lambda width: pl.BlockSpec((1, width), lambda b, i: (0, 0))
    return pl.pallas_call(
        _outproj_kernel,
        grid=(bsz, s // tm),
        in_specs=[
            row(D_ATT), row(D_RNN), vec(D_ATT), vec(D_RNN),
            pl.BlockSpec((D_ATT + D_RNN, d), lambda b, i: (0, 0), pipeline_mode=pl.Buffered(1)),
            row(d),
            pl.BlockSpec((1, 1, 1, d), lambda b, i: (2, b, 0, 0)),
            vec(d),
        ],
        out_specs=row(d),
        out_shape=jax.ShapeDtypeStruct((bsz, s, d), F32),
        compiler_params=_params(("parallel", "arbitrary")),
        name="out_proj",
    )(o_att, o_rnn, g_att.reshape(1, -1), g_rnn.reshape(1, -1), w_out, x, mod6, g_post.reshape(1, d))


def _mlp_kernel(x_ref, g_ref, sc_ref, sh_ref, w1_ref, w2_ref, gt_ref, gp_ref, o_ref, h_ref, a_ref):
    j = pl.program_id(2)
    n_chunks = pl.num_programs(2) - 1
    slot = j % 2

    def up():
        u = jnp.maximum(_dot(h_ref[...], w1_ref[...]), 0.0)
        a_ref[slot] = (u * u).astype(BF16)

    def down():
        o_ref[0] += _dot(a_ref[1 - slot], w2_ref[...])

    @pl.when(j == 0)
    def _():
        h = _rms(x_ref[0], g_ref[...]) * (1.0 + sc_ref[0, 0]) + sh_ref[0, 0]
        h_ref[...] = h.astype(BF16)
        o_ref[0] = jnp.zeros(o_ref.shape[1:], F32)
        up()

    @pl.when((j > 0) & (j < n_chunks))
    def _():
        down()
        up()

    @pl.when(j == n_chunks)
    def _():
        down()
        o_ref[0] = x_ref[0] + gt_ref[0, 0] * _rms(o_ref[0], gp_ref[...])


def _mlp(x, g_pre, mod6, w1, w2, g_post, tm=1024, tf=1024):
    bsz, s, d = x.shape
    n_chunks = w1.shape[1] // tf
    row = pl.BlockSpec((1, tm, d), lambda b, i, j: (b, i, 0))
    vec = pl.BlockSpec((1, d), lambda b, i, j: (0, 0))
    modk = lambda k: pl.BlockSpec((1, 1, 1, d), lambda b, i, j: (k, b, 0, 0))
    return pl.pallas_call(
        _mlp_kernel,
        grid=(bsz, s // tm, n_chunks + 1),
        in_specs=[
            row, vec, modk(4), modk(3),
            pl.BlockSpec((d, tf), lambda b, i, j: (0, jnp.minimum(j, n_chunks - 1))),
            pl.BlockSpec((tf, d), lambda b, i, j: (jnp.maximum(j - 1, 0), 0)),
            modk(5), vec,
        ],
        out_specs=row,
        out_shape=jax.ShapeDtypeStruct((bsz, s, d), F32),
        scratch_shapes=[pltpu.VMEM((tm, d), BF16), pltpu.VMEM((2, tm, tf), BF16)],
        compiler_params=_params(("parallel", "parallel", "arbitrary")),
        name="mlp",
    )(x, g_pre.reshape(1, d), mod6, mod6, w1, w2, mod6, g_post.reshape(1, d))


def _layer(x, c, w_ada, b_ada, g_pre_mix, g_post_mix, g_pre_mlp, g_post_mlp, w_in_perm,
           cmp_w1_k, cmp_w2_k, cmp_pe_k, cmp_w1_v, cmp_w2_v, cmp_pe_v,
           conv_w, conv_b, w_rg_a, b_rg_a, w_rg_x, b_rg_x, lru_lambda,
           g_grp_att, g_grp_rnn, w_out, w_ff1, w_ff2):
    bsz, s, d = x.shape
    dh = HEAD_DIM

    mod = _ada(c, w_ada, b_ada)
    mod6 = mod.reshape(bsz, 6, 1, d).transpose(1, 0, 2, 3)

    qt, kvc, ks, kw, vst, vwt, r_in, glt = _inproj(x, g_pre_mix, mod6, w_in_perm)

    def pair_diag(w, axis):
        z = jnp.zeros_like(w)
        return jnp.concatenate([jnp.concatenate([w, z], axis=-1), jnp.concatenate([z, w], axis=-1)], axis=axis)

    n_pair = CMP_STRIDE // 2
    pe = jnp.tile(jnp.stack([cmp_pe_k, cmp_pe_v]), (1, 1, 2)).reshape(2, 2, n_pair, 4 * dh)
    w1c = pair_diag(jnp.stack([cmp_w1_k, cmp_w1_v]).astype(BF16).reshape(2, CMP_BLOCK, dh, -1), 2)
    w1c = w1c.reshape(2, 2, n_pair, 4 * dh, w1c.shape[-1])
    w2c = pair_diag(jnp.stack([cmp_w2_k, cmp_w2_v]).astype(BF16), 1)
    kv_cmp = _compress(kvc, pe, w1c, w2c)

    o_cmp, sel_bias = _cmp_select(qt, kv_cmp)
    o_att = _attention(qt, o_cmp, sel_bias, ks, kw, vst, vwt, glt)

    w_cat = jnp.concatenate([w_rg_a, w_rg_x], axis=-1).astype(BF16)
    b_cat = jnp.concatenate([b_rg_a.reshape(RNN_BLOCKS, 1, RNN_BLOCK_DIM),
                             b_rg_x.reshape(RNN_BLOCKS, 1, RNN_BLOCK_DIM)], axis=-1)
    o_rnn = _rglru(r_in, conv_w, conv_b.reshape(1, -1), w_cat, b_cat, lru_lambda.reshape(1, -1))

    x1 = _outproj(o_att, o_rnn, g_grp_att, g_grp_rnn, w_out.astype(BF16), x, mod6, g_post_mix)
    return _mlp(x1, g_pre_mlp, mod6, w_ff1.astype(BF16), w_ff2.astype(BF16), g_post_mlp)


def kernel(x, c, w_ada, b_ada, g_pre_mix, g_post_mix, g_pre_mlp, g_post_mlp, w_in, cmp_w1_k, cmp_w2_k, cmp_pe_k, cmp_w1_v, cmp_w2_v, cmp_pe_v, conv_w, conv_b, w_rg_a, b_rg_a, w_rg_x, b_rg_x, lru_lambda, g_grp_att, g_grp_rnn, w_out, w_ff1, w_ff2):
    depth = w_ada.shape[0]
    for l in range(depth):
        x = _layer(x, c, w_ada[l], b_ada[l], g_pre_mix[l], g_post_mix[l], g_pre_mlp[l], g_post_mlp[l],
                   _w_in_prep(w_in, l), cmp_w1_k[l], cmp_w2_k[l], cmp_pe_k[l], cmp_w1_v[l], cmp_w2_v[l], cmp_pe_v[l],
                   conv_w[l], conv_b[l], w_rg_a[l], b_rg_a[l], w_rg_x[l], b_rg_x[l], lru_lambda[l],
                   g_grp_att[l], g_grp_rnn[l], w_out[l], w_ff1[l], w_ff2[l])
    return x
```

```python
import jax
import jax.numpy as jnp
from jax import lax
from jax.experimental import pallas as pl
from jax.experimental.pallas import tpu as pltpu

F32 = jnp.float32
BF16 = jnp.bfloat16

D_MODEL = 2048
D_ATT = 1024
D_RNN = 1024
N_Q_HEADS = 16
N_KV_GROUPS = 4
HEADS_PER_GROUP = 4
HEAD_DIM = 64
D_KV = 256
CMP_BLOCK = 32
CMP_STRIDE = 16
SEL_BLOCK = 64
SEL_SHIFT = 6
SEL_TOP_N = 8
WINDOW = 512
N_BRANCH = 3
RNN_BLOCKS = 8
RNN_BLOCK_DIM = 128
CONV_WIDTH = 4
LRU_C = 8.0
D_FF = 4 * D_MODEL
EPS = 1e-6
NEG = -1e30
FORCE_SCORE = 1e9
LOG2_E = 1.4426950408889634

LANES = 128
SUBLANES = 8
VMEM_LIMIT = 60 * 1024 * 1024

TQ = 256
N_CMP_PAD = 128
N_SEL = 32
GL_PAD = 128
TT = 128
assert WINDOW == 2 * TQ


def _params(sem):
    return pltpu.CompilerParams(dimension_semantics=sem, vmem_limit_bytes=VMEM_LIMIT)


def _dot(a, b):
    return jnp.dot(a, b, preferred_element_type=F32)


def _rms(x, g):
    return x * lax.rsqrt(jnp.mean(x * x, axis=-1, keepdims=True) + EPS) * g


def _ada_kernel(c_ref, w_ref, b_ref, o_ref):
    c = c_ref[...]
    ca = (c * jax.nn.sigmoid(c)).astype(BF16)
    o_ref[...] = _dot(ca, w_ref[...].astype(BF16)) + b_ref[...]


def _ada(c, w, b):
    bsz, d = c.shape
    n = w.shape[1]
    tn = 1024
    return pl.pallas_call(
        _ada_kernel,
        grid=(n // tn,),
        in_specs=[
            pl.BlockSpec((bsz, d), lambda j: (0, 0)),
            pl.BlockSpec((d, tn), lambda j: (0, j)),
            pl.BlockSpec((1, tn), lambda j: (0, j)),
        ],
        out_specs=pl.BlockSpec((bsz, tn), lambda j: (0, j)),
        out_shape=jax.ShapeDtypeStruct((bsz, n), F32),
        compiler_params=_params(("arbitrary",)),
        name="ada_mod",
    )(c, w, b.reshape(1, n))


C_Q = (0, D_ATT)
C_CMP = (C_Q[1], C_Q[1] + 2 * D_KV)
C_K = (C_CMP[1], C_CMP[1] + 2 * D_KV)
C_V = (C_K[1], C_K[1] + 2 * D_KV)
C_R = (C_V[1], C_V[1] + 2 * D_RNN)
C_GL = (C_R[1], C_R[1] + GL_PAD)
D_IN_PAD = C_GL[1]
K_ROW = LANES
V_ROWS = HEAD_DIM + 16
N_GATE = HEADS_PER_GROUP * N_BRANCH


def _w_in_prep_kernel(wt_ref, o_ref):
    seg = lambda lo, hi: wt_ref[0, lo:hi, :].T.astype(BF16)
    kv0 = D_ATT
    gl0 = D_ATT + 6 * D_KV
    n_gl = N_BRANCH * N_Q_HEADS
    o_ref[:, 0:C_K[0] + D_KV] = seg(0, kv0 + 3 * D_KV)
    o_ref[:, C_K[0] + D_KV:C_K[1]] = seg(kv0 + 4 * D_KV, kv0 + 5 * D_KV)
    o_ref[:, C_V[0]:C_V[0] + D_KV] = seg(kv0 + 3 * D_KV, kv0 + 4 * D_KV)
    o_ref[:, C_V[0] + D_KV:C_V[1]] = seg(kv0 + 5 * D_KV, kv0 + 6 * D_KV)
    o_ref[:, C_R[0]:C_R[1]] = seg(gl0 + n_gl, gl0 + n_gl + 2 * D_RNN)
    gl_tile = wt_ref[0, gl0:gl0 + GL_PAD, :].T
    lane = lax.broadcasted_iota(jnp.int32, gl_tile.shape, 1)
    o_ref[:, C_GL[0]:C_GL[1]] = jnp.where(lane < n_gl, gl_tile, 0.0).astype(BF16)


def _w_in_prep(w_in_all, layer, tk=256):
    _, d, n = w_in_all.shape
    w_t = jnp.swapaxes(w_in_all, 1, 2)
    return pl.pallas_call(
        _w_in_prep_kernel,
        grid=(d // tk,),
        in_specs=[pl.BlockSpec((1, n, tk), lambda i: (layer, 0, i))],
        out_specs=pl.BlockSpec((tk, D_IN_PAD), lambda i: (i, 0)),
        out_shape=jax.ShapeDtypeStruct((d, D_IN_PAD), BF16),
        compiler_params=_params(("arbitrary",)),
        name="w_in_prep",
    )(w_t)


def _inproj_kernel(x_ref, g_ref, sc_ref, sh_ref, w_ref,
                   qt_ref, cmp_ref, ks_ref, kw_ref, vst_ref, vwt_ref, r_ref, glt_ref):
    tm = x_ref.shape[1]
    ng, hg, dh = N_KV_GROUPS, HEADS_PER_GROUP, HEAD_DIM
    x = x_ref[0]
    h = _rms(x, g_ref[...]) * (1.0 + sc_ref[0, 0]) + sh_ref[0, 0]
    hb = h.astype(BF16)
    tiles = [slice(u * TQ, (u + 1) * TQ) for u in range(tm // TQ)]

    q_t = (_dot(hb, w_ref[:, C_Q[0]:C_Q[1]]) * (dh ** -0.5 * LOG2_E)).T.astype(BF16)
    for j in range(ng):
        for u, cols in enumerate(tiles):
            qt_ref[0, j, u] = jnp.concatenate(
                [q_t[(hg * j + hh) * dh:(hg * j + hh + 1) * dh, cols] for hh in range(hg)], axis=1)

    kv_cmp = _dot(hb, w_ref[:, C_CMP[0]:C_CMP[1]])
    for c in range(2 * D_KV // LANES):
        cmp_ref[0, c] = kv_cmp[:, c * LANES:(c + 1) * LANES]

    keys = _dot(hb, w_ref[:, C_K[0]:C_K[1]])
    row_pos = pl.program_id(1) * tm + lax.broadcasted_iota(jnp.int32, (tm, K_ROW - dh), 0)
    lane = lax.broadcasted_iota(jnp.int32, (tm, K_ROW - dh), 1)
    onehot = jnp.where(lax.shift_right_logical(row_pos, SEL_SHIFT) == lane, 1.0, 0.0)
    for j in range(ng):
        ks_ref[0, j] = jnp.concatenate([keys[:, j * dh:(j + 1) * dh], onehot], axis=1).astype(BF16)
        kw_ref[0, j] = jnp.concatenate(
            [keys[:, D_KV + j * dh:D_KV + (j + 1) * dh], onehot], axis=1).astype(BF16)

    v_t = _dot(hb, w_ref[:, C_V[0]:C_V[1]]).T.astype(BF16)
    ones = jnp.ones((V_ROWS - dh, TQ), BF16)
    for j in range(ng):
        for u, cols in enumerate(tiles):
            vst_ref[0, j, u] = jnp.concatenate([v_t[j * dh:(j + 1) * dh, cols], ones], axis=0)
            vwt_ref[0, j, u] = jnp.concatenate([v_t[D_KV + j * dh:D_KV + (j + 1) * dh, cols], ones], axis=0)

    half = (C_R[0] + C_R[1]) // 2
    r_ref[0, :, 0:D_RNN] = _dot(hb, w_ref[:, C_R[0]:half])
    r_ref[0, :, D_RNN:2 * D_RNN] = _dot(hb, w_ref[:, half:C_R[1]])

    gl_t = _dot(hb, w_ref[:, C_GL[0]:C_GL[1]]).T
    for j in range(ng):
        glt_ref[0, j] = gl_t[N_GATE * j:N_GATE * (j + 1), :]


def _inproj(x, g, mod6, w_perm, tm=512):
    bsz, s, d = x.shape
    ng = N_KV_GROUPS
    nt = tm // TQ
    row = lambda width: pl.BlockSpec((1, tm, width), lambda b, i: (b, i, 0))
    grp_rows = pl.BlockSpec((1, ng, tm, K_ROW), lambda b, i: (b, 0, i, 0))
    grp_tiles = lambda r, c: pl.BlockSpec((1, ng, nt, r, c), lambda b, i: (b, 0, i, 0, 0))
    sds = jax.ShapeDtypeStruct
    return pl.pallas_call(
        _inproj_kernel,
        grid=(bsz, s // tm),
        in_specs=[
            row(d),
            pl.BlockSpec((1, d), lambda b, i: (0, 0)),
            pl.BlockSpec((1, 1, 1, d), lambda b, i: (1, b, 0, 0)),
            pl.BlockSpec((1, 1, 1, d), lambda b, i: (0, b, 0, 0)),
            pl.BlockSpec((d, D_IN_PAD), lambda b, i: (0, 0), pipeline_mode=pl.Buffered(1)),
        ],
        out_specs=[
            grp_tiles(HEAD_DIM, HEADS_PER_GROUP * TQ),
            pl.BlockSpec((1, 2 * D_KV // LANES, tm, LANES), lambda b, i: (b, 0, i, 0)),
            grp_rows, grp_rows,
            grp_tiles(V_ROWS, TQ), grp_tiles(V_ROWS, TQ), row(2 * D_RNN),
            pl.BlockSpec((1, ng, N_GATE, tm), lambda b, i: (b, 0, 0, i)),
        ],
        out_shape=[
            sds((bsz, ng, s // TQ, HEAD_DIM, HEADS_PER_GROUP * TQ), BF16),
            sds((bsz, 2 * D_KV // LANES, s, LANES), F32),
            sds((bsz, ng, s, K_ROW), BF16), sds((bsz, ng, s, K_ROW), BF16),
            sds((bsz, ng, s // TQ, V_ROWS, TQ), BF16), sds((bsz, ng, s // TQ, V_ROWS, TQ), BF16),
            sds((bsz, s, 2 * D_RNN), F32),
            sds((bsz, ng, N_GATE, s), F32),
        ],
        compiler_params=_params(("parallel", "arbitrary")),
        name="in_proj",
    )(x, g.reshape(1, d), mod6, mod6, w_perm)


def _compress_kernel(x_ref, pe_ref, w1_ref, w2_ref, o_ref):
    bsz = x_ref.shape[0]
    dh = HEAD_DIM
    n_rows = x_ref.shape[2] // CMP_STRIDE
    rows = bsz * n_rows
    first = jnp.zeros((rows, w1_ref.shape[4]), F32)
    second = jnp.zeros((rows, w1_ref.shape[4]), F32)
    for lp in range(CMP_STRIDE // 2):
        x_2 = jnp.concatenate(
            [x_ref[:, 0, pl.ds(2 * lp + k, n_rows, stride=CMP_STRIDE), :].reshape(rows, LANES) for k in range(2)],
            axis=1)
        first = first + _dot((x_2 + pe_ref[0, 0, lp:lp + 1, :]).astype(BF16), w1_ref[0, 0, lp])
        second = second + _dot((x_2 + pe_ref[0, 1, lp:lp + 1, :]).astype(BF16), w1_ref[0, 1, lp])
    hid = first + pltpu.roll(second, rows - 1, axis=0)
    out = _dot(jax.nn.gelu(hid).astype(BF16), w2_ref[0]).astype(BF16)
    for b in range(bsz):
        for j in range(2):
            o_ref[0, b, j] = out[b * n_rows:(b + 1) * n_rows, j * dh:(j + 1) * dh]


def _compress(kvc4, pe, w1, w2):
    bsz, n_tiles, s, _ = kvc4.shape
    n_rows = s // CMP_STRIDE
    per_kind = lambda a: pl.BlockSpec((1,) + a.shape[1:], lambda t, p: (t,) + (0,) * (a.ndim - 1))
    return pl.pallas_call(
        _compress_kernel,
        grid=(2, n_tiles // 2),
        in_specs=[pl.BlockSpec((bsz, 1, s, LANES), lambda t, p: (0, 2 * t + p, 0, 0)),
                  per_kind(pe), per_kind(w1), per_kind(w2)],
        out_specs=pl.BlockSpec((1, bsz, 2, n_rows, HEAD_DIM), lambda t, p: (t, 0, p, 0, 0)),
        out_shape=jax.ShapeDtypeStruct((2, bsz, N_KV_GROUPS, n_rows, HEAD_DIM), BF16),
        compiler_params=_params(("arbitrary", "arbitrary")),
        name="compress_kv",
    )(kvc4, pe, w1, w2)


def _cmp_select_kernel(qt_ref, kc_ref, vc_ref, oc_ref, bias_ref):
    hg = HEADS_PER_GROUP
    kc = kc_ref[0, 0, 0]
    vct = vc_ref[0, 0, 0].astype(F32).T.astype(BF16)
    jj = lax.broadcasted_iota(jnp.int32, (N_SEL, N_CMP_PAD), 0) * SEL_BLOCK
    nn = lax.broadcasted_iota(jnp.int32, (N_SEL, N_CMP_PAD), 1) * CMP_STRIDE
    ov = jnp.minimum(nn + CMP_BLOCK, jj + SEL_BLOCK) - jnp.maximum(nn, jj)
    w_sel = (jnp.maximum(ov, 0).astype(F32) * (1.0 / CMP_BLOCK)).astype(BF16)
    nrow = lax.broadcasted_iota(jnp.int32, (N_CMP_PAD, TQ), 0)
    blk = lax.broadcasted_iota(jnp.int32, (N_SEL, TQ), 0)

    for qi in range(qt_ref.shape[2]):
        qt = qt_ref[0, 0, qi]
        pos = qi * TQ + lax.broadcasted_iota(jnp.int32, (1, TQ), 1)
        mask_c = nrow * CMP_STRIDE + (CMP_BLOCK - 1) <= pos

        def per_head_where(a, fill):
            return jnp.concatenate(
                [jnp.where(mask_c, a[:, h * TQ:(h + 1) * TQ], fill) for h in range(hg)], axis=1)

        s = per_head_where(_dot(kc, qt), NEG)
        e = jnp.exp2(s - jnp.max(s, axis=0, keepdims=True))
        p = per_head_where(e * (1.0 / jnp.sum(e, axis=0, keepdims=True)), 0.0)
        oc_ref[0, 0, qi] = _dot(vct, p.astype(BF16))
        psum = p[:, 0:TQ]
        for h in range(1, hg):
            psum = psum + p[:, h * TQ:(h + 1) * TQ]

        p_hi = psum.astype(BF16)
        p_lo = (psum - p_hi.astype(F32)).astype(BF16)
        imp = _dot(w_sel, p_hi) + _dot(w_sel, p_lo)
        cur = lax.shift_right_logical(pos, SEL_SHIFT)
        free = jnp.where(blk * SEL_BLOCK <= pos, imp, -FORCE_SCORE)
        score = jnp.where(blk == 0, FORCE_SCORE,
                          jnp.where(blk == cur, FORCE_SCORE, jnp.where(blk == cur - 1, FORCE_SCORE, free)))
        rank = jnp.zeros((N_SEL, TQ), F32)
        for k in range(N_SEL):
            sk = score[k:k + 1, :]
            tie = jnp.where(blk > k, 1.0, 0.0)
            rank = rank + jnp.where(sk > score, 1.0, jnp.where(sk == score, tie, 0.0))
        bias_ref[0, 0, qi] = jnp.where(rank < SEL_TOP_N, 0.0, NEG).astype(BF16)


def _cmp_select(qt, kv_cmp):
    bsz, g, nq, dh, wide = qt.shape
    per_bg = lambda *shape: pl.BlockSpec((1, 1) + shape, lambda b, j: (b, j) + (0,) * len(shape))
    kind = lambda t: pl.BlockSpec((1, 1, 1, N_CMP_PAD, dh), lambda b, j: (t, b, j, 0, 0))
    return pl.pallas_call(
        _cmp_select_kernel,
        grid=(bsz, g),
        in_specs=[per_bg(nq, dh, wide), kind(0), kind(1)],
        out_specs=[per_bg(nq, dh, wide), per_bg(nq, N_SEL, TQ)],
        out_shape=[jax.ShapeDtypeStruct((bsz, g, nq, dh, wide), F32),
                   jax.ShapeDtypeStruct((bsz, g, nq, N_SEL, TQ), BF16)],
        compiler_params=_params(("parallel", "arbitrary")),
        name="cmp_select",
    )(qt, kv_cmp, kv_cmp)


def _attn_kernel(qt_ref, oc_ref, bias_ref, ks_ref, kw_ref, vst_ref, vwt_ref, gl_ref, o_ref,
                 m_ref, acc_ref, sc_ref):
    qi = pl.program_id(2)
    last = pl.num_programs(2) - 1
    hg = HEADS_PER_GROUP
    dh = HEAD_DIM
    wide = hg * TQ
    n_grp = qt_ref.shape[1]
    qts = [qt_ref[0, gg, 0] for gg in range(n_grp)]

    def per_head_where(mask, a, fill):
        return jnp.concatenate(
            [jnp.where(mask, a[:, h * TQ:(h + 1) * TQ], fill) for h in range(hg)], axis=1)

    def block_bias(value):
        return jnp.full((N_SEL, wide), value, F32).astype(BF16)

    zero_rows = jnp.zeros((K_ROW - dh - N_SEL, wide), BF16)

    def scores(k_ref, gg, kt, bias):
        k_t = k_ref[0, gg, pl.ds(pl.multiple_of(kt * TQ, TQ), TQ), :]
        return _dot(k_t, jnp.concatenate([qts[gg], bias, zero_rows], axis=0))

    rel = (lax.broadcasted_iota(jnp.int32, (TQ, TQ), 0)
           - lax.broadcasted_iota(jnp.int32, (TQ, TQ), 1))

    def online(state, sc, v_t):
        m_tile = jnp.max(sc, axis=0, keepdims=True)
        if state is None:
            return m_tile, _dot(v_t, jnp.exp2(sc - m_tile).astype(BF16))
        m_old, acc_old = state
        m_new = jnp.maximum(m_old, m_tile)
        return m_new, jnp.exp2(m_old - m_new) * acc_old + _dot(v_t, jnp.exp2(sc - m_new).astype(BF16))

    k1 = jnp.maximum(qi - 1, 0)
    k2 = jnp.maximum(qi - 2, 0)
    bias_sel = []
    for gg in range(n_grp):
        s0 = per_head_where(rel <= 0, scores(kw_ref, gg, qi, block_bias(0.0)), NEG)
        s1 = scores(kw_ref, gg, k1, block_bias(jnp.where(qi >= 1, 0.0, NEG)))
        s2 = per_head_where(rel > 0, scores(kw_ref, gg, k2, block_bias(jnp.where(qi >= 2, 0.0, NEG))), NEG)
        win = online(None, s0, vwt_ref[0, gg, qi])
        win = online(win, s1, vwt_ref[0, gg, k1])
        win = online(win, s2, vwt_ref[0, gg, k2])
        acc_ref[gg, 1] = win[1]
        bias_sel.append(jnp.concatenate([bias_ref[0, gg, 0]] * hg, axis=1))

    def sel_scores(gg, kt):
        pad_bias = block_bias(jnp.where(kt < qi, 0.0, NEG))
        return scores(ks_ref, gg, jnp.minimum(kt, last), jnp.minimum(bias_sel[gg], pad_bias))

    def sel_accumulate(gg, sc, kt):
        m_ref[gg], acc_ref[gg, 0] = online((m_ref[gg], acc_ref[gg, 0]), sc,
                                           vst_ref[0, gg, jnp.minimum(kt, last)])

    for gg in range(n_grp):
        sd = per_head_where(rel <= 0, scores(ks_ref, gg, qi, bias_sel[gg]), NEG)
        m_ref[gg], acc_ref[gg, 0] = online(None, sd, vst_ref[0, gg, qi])
        sc_ref[gg, 0] = sel_scores(gg, 0)

    def sel_pair(pi, carry):
        kt = 2 * pi
        for gg in range(n_grp):
            sc_ref[gg, 1] = sel_scores(gg, kt + 1)
            sel_accumulate(gg, sc_ref[gg, 0], kt)
        for gg in range(n_grp):
            sc_ref[gg, 0] = sel_scores(gg, kt + 2)
            sel_accumulate(gg, sc_ref[gg, 1], kt + 1)
        return carry

    lax.fori_loop(0, (qi + 1) // 2, sel_pair, 0)

    outs = []
    for gg in range(n_grp):
        gate = jax.nn.sigmoid(gl_ref[0, gg])
        o_cmp = oc_ref[0, gg, 0]
        acc_s = acc_ref[gg, 0]
        acc_w = acc_ref[gg, 1]
        o_sel = acc_s[0:dh, :] * (1.0 / acc_s[dh:dh + 1, :])
        o_win = acc_w[0:dh, :] * (1.0 / acc_w[dh:dh + 1, :])
        for h in range(hg):
            lanes = slice(h * TQ, (h + 1) * TQ)
            g0 = gate[3 * h:3 * h + 1, :]
            g1 = gate[3 * h + 1:3 * h + 2, :]
            g2 = gate[3 * h + 2:3 * h + 3, :]
            outs.append(g0 * o_cmp[:, lanes] + g1 * o_sel[:, lanes] + g2 * o_win[:, lanes])
    o_ref[0] = jnp.concatenate(outs, axis=0).T


def _attention(qt, o_cmp, bias, ks, kw, vst, vwt, glt):
    bsz, g, nq = qt.shape[:3]
    s = nq * TQ
    width = HEADS_PER_GROUP * HEAD_DIM
    wide = HEADS_PER_GROUP * TQ
    gs = 2
    per_bg = lambda *shape: pl.BlockSpec((1, gs) + shape, lambda b, j, i: (b, j) + (0,) * len(shape))
    per_tile = lambda *shape: pl.BlockSpec((1, gs, 1) + shape, lambda b, j, i: (b, j, i) + (0,) * len(shape))
    return pl.pallas_call(
        _attn_kernel,
        grid=(bsz, g // gs, nq),
        in_specs=[
            per_tile(HEAD_DIM, wide),
            per_tile(HEAD_DIM, wide),
            per_tile(N_SEL, TQ),
            per_bg(s, K_ROW),
            per_bg(s, K_ROW),
            per_bg(nq, V_ROWS, TQ),
            per_bg(nq, V_ROWS, TQ),
            pl.BlockSpec((1, gs, HEADS_PER_GROUP * N_BRANCH, TQ), lambda b, j, i: (b, j, 0, i)),
        ],
        out_specs=pl.BlockSpec((1, TQ, gs * width), lambda b, j, i: (b, i, j)),
        out_shape=jax.ShapeDtypeStruct((bsz, s, D_ATT), F32),
        scratch_shapes=[
            pltpu.VMEM((gs, 1, wide), F32),
            pltpu.VMEM((gs, 2, V_ROWS, wide), F32),
            pltpu.VMEM((gs, 2, TQ, wide), F32),
        ],
        compiler_params=_params(("parallel", "parallel", "arbitrary")),
        name="nsa_attention",
    )(qt, o_cmp, bias, ks, kw, vst, vwt, glt)


def _rglru_kernel(x_ref, y_ref, cw_ref, cb_ref, w_ref, b_ref, lam_ref, o_ref, xpad_ref):
    s = x_ref.shape[1]
    c = RNN_BLOCK_DIM
    pad = SUBLANES
    xpad_ref[0:pad, :] = jnp.zeros((pad, c), F32)
    xpad_ref[pad:pad + s, :] = x_ref[0]
    cw = cw_ref[...]
    w = w_ref[0]
    bias = b_ref[0]
    lam = lam_ref[...]
    log_a_per_gate = -LRU_C * (jnp.maximum(-lam, 0.0) + jnp.log1p(jnp.exp(-jnp.abs(lam))))
    sub = lax.broadcasted_iota(jnp.int32, (TT, c), 0) & (SUBLANES - 1)
    h = jnp.zeros((1, c), F32)

    for ci in range(s // TT):
        t0 = ci * TT
        xc = cb_ref[...] + sum(
            xpad_ref[t0 + pad - (CONV_WIDTH - 1) + k:t0 + pad - (CONV_WIDTH - 1) + k + TT, :] * cw[k:k + 1, :]
            for k in range(CONV_WIDTH))
        gates = jax.nn.sigmoid(_dot(xc.astype(BF16), w) + bias)
        r = gates[:, 0:c]
        i = gates[:, c:2 * c]
        log_a = r * log_a_per_gate
        a = jnp.exp(log_a)
        var = -jnp.tanh(log_a) * (a * a + 1.0)
        bt = jnp.where(var > 0.0, var * lax.rsqrt(var), 0.0) * (i * xc)
        for d in (1, 2, 4):
            keep = sub >= d
            a_prev = jnp.where(keep, pltpu.roll(a, d, axis=0), 1.0)
            b_prev = jnp.where(keep, pltpu.roll(bt, d, axis=0), 0.0)
            bt = bt + a * b_prev
            a = a * a_prev
        states = []
        for gi in range(TT // SUBLANES):
            grp = slice(gi * SUBLANES, (gi + 1) * SUBLANES)
            hg = bt[grp, :] + a[grp, :] * h
            states.append(hg)
            h = hg[SUBLANES - 1:SUBLANES, :]
        o_ref[0, t0:t0 + TT, :] = jax.nn.gelu(y_ref[0, t0:t0 + TT, :]) * jnp.concatenate(states, axis=0)


def _rglru(r_in, conv_w, conv_b, w_cat, b_cat, lam):
    bsz, s, _ = r_in.shape
    c = RNN_BLOCK_DIM
    nb = RNN_BLOCKS
    return pl.pallas_call(
        _rglru_kernel,
        grid=(bsz, nb),
        in_specs=[
            pl.BlockSpec((1, s, c), lambda b, j: (b, 0, j)),
            pl.BlockSpec((1, s, c), lambda b, j: (b, 0, nb + j)),
            pl.BlockSpec((CONV_WIDTH, c), lambda b, j: (0, j)),
            pl.BlockSpec((1, c), lambda b, j: (0, j)),
            pl.BlockSpec((1, c, 2 * c), lambda b, j: (j, 0, 0)),
            pl.BlockSpec((1, 1, 2 * c), lambda b, j: (j, 0, 0)),
            pl.BlockSpec((1, c), lambda b, j: (0, j)),
        ],
        out_specs=pl.BlockSpec((1, s, c), lambda b, j: (b, 0, j)),
        out_shape=jax.ShapeDtypeStruct((bsz, s, D_RNN), F32),
        scratch_shapes=[pltpu.VMEM((s + SUBLANES, c), F32)],
        compiler_params=_params(("parallel", "arbitrary")),
        name="rg_lru",
    )(r_in, r_in, conv_w, conv_b, w_cat, b_cat, lam)


def _outproj_kernel(oa_ref, or_ref, ga_ref, gr_ref, w_ref, x_ref, gt_ref, gp_ref, o_ref):
    a = _rms(oa_ref[0], ga_ref[...]).astype(BF16)
    r = _rms(or_ref[0], gr_ref[...]).astype(BF16)
    mix = _dot(a, w_ref[0:D_ATT, :]) + _dot(r, w_ref[D_ATT:D_ATT + D_RNN, :])
    o_ref[0] = x_ref[0] + gt_ref[0, 0] * _rms(mix, gp_ref[...])


def _outproj(o_att, o_rnn, g_att, g_rnn, w_out, x, mod6, g_post, tm=512):
    bsz, s, d = x.shape
    row = lambda width: pl.BlockSpec((1, tm, width), lambda b, i: (b, i, 0))
    vec = lambda width: pl.BlockSpec((1, width), lambda b, i: (0, 0))
    return pl.pallas_call(
        _outproj_kernel,
        grid=(bsz, s // tm),
        in_specs=[
            row(D_ATT), row(D_RNN), vec(D_ATT), vec(D_RNN),
            pl.BlockSpec((D_ATT + D_RNN, d), lambda b, i: (0, 0), pipeline_mode=pl.Buffered(1)),
            row(d),
            pl.BlockSpec((1, 1, 1, d), lambda b, i: (2, b, 0, 0)),
            vec(d),
        ],
        out_specs=row(d),
        out_shape=jax.ShapeDtypeStruct((bsz, s, d), F32),
        compiler_params=_params(("parallel", "arbitrary")),
        name="out_proj",
    )(o_att, o_rnn, g_att.reshape(1, -1), g_rnn.reshape(1, -1), w_out, x, mod6, g_post.reshape(1, d))


def _mlp_kernel(x_ref, g_ref, sc_ref, sh_ref, w1_ref, w2_ref, gt_ref, gp_ref, o_ref, h_ref, a_ref):
    j = pl.program_id(2)
    n_chunks = pl.num_programs(2) - 1
    slot = j % 2

    def up():
        u = jnp.maximum(_dot(h_ref[...], w1_ref[...]), 0.0)
        a_ref[slot] = (u * u).astype(BF16)

    def down():
        o_ref[0] += _dot(a_ref[1 - slot], w2_ref[...])

    @pl.when(j == 0)
    def _():
        h = _rms(x_ref[0], g_ref[...]) * (1.0 + sc_ref[0, 0]) + sh_ref[0, 0]
        h_ref[...] = h.astype(BF16)
        o_ref[0] = jnp.zeros(o_ref.shape[1:], F32)
        up()

    @pl.when((j > 0) & (j < n_chunks))
    def _():
        down()
        up()

    @pl.when(j == n_chunks)
    def _():
        down()
        o_ref[0] = x_ref[0] + gt_ref[0, 0] * _rms(o_ref[0], gp_ref[...])


def _mlp(x, g_pre, mod6, w1, w2, g_post, tm=1024, tf=1024):
    bsz, s, d = x.shape
    n_chunks = w1.shape[1] // tf
    row = pl.BlockSpec((1, tm, d), lambda b, i, j: (b, i, 0))
    vec = pl.BlockSpec((1, d), lambda b, i, j: (0, 0))
    modk = lambda k: pl.BlockSpec((1, 1, 1, d), lambda b, i, j: (k, b, 0, 0))
    return pl.pallas_call(
        _mlp_kernel,
        grid=(bsz, s // tm, n_chunks + 1),
        in_specs=[
            row, vec, modk(4), modk(3),
            pl.BlockSpec((d, tf), lambda b, i, j: (0, jnp.minimum(j, n_chunks - 1))),
            pl.BlockSpec((tf, d), lambda b, i, j: (jnp.maximum(j - 1, 0), 0)),
            modk(5), vec,
        ],
        out_specs=row,
        out_shape=jax.ShapeDtypeStruct((bsz, s, d), F32),
        scratch_shapes=[pltpu.VMEM((tm, d), BF16), pltpu.VMEM((2, tm, tf), BF16)],
        compiler_params=_params(("parallel", "parallel", "arbitrary")),
        name="mlp",
    )(x, g_pre.reshape(1, d), mod6, mod6, w1, w2, mod6, g_post.reshape(1, d))


def _layer(x, c, w_ada, b_ada, g_pre_mix, g_post_mix, g_pre_mlp, g_post_mlp, w_in_perm,
           cmp_w1_k, cmp_w2_k, cmp_pe_k, cmp_w1_v, cmp_w2_v, cmp_pe_v,
           conv_w, conv_b, w_rg_a, b_rg_a, w_rg_x, b_rg_x, lru_lambda,
           g_grp_att, g_grp_rnn, w_out, w_ff1, w_ff2):
    bsz, s, d = x.shape
    dh = HEAD_DIM

    mod = _ada(c, w_ada, b_ada)
    mod6 = mod.reshape(bsz, 6, 1, d).transpose(1, 0, 2, 3)

    qt, kvc, ks, kw, vst, vwt, r_in, glt = _inproj(x, g_pre_mix, mod6, w_in_perm)

    def pair_diag(w, axis):
        z = jnp.zeros_like(w)
        return jnp.concatenate([jnp.concatenate([w, z], axis=-1), jnp.concatenate([z, w], axis=-1)], axis=axis)

    n_pair = CMP_STRIDE // 2
    pe = jnp.tile(jnp.stack([cmp_pe_k, cmp_pe_v]), (1, 1, 2)).reshape(2, 2, n_pair, 4 * dh)
    w1c = pair_diag(jnp.stack([cmp_w1_k, cmp_w1_v]).astype(BF16).reshape(2, CMP_BLOCK, dh, -1), 2)
    w1c = w1c.reshape(2, 2, n_pair, 4 * dh, w1c.shape[-1])
    w2c = pair_diag(jnp.stack([cmp_w2_k, cmp_w2_v]).astype(BF16), 1)
    kv_cmp = _compress(kvc, pe, w1c, w2c)

    o_cmp, sel_bias = _cmp_select(qt, kv_cmp)
    o_att = _attention(qt, o_cmp, sel_bias, ks, kw, vst, vwt, glt)

    w_cat = jnp.concatenate([w_rg_a, w_rg_x], axis=-1).astype(BF16)
    b_cat = jnp.concatenate([b_rg_a.reshape(RNN_BLOCKS, 1, RNN_BLOCK_DIM),
                             b_rg_x.reshape(RNN_BLOCKS, 1, RNN_BLOCK_DIM)], axis=-1)
    o_rnn = _rglru(r_in, conv_w, conv_b.reshape(1, -1), w_cat, b_cat, lru_lambda.reshape(1, -1))

    x1 = _outproj(o_att, o_rnn, g_grp_att, g_grp_rnn, w_out.astype(BF16), x, mod6, g_post_mix)
    return _mlp(x1, g_pre_mlp, mod6, w_ff1.astype(BF16), w_ff2.astype(BF16), g_post_mlp)


def kernel(x, c, w_ada, b_ada, g_pre_mix, g_post_mix, g_pre_mlp, g_post_mlp, w_in, cmp_w1_k, cmp_w2_k, cmp_pe_k, cmp_w1_v, cmp_w2_v, cmp_pe_v, conv_w, conv_b, w_rg_a, b_rg_a, w_rg_x, b_rg_x, lru_lambda, g_grp_att, g_grp_rnn, w_out, w_ff1, w_ff2):
    depth = w_ada.shape[0]
    for l in range(depth):
        x = _layer(x, c, w_ada[l], b_ada[l], g_pre_mix[l], g_post_mix[l], g_pre_mlp[l], g_post_mlp[l],
                   _w_in_prep(w_in, l), cmp_w1_k[l], cmp_w2_k[l], cmp_pe_k[l], cmp_w1_v[l], cmp_w2_v[l], cmp_pe_v[l],
                   conv_w[l], conv_b[l], w_rg_a[l], b_rg_a[l], w_rg_x[l], b_rg_x[l], lru_lambda[l],
                   g_grp_att[l], g_grp_rnn[l], w_out[l], w_ff1[l], w_ff2[l])
    return x
```

```python
import jax
import jax.numpy as jnp
from jax import lax
from jax.experimental import pallas as pl
from jax.experimental.pallas import tpu as pltpu

F32 = jnp.float32
BF16 = jnp.bfloat16

D_MODEL = 2048
D_ATT = 1024
D_RNN = 1024
N_Q_HEADS = 16
N_KV_GROUPS = 4
HEADS_PER_GROUP = 4
HEAD_DIM = 64
D_KV = 256
CMP_BLOCK = 32
CMP_STRIDE = 16
SEL_BLOCK = 64
SEL_SHIFT = 6
SEL_TOP_N = 8
WINDOW = 512
N_BRANCH = 3
RNN_BLOCKS = 8
RNN_BLOCK_DIM = 128
CONV_WIDTH = 4
LRU_C = 8.0
D_FF = 4 * D_MODEL
EPS = 1e-6
NEG = -1e30
FORCE_SCORE = 1e9
LOG2_E = 1.4426950408889634

LANES = 128
SUBLANES = 8
VMEM_LIMIT = 60 * 1024 * 1024

TQ = 256
N_CMP_PAD = 128
N_SEL = 32
GL_PAD = 128
TT = 128
assert WINDOW == 2 * TQ


def _params(sem):
    return pltpu.CompilerParams(dimension_semantics=sem, vmem_limit_bytes=VMEM_LIMIT)


def _dot(a, b):
    return jnp.dot(a, b, preferred_element_type=F32)


def _rms(x, g):
    return x * lax.rsqrt(jnp.mean(x * x, axis=-1, keepdims=True) + EPS) * g


def _ada_kernel(c_ref, w_ref, b_ref, o_ref):
    c = c_ref[...]
    ca = (c * jax.nn.sigmoid(c)).astype(BF16)
    o_ref[...] = _dot(ca, w_ref[...].astype(BF16)) + b_ref[...]


def _ada(c, w, b):
    bsz, d = c.shape
    n = w.shape[1]
    tn = 1024
    return pl.pallas_call(
        _ada_kernel,
        grid=(n // tn,),
        in_specs=[
            pl.BlockSpec((bsz, d), lambda j: (0, 0)),
            pl.BlockSpec((d, tn), lambda j: (0, j)),
            pl.BlockSpec((1, tn), lambda j: (0, j)),
        ],
        out_specs=pl.BlockSpec((bsz, tn), lambda j: (0, j)),
        out_shape=jax.ShapeDtypeStruct((bsz, n), F32),
        compiler_params=_params(("arbitrary",)),
        name="ada_mod",
    )(c, w, b.reshape(1, n))


C_Q = (0, D_ATT)
C_CMP = (C_Q[1], C_Q[1] + 2 * D_KV)
C_K = (C_CMP[1], C_CMP[1] + 2 * D_KV)
C_V = (C_K[1], C_K[1] + 2 * D_KV)
C_R = (C_V[1], C_V[1] + 2 * D_RNN)
C_GL = (C_R[1], C_R[1] + GL_PAD)
D_IN_PAD = C_GL[1]
K_ROW = LANES
V_ROWS = HEAD_DIM + 16
N_GATE = HEADS_PER_GROUP * N_BRANCH


def _w_in_prep_kernel(wt_ref, o_ref):
    seg = lambda lo, hi: wt_ref[0, lo:hi, :].T.astype(BF16)
    kv0 = D_ATT
    gl0 = D_ATT + 6 * D_KV
    n_gl = N_BRANCH * N_Q_HEADS
    o_ref[:, 0:C_K[0] + D_KV] = seg(0, kv0 + 3 * D_KV)
    o_ref[:, C_K[0] + D_KV:C_K[1]] = seg(kv0 + 4 * D_KV, kv0 + 5 * D_KV)
    o_ref[:, C_V[0]:C_V[0] + D_KV] = seg(kv0 + 3 * D_KV, kv0 + 4 * D_KV)
    o_ref[:, C_V[0] + D_KV:C_V[1]] = seg(kv0 + 5 * D_KV, kv0 + 6 * D_KV)
    o_ref[:, C_R[0]:C_R[1]] = seg(gl0 + n_gl, gl0 + n_gl + 2 * D_RNN)
    gl_tile = wt_ref[0, gl0:gl0 + GL_PAD, :].T
    lane = lax.broadcasted_iota(jnp.int32, gl_tile.shape, 1)
    o_ref[:, C_GL[0]:C_GL[1]] = jnp.where(lane < n_gl, gl_tile, 0.0).astype(BF16)


def _w_in_prep(w_in_all, layer, tk=256):
    _, d, n = w_in_all.shape
    w_t = jnp.swapaxes(w_in_all, 1, 2)
    return pl.pallas_call(
        _w_in_prep_kernel,
        grid=(d // tk,),
        in_specs=[pl.BlockSpec((1, n, tk), lambda i: (layer, 0, i))],
        out_specs=pl.BlockSpec((tk, D_IN_PAD), lambda i: (i, 0)),
        out_shape=jax.ShapeDtypeStruct((d, D_IN_PAD), BF16),
        compiler_params=_params(("arbitrary",)),
        name="w_in_prep",
    )(w_t)


def _inproj_kernel(x_ref, g_ref, sc_ref, sh_ref, w_ref,
                   qt_ref, cmp_ref, ks_ref, kw_ref, vst_ref, vwt_ref, r_ref, glt_ref):
    tm = x_ref.shape[1]
    ng, hg, dh = N_KV_GROUPS, HEADS_PER_GROUP, HEAD_DIM
    x = x_ref[0]
    h = _rms(x, g_ref[...]) * (1.0 + sc_ref[0, 0]) + sh_ref[0, 0]
    hb = h.astype(BF16)
    tiles = [slice(u * TQ, (u + 1) * TQ) for u in range(tm // TQ)]

    q_t = (_dot(hb, w_ref[:, C_Q[0]:C_Q[1]]) * (dh ** -0.5 * LOG2_E)).T.astype(BF16)
    for j in range(ng):
        for u, cols in enumerate(tiles):
            qt_ref[0, j, u] = jnp.concatenate(
                [q_t[(hg * j + hh) * dh:(hg * j + hh + 1) * dh, cols] for hh in range(hg)], axis=1)

    kv_cmp = _dot(hb, w_ref[:, C_CMP[0]:C_CMP[1]])
    for c in range(2 * D_KV // LANES):
        cmp_ref[0, c] = kv_cmp[:, c * LANES:(c + 1) * LANES]

    keys = _dot(hb, w_ref[:, C_K[0]:C_K[1]])
    row_pos = pl.program_id(1) * tm + lax.broadcasted_iota(jnp.int32, (tm, K_ROW - dh), 0)
    lane = lax.broadcasted_iota(jnp.int32, (tm, K_ROW - dh), 1)
    onehot = jnp.where(lax.shift_right_logical(row_pos, SEL_SHIFT) == lane, 1.0, 0.0)
    for j in range(ng):
        ks_ref[0, j] = jnp.concatenate([keys[:, j * dh:(j + 1) * dh], onehot], axis=1).astype(BF16)
        kw_ref[0, j] = jnp.concatenate(
            [keys[:, D_KV + j * dh:D_KV + (j + 1) * dh], onehot], axis=1).astype(BF16)

    v_t = _dot(hb, w_ref[:, C_V[0]:C_V[1]]).T.astype(BF16)
    ones = jnp.ones((V_ROWS - dh, TQ), BF16)
    for j in range(ng):
        for u, cols in enumerate(tiles):
            vst_ref[0, j, u] = jnp.concatenate([v_t[j * dh:(j + 1) * dh, cols], ones], axis=0)
            vwt_ref[0, j, u] = jnp.concatenate([v_t[D_KV + j * dh:D_KV + (j + 1) * dh, cols], ones], axis=0)

    half = (C_R[0] + C_R[1]) // 2
    r_ref[0, :, 0:D_RNN] = _dot(hb, w_ref[:, C_R[0]:half])
    r_ref[0, :, D_RNN:2 * D_RNN] = _dot(hb, w_ref[:, half:C_R[1]])

    gl_t = _dot(hb, w_ref[:, C_GL[0]:C_GL[1]]).T
    for j in range(ng):
        glt_ref[0, j] = gl_t[N_GATE * j:N_GATE * (j + 1), :]


def _inproj(x, g, mod6, w_perm, tm=512):
    bsz, s, d = x.shape
    ng = N_KV_GROUPS
    nt = tm // TQ
    row = lambda width: pl.BlockSpec((1, tm, width), lambda b, i: (b, i, 0))
    grp_rows = pl.BlockSpec((1, ng, tm, K_ROW), lambda b, i: (b, 0, i, 0))
    grp_tiles = lambda r, c: pl.BlockSpec((1, ng, nt, r, c), lambda b, i: (b, 0, i, 0, 0))
    sds = jax.ShapeDtypeStruct
    return pl.pallas_call(
        _inproj_kernel,
        grid=(bsz, s // tm),
        in_specs=[
            row(d),
            pl.BlockSpec((1, d), lambda b, i: (0, 0)),
            pl.BlockSpec((1, 1, 1, d), lambda b, i: (1, b, 0, 0)),
            pl.BlockSpec((1, 1, 1, d), lambda b, i: (0, b, 0, 0)),
            pl.BlockSpec((d, D_IN_PAD), lambda b, i: (0, 0), pipeline_mode=pl.Buffered(1)),
        ],
        out_specs=[
            grp_tiles(HEAD_DIM, HEADS_PER_GROUP * TQ),
            pl.BlockSpec((1, 2 * D_KV // LANES, tm, LANES), lambda b, i: (b, 0, i, 0)),
            grp_rows, grp_rows,
            grp_tiles(V_ROWS, TQ), grp_tiles(V_ROWS, TQ), row(2 * D_RNN),
            pl.BlockSpec((1, ng, N_GATE, tm), lambda b, i: (b, 0, 0, i)),
        ],
        out_shape=[
            sds((bsz, ng, s // TQ, HEAD_DIM, HEADS_PER_GROUP * TQ), BF16),
            sds((bsz, 2 * D_KV // LANES, s, LANES), F32),
            sds((bsz, ng, s, K_ROW), BF16), sds((bsz, ng, s, K_ROW), BF16),
            sds((bsz, ng, s // TQ, V_ROWS, TQ), BF16), sds((bsz, ng, s // TQ, V_ROWS, TQ), BF16),
            sds((bsz, s, 2 * D_RNN), F32),
            sds((bsz, ng, N_GATE, s), F32),
        ],
        compiler_params=_params(("parallel", "arbitrary")),
        name="in_proj",
    )(x, g.reshape(1, d), mod6, mod6, w_perm)


def _compress_kernel(x_ref, pe_ref, w1_ref, w2_ref, o_ref):
    bsz = x_ref.shape[0]
    dh = HEAD_DIM
    n_rows = x_ref.shape[2] // CMP_STRIDE
    rows = bsz * n_rows
    first = jnp.zeros((rows, w1_ref.shape[4]), F32)
    second = jnp.zeros((rows, w1_ref.shape[4]), F32)
    for lp in range(CMP_STRIDE // 2):
        x_2 = jnp.concatenate(
            [x_ref[:, 0, pl.ds(2 * lp + k, n_rows, stride=CMP_STRIDE), :].reshape(rows, LANES) for k in range(2)],
            axis=1)
        first = first + _dot((x_2 + pe_ref[0, 0, lp:lp + 1, :]).astype(BF16), w1_ref[0, 0, lp])
        second = second + _dot((x_2 + pe_ref[0, 1, lp:lp + 1, :]).astype(BF16), w1_ref[0, 1, lp])
    hid = first + pltpu.roll(second, rows - 1, axis=0)
    out = _dot(jax.nn.gelu(hid).astype(BF16), w2_ref[0]).astype(BF16)
    for b in range(bsz):
        for j in range(2):
            o_ref[0, b, j] = out[b * n_rows:(b + 1) * n_rows, j * dh:(j + 1) * dh]


def _compress(kvc4, pe, w1, w2):
    bsz, n_tiles, s, _ = kvc4.shape
    n_rows = s // CMP_STRIDE
    per_kind = lambda a: pl.BlockSpec((1,) + a.shape[1:], lambda t, p: (t,) + (0,) * (a.ndim - 1))
    return pl.pallas_call(
        _compress_kernel,
        grid=(2, n_tiles // 2),
        in_specs=[pl.BlockSpec((bsz, 1, s, LANES), lambda t, p: (0, 2 * t + p, 0, 0)),
                  per_kind(pe), per_kind(w1), per_kind(w2)],
        out_specs=pl.BlockSpec((1, bsz, 2, n_rows, HEAD_DIM), lambda t, p: (t, 0, p, 0, 0)),
        out_shape=jax.ShapeDtypeStruct((2, bsz, N_KV_GROUPS, n_rows, HEAD_DIM), BF16),
        compiler_params=_params(("arbitrary", "arbitrary")),
        name="compress_kv",
    )(kvc4, pe, w1, w2)


def _cmp_select_kernel(qt_ref, kc_ref, vc_ref, oc_ref, bias_ref):
    hg = HEADS_PER_GROUP
    kc = kc_ref[0, 0, 0]
    vct = vc_ref[0, 0, 0].astype(F32).T.astype(BF16)
    jj = lax.broadcasted_iota(jnp.int32, (N_SEL, N_CMP_PAD), 0) * SEL_BLOCK
    nn = lax.broadcasted_iota(jnp.int32, (N_SEL, N_CMP_PAD), 1) * CMP_STRIDE
    ov = jnp.minimum(nn + CMP_BLOCK, jj + SEL_BLOCK) - jnp.maximum(nn, jj)
    w_sel = (jnp.maximum(ov, 0).astype(F32) * (1.0 / CMP_BLOCK)).astype(BF16)
    nrow = lax.broadcasted_iota(jnp.int32, (N_CMP_PAD, TQ), 0)
    blk = lax.broadcasted_iota(jnp.int32, (N_SEL, TQ), 0)

    for qi in range(qt_ref.shape[2]):
        qt = qt_ref[0, 0, qi]
        pos = qi * TQ + lax.broadcasted_iota(jnp.int32, (1, TQ), 1)
        mask_c = nrow * CMP_STRIDE + (CMP_BLOCK - 1) <= pos

        def per_head_where(a, fill):
            return jnp.concatenate(
                [jnp.where(mask_c, a[:, h * TQ:(h + 1) * TQ], fill) for h in range(hg)], axis=1)

        s = per_head_where(_dot(kc, qt), NEG)
        e = jnp.exp2(s - jnp.max(s, axis=0, keepdims=True))
        p = per_head_where(e * (1.0 / jnp.sum(e, axis=0, keepdims=True)), 0.0)
        oc_ref[0, 0, qi] = _dot(vct, p.astype(BF16))
        psum = p[:, 0:TQ]
        for h in range(1, hg):
            psum = psum + p[:, h * TQ:(h + 1) * TQ]

        p_hi = psum.astype(BF16)
        p_lo = (psum - p_hi.astype(F32)).astype(BF16)
        imp = _dot(w_sel, p_hi) + _dot(w_sel, p_lo)
        cur = lax.shift_right_logical(pos, SEL_SHIFT)
        free = jnp.where(blk * SEL_BLOCK <= pos, imp, -FORCE_SCORE)
        score = jnp.where(blk == 0, FORCE_SCORE,
                          jnp.where(blk == cur, FORCE_SCORE, jnp.where(blk == cur - 1, FORCE_SCORE, free)))
        rank = jnp.zeros((N_SEL, TQ), F32)
        for k in range(N_SEL):
            sk = score[k:k + 1, :]
            tie = jnp.where(blk > k, 1.0, 0.0)
            rank = rank + jnp.where(sk > score, 1.0, jnp.where(sk == score, tie, 0.0))
        bias_ref[0, 0, qi] = jnp.where(rank < SEL_TOP_N, 0.0, NEG).astype(BF16)


def _cmp_select(qt, kv_cmp):
    bsz, g, nq, dh, wide = qt.shape
    per_bg = lambda *shape: pl.BlockSpec((1, 1) + shape, lambda b, j: (b, j) + (0,) * len(shape))
    kind = lambda t: pl.BlockSpec((1, 1, 1, N_CMP_PAD, dh), lambda b, j: (t, b, j, 0, 0))
    return pl.pallas_call(
        _cmp_select_kernel,
        grid=(bsz, g),
        in_specs=[per_bg(nq, dh, wide), kind(0), kind(1)],
        out_specs=[per_bg(nq, dh, wide), per_bg(nq, N_SEL, TQ)],
        out_shape=[jax.ShapeDtypeStruct((bsz, g, nq, dh, wide), F32),
                   jax.ShapeDtypeStruct((bsz, g, nq, N_SEL, TQ), BF16)],
        compiler_params=_params(("parallel", "arbitrary")),
        name="cmp_select",
    )(qt, kv_cmp, kv_cmp)


def _attn_kernel(qt_ref, oc_ref, bias_ref, ks_ref, kw_ref, vst_ref, vwt_ref, gl_ref, o_ref,
                 m_ref, acc_ref, sc_ref):
    qi = pl.program_id(2)
    last = pl.num_programs(2) - 1
    hg = HEADS_PER_GROUP
    dh = HEAD_DIM
    wide = hg * TQ
    n_grp = qt_ref.shape[1]
    qts = [qt_ref[0, gg, 0] for gg in range(n_grp)]

    def per_head_where(mask, a, fill):
        return jnp.concatenate(
            [jnp.where(mask, a[:, h * TQ:(h + 1) * TQ], fill) for h in range(hg)], axis=1)

    def block_bias(value):
        return jnp.full((N_SEL, wide), value, F32).astype(BF16)

    zero_rows = jnp.zeros((K_ROW - dh - N_SEL, wide), BF16)

    def scores(k_ref, gg, kt, bias):
        k_t = k_ref[0, gg, pl.ds(pl.multiple_of(kt * TQ, TQ), TQ), :]
        return _dot(k_t, jnp.concatenate([qts[gg], bias, zero_rows], axis=0))

    rel = (lax.broadcasted_iota(jnp.int32, (TQ, TQ), 0)
           - lax.broadcasted_iota(jnp.int32, (TQ, TQ), 1))

    def online(state, sc, v_t):
        m_tile = jnp.max(sc, axis=0, keepdims=True)
        if state is None:
            return m_tile, _dot(v_t, jnp.exp2(sc - m_tile).astype(BF16))
        m_old, acc_old = state
        m_new = jnp.maximum(m_old, m_tile)
        return m_new, jnp.exp2(m_old - m_new) * acc_old + _dot(v_t, jnp.exp2(sc - m_new).astype(BF16))

    k1 = jnp.maximum(qi - 1, 0)
    k2 = jnp.maximum(qi - 2, 0)
    bias_sel = []
    for gg in range(n_grp):
        s0 = per_head_where(rel <= 0, scores(kw_ref, gg, qi, block_bias(0.0)), NEG)
        s1 = scores(kw_ref, gg, k1, block_bias(jnp.where(qi >= 1, 0.0, NEG)))
        s2 = per_head_where(rel > 0, scores(kw_ref, gg, k2, block_bias(jnp.where(qi >= 2, 0.0, NEG))), NEG)
        win = online(None, s0, vwt_ref[0, gg, qi])
        win = online(win, s1, vwt_ref[0, gg, k1])
        win = online(win, s2, vwt_ref[0, gg, k2])
        acc_ref[gg, 1] = win[1]
        bias_sel.append(jnp.concatenate([bias_ref[0, gg, 0]] * hg, axis=1))

    def sel_scores(gg, kt):
        pad_bias = block_bias(jnp.where(kt < qi, 0.0, NEG))
        return scores(ks_ref, gg, jnp.minimum(kt, last), jnp.minimum(bias_sel[gg], pad_bias))

    def sel_accumulate(gg, sc, kt):
        m_ref[gg], acc_ref[gg, 0] = online((m_ref[gg], acc_ref[gg, 0]), sc,
                                           vst_ref[0, gg, jnp.minimum(kt, last)])

    for gg in range(n_grp):
        sd = per_head_where(rel <= 0, scores(ks_ref, gg, qi, bias_sel[gg]), NEG)
        m_ref[gg], acc_ref[gg, 0] = online(None, sd, vst_ref[0, gg, qi])
        sc_ref[gg, 0] = sel_scores(gg, 0)

    def sel_pair(pi, carry):
        kt = 2 * pi
        for gg in range(n_grp):
            sc_ref[gg, 1] = sel_scores(gg, kt + 1)
            sel_accumulate(gg, sc_ref[gg, 0], kt)
        for gg in range(n_grp):
            sc_ref[gg, 0] = sel_scores(gg, kt + 2)
            sel_accumulate(gg, sc_ref[gg, 1], kt + 1)
        return carry

    lax.fori_loop(0, (qi + 1) // 2, sel_pair, 0)

    outs = []
    for gg in range(n_grp):
        gate = jax.nn.sigmoid(gl_ref[0, gg])
        o_cmp = oc_ref[0, gg, 0]
        acc_s = acc_ref[gg, 0]
        acc_w = acc_ref[gg, 1]
        o_sel = acc_s[0:dh, :] * (1.0 / acc_s[dh:dh + 1, :])
        o_win = acc_w[0:dh, :] * (1.0 / acc_w[dh:dh + 1, :])
        for h in range(hg):
            lanes = slice(h * TQ, (h + 1) * TQ)
            g0 = gate[3 * h:3 * h + 1, :]
            g1 = gate[3 * h + 1:3 * h + 2, :]
            g2 = gate[3 * h + 2:3 * h + 3, :]
            outs.append(g0 * o_cmp[:, lanes] + g1 * o_sel[:, lanes] + g2 * o_win[:, lanes])
    o_ref[0] = jnp.concatenate(outs, axis=0).T


def _attention(qt, o_cmp, bias, ks, kw, vst, vwt, glt):
    bsz, g, nq = qt.shape[:3]
    s = nq * TQ
    width = HEADS_PER_GROUP * HEAD_DIM
    wide = HEADS_PER_GROUP * TQ
    gs = 4
    per_bg = lambda *shape: pl.BlockSpec((1, gs) + shape, lambda b, j, i: (b, j) + (0,) * len(shape))
    per_tile = lambda *shape: pl.BlockSpec((1, gs, 1) + shape, lambda b, j, i: (b, j, i) + (0,) * len(shape))
    return pl.pallas_call(
        _attn_kernel,
        grid=(bsz, g // gs, nq),
        in_specs=[
            per_tile(HEAD_DIM, wide),
            per_tile(HEAD_DIM, wide),
            per_tile(N_SEL, TQ),
            per_bg(s, K_ROW),
            per_bg(s, K_ROW),
            per_bg(nq, V_ROWS, TQ),
            per_bg(nq, V_ROWS, TQ),
            pl.BlockSpec((1, gs, HEADS_PER_GROUP * N_BRANCH, TQ), lambda b, j, i: (b, j, 0, i)),
        ],
        out_specs=pl.BlockSpec((1, TQ, gs * width), lambda b, j, i: (b, i, j)),
        out_shape=jax.ShapeDtypeStruct((bsz, s, D_ATT), F32),
        scratch_shapes=[
            pltpu.VMEM((gs, 1, wide), F32),
            pltpu.VMEM((gs, 2, V_ROWS, wide), F32),
            pltpu.VMEM((gs, 2, TQ, wide), F32),
        ],
        compiler_params=_params(("parallel", "parallel", "arbitrary")),
        name="nsa_attention",
    )(qt, o_cmp, bias, ks, kw, vst, vwt, glt)


def _rglru_kernel(x_ref, y_ref, cw_ref, cb_ref, w_ref, b_ref, lam_ref, o_ref, xpad_ref):
    s = x_ref.shape[1]
    c = RNN_BLOCK_DIM
    pad = SUBLANES
    xpad_ref[0:pad, :] = jnp.zeros((pad, c), F32)
    xpad_ref[pad:pad + s, :] = x_ref[0]
    cw = cw_ref[...]
    w = w_ref[0]
    bias = b_ref[0]
    lam = lam_ref[...]
    log_a_per_gate = -LRU_C * (jnp.maximum(-lam, 0.0) + jnp.log1p(jnp.exp(-jnp.abs(lam))))
    sub = lax.broadcasted_iota(jnp.int32, (TT, c), 0) & (SUBLANES - 1)
    h = jnp.zeros((1, c), F32)

    for ci in range(s // TT):
        t0 = ci * TT
        xc = cb_ref[...] + sum(
            xpad_ref[t0 + pad - (CONV_WIDTH - 1) + k:t0 + pad - (CONV_WIDTH - 1) + k + TT, :] * cw[k:k + 1, :]
            for k in range(CONV_WIDTH))
        gates = jax.nn.sigmoid(_dot(xc.astype(BF16), w) + bias)
        r = gates[:, 0:c]
        i = gates[:, c:2 * c]
        log_a = r * log_a_per_gate
        a = jnp.exp(log_a)
        var = -jnp.tanh(log_a) * (a * a + 1.0)
        bt = jnp.where(var > 0.0, var * lax.rsqrt(var), 0.0) * (i * xc)
        for d in (1, 2, 4):
            keep = sub >= d
            a_prev = jnp.where(keep, pltpu.roll(a, d, axis=0), 1.0)
            b_prev = jnp.where(keep, pltpu.roll(bt, d, axis=0), 0.0)
            bt = bt + a * b_prev
            a = a * a_prev
        states = []
        for gi in range(TT // SUBLANES):
            grp = slice(gi * SUBLANES, (gi + 1) * SUBLANES)
            hg = bt[grp, :] + a[grp, :] * h
            states.append(hg)
            h = hg[SUBLANES - 1:SUBLANES, :]
        o_ref[0, t0:t0 + TT, :] = jax.nn.gelu(y_ref[0, t0:t0 + TT, :]) * jnp.concatenate(states, axis=0)


def _rglru(r_in, conv_w, conv_b, w_cat, b_cat, lam):
    bsz, s, _ = r_in.shape
    c = RNN_BLOCK_DIM
    nb = RNN_BLOCKS
    return pl.pallas_call(
        _rglru_kernel,
        grid=(bsz, nb),
        in_specs=[
            pl.BlockSpec((1, s, c), lambda b, j: (b, 0, j)),
            pl.BlockSpec((1, s, c), lambda b, j: (b, 0, nb + j)),
            pl.BlockSpec((CONV_WIDTH, c), lambda b, j: (0, j)),
            pl.BlockSpec((1, c), lambda b, j: (0, j)),
            pl.BlockSpec((1, c, 2 * c), lambda b, j: (j, 0, 0)),
            pl.BlockSpec((1, 1, 2 * c), lambda b, j: (j, 0, 0)),
            pl.BlockSpec((1, c), lambda b, j: (0, j)),
        ],
        out_specs=pl.BlockSpec((1, s, c), lambda b, j: (b, 0, j)),
        out_shape=jax.ShapeDtypeStruct((bsz, s, D_RNN), F32),
        scratch_shapes=[pltpu.VMEM((s + SUBLANES, c), F32)],
        compiler_params=_params(("parallel", "arbitrary")),
        name="rg_lru",
    )(r_in, r_in, conv_w, conv_b, w_cat, b_cat, lam)


def _outproj_kernel(oa_ref, or_ref, ga_ref, gr_ref, w_ref, x_ref, gt_ref, gp_ref, o_ref):
    a = _rms(oa_ref[0], ga_ref[...]).astype(BF16)
    r = _rms(or_ref[0], gr_ref[...]).astype(BF16)
    mix = _dot(a, w_ref[0:D_ATT, :]) + _dot(r, w_ref[D_ATT:D_ATT + D_RNN, :])
    o_ref[0] = x_ref[0] + gt_ref[0, 0] * _rms(mix, gp_ref[...])


def _outproj(o_att, o_rnn, g_att, g_rnn, w_out, x, mod6, g_post, tm=512):
    bsz, s, d = x.shape
    row = lambda width: pl.BlockSpec((1, tm, width), lambda b, i: (b, i, 0))
    vec = lambda width: pl.BlockSpec((1, width), lambda b, i: (0, 0))
    return pl.pallas_call(
        _outproj_kernel,
        grid=(bsz, s // tm),
        in_specs=[
            row(D_ATT), row(D_RNN), vec(D_ATT), vec(D_RNN),
            pl.BlockSpec((D_ATT + D_RNN, d), lambda b, i: (0, 0), pipeline_mode=pl.Buffered(1)),
            row(d),
            pl.BlockSpec((1, 1, 1, d), lambda b, i: (2, b, 0, 0)),
            vec(d),
        ],
        out_specs=row(d),
        out_shape=jax.ShapeDtypeStruct((bsz, s, d), F32),
        compiler_params=_params(("parallel", "arbitrary")),
        name="out_proj",
    )(o_att, o_rnn, g_att.reshape(1, -1), g_rnn.reshape(1, -1), w_out, x, mod6, g_post.reshape(1, d))


def _mlp_kernel(x_ref, g_ref, sc_ref, sh_ref, w1_ref, w2_ref, gt_ref, gp_ref, o_ref, h_ref, a_ref):
    j = pl.program_id(2)
    n_chunks = pl.num_programs(2) - 1
    slot = j % 2

    def up():
        u = jnp.maximum(_dot(h_ref[...], w1_ref[...]), 0.0)
        a_ref[slot] = (u * u).astype(BF16)

    def down():
        o_ref[0] += _dot(a_ref[1 - slot], w2_ref[...])

    @pl.when(j == 0)
    def _():
        h = _rms(x_ref[0], g_ref[...]) * (1.0 + sc_ref[0, 0]) + sh_ref[0, 0]
        h_ref[...] = h.astype(BF16)
        o_ref[0] = jnp.zeros(o_ref.shape[1:], F32)
        up()

    @pl.when((j > 0) & (j < n_chunks))
    def _():
        down()
        up()

    @pl.when(j == n_chunks)
    def _():
        down()
        o_ref[0] = x_ref[0] + gt_ref[0, 0] * _rms(o_ref[0], gp_ref[...])


def _mlp(x, g_pre, mod6, w1, w2, g_post, tm=1024, tf=1024):
    bsz, s, d = x.shape
    n_chunks = w1.shape[1] // tf
    row = pl.BlockSpec((1, tm, d), lambda b, i, j: (b, i, 0))
    vec = pl.BlockSpec((1, d), lambda b, i, j: (0, 0))
    modk = lambda k: pl.BlockSpec((1, 1, 1, d), lambda b, i, j: (k, b, 0, 0))
    return pl.pallas_call(
        _mlp_kernel,
        grid=(bsz, s // tm, n_chunks + 1),
        in_specs=[
            row, vec, modk(4), modk(3),
            pl.BlockSpec((d, tf), lambda b, i, j: (0, jnp.minimum(j, n_chunks - 1))),
            pl.BlockSpec((tf, d), lambda b, i, j: (jnp.maximum(j - 1, 0), 0)),
            modk(5), vec,
        ],
        out_specs=row,
        out_shape=jax.ShapeDtypeStruct((bsz, s, d), F32),
        scratch_shapes=[pltpu.VMEM((tm, d), BF16), pltpu.VMEM((2, tm, tf), BF16)],
        compiler_params=_params(("parallel", "parallel", "arbitrary")),
        name="mlp",
    )(x, g_pre.reshape(1, d), mod6, mod6, w1, w2, mod6, g_post.reshape(1, d))


def _layer(x, c, w_ada, b_ada, g_pre_mix, g_post_mix, g_pre_mlp, g_post_mlp, w_in_perm,
           cmp_w1_k, cmp_w2_k, cmp_pe_k, cmp_w1_v, cmp_w2_v, cmp_pe_v,
           conv_w, conv_b, w_rg_a, b_rg_a, w_rg_x, b_rg_x, lru_lambda,
           g_grp_att, g_grp_rnn, w_out, w_ff1, w_ff2):
    bsz, s, d = x.shape
    dh = HEAD_DIM

    mod = _ada(c, w_ada, b_ada)
    mod6 = mod.reshape(bsz, 6, 1, d).transpose(1, 0, 2, 3)

    qt, kvc, ks, kw, vst, vwt, r_in, glt = _inproj(x, g_pre_mix, mod6, w_in_perm)

    def pair_diag(w, axis):
        z = jnp.zeros_like(w)
        return jnp.concatenate([jnp.concatenate([w, z], axis=-1), jnp.concatenate([z, w], axis=-1)], axis=axis)

    n_pair = CMP_STRIDE // 2
    pe = jnp.tile(jnp.stack([cmp_pe_k, cmp_pe_v]), (1, 1, 2)).reshape(2, 2, n_pair, 4 * dh)
    w1c = pair_diag(jnp.stack([cmp_w1_k, cmp_w1_v]).astype(BF16).reshape(2, CMP_BLOCK, dh, -1), 2)
    w1c = w1c.reshape(2, 2, n_pair, 4 * dh, w1c.shape[-1])
    w2c = pair_diag(jnp.stack([cmp_w2_k, cmp_w2_v]).astype(BF16), 1)
    kv_cmp = _compress(kvc, pe, w1c, w2c)

    o_cmp, sel_bias = _cmp_select(qt, kv_cmp)
    o_att = _attention(qt, o_cmp, sel_bias, ks, kw, vst, vwt, glt)

    w_cat = jnp.concatenate([w_rg_a, w_rg_x], axis=-1).astype(BF16)
    b_cat = jnp.concatenate([b_rg_a.reshape(RNN_BLOCKS, 1, RNN_BLOCK_DIM),
                             b_rg_x.reshape(RNN_BLOCKS, 1, RNN_BLOCK_DIM)], axis=-1)
    o_rnn = _rglru(r_in, conv_w, conv_b.reshape(1, -1), w_cat, b_cat, lru_lambda.reshape(1, -1))

    x1 = _outproj(o_att, o_rnn, g_grp_att, g_grp_rnn, w_out.astype(BF16), x, mod6, g_post_mix)
    return _mlp(x1, g_pre_mlp, mod6, w_ff1.astype(BF16), w_ff2.astype(BF16), g_post_mlp)


def kernel(x, c, w_ada, b_ada, g_pre_mix, g_post_mix, g_pre_mlp, g_post_mlp, w_in, cmp_w1_k, cmp_w2_k, cmp_pe_k, cmp_w1_v, cmp_w2_v, cmp_pe_v, conv_w, conv_b, w_rg_a, b_rg_a, w_rg_x, b_rg_x, lru_lambda, g_grp_att, g_grp_rnn, w_out, w_ff1, w_ff2):
    depth = w_ada.shape[0]
    for l in range(depth):
        x = _layer(x, c, w_ada[l], b_ada[l], g_pre_mix[l], g_post_mix[l], g_pre_mlp[l], g_post_mlp[l],
                   _w_in_prep(w_in, l), cmp_w1_k[l], cmp_w2_k[l], cmp_pe_k[l], cmp_w1_v[l], cmp_w2_v[l], cmp_pe_v[l],
                   conv_w[l], conv_b[l], w_rg_a[l], b_rg_a[l], w_rg_x[l], b_rg_x[l], lru_lambda[l],
                   g_grp_att[l], g_grp_rnn[l], w_out[l], w_ff1[l], w_ff2[l])
    return x
```
